```python
import math
import jax, jax.numpy as jnp
from jax import lax
import numpy as np

D_MODEL = 4096
BATCH = 8
SEQ = 2048
DEPTH = 2
DEC_BATCH = 16
DEC_SEQ = 16
PAST_LEN = 2048

CHUNK = 64
QBLOCK = 128
NORM_EPS = 1e-6
NEG_INF = -1e30

S5_WIDTH = 1024
S5_GROUP_CH = 16
S5_GROUPS = S5_WIDTH // S5_GROUP_CH
S5_STATE = 64
MLA_HEADS = 24
MLA_Q_RANK = 1024
MLA_KV_RANK = 512
MLA_NOPE = 128
MLA_ROPE = 64
MLA_V = 128
MLA_SCALE = (MLA_NOPE + MLA_ROPE) ** -0.5
ROPE_THETA = 10000.0
IN0_WIDTH = S5_WIDTH + MLA_Q_RANK + MLA_KV_RANK + MLA_ROPE
MIX0_WIDTH = S5_WIDTH + MLA_HEADS * MLA_V

RWKV_HEADS = 32
RWKV_HEAD = 64
RWKV_WIDTH = RWKV_HEADS * RWKV_HEAD
RWKV_DECAY_LORA = 96
RWKV_A_LORA = 96
RWKV_GATE_LORA = 256
RWKV_SHIFT_WIDTH = 3 * RWKV_WIDTH + RWKV_DECAY_LORA + RWKV_A_LORA + RWKV_GATE_LORA
RWKV_LN_EPS = 64e-5
FOX_HEADS = 16
FOX_HEAD = 128
FOX_WIDTH = FOX_HEADS * FOX_HEAD
FOX_SCALE = FOX_HEAD ** -0.5
FOX_FORGET_BIAS = 2.0
IN1_WIDTH = RWKV_SHIFT_WIDTH + 3 * FOX_WIDTH + FOX_HEADS
MIX1_WIDTH = RWKV_WIDTH + FOX_WIDTH

D_FF = 11008
N_EXPERTS = 8
TOP_K = 2
D_FF_EXPERT = 14336

kernel_name = 'hybrid_s5_mla_rwkv7_fox_stream_step'


def rmsnorm(x, g):
    xf = x.astype(jnp.float32)
    y = xf * lax.rsqrt(jnp.mean(xf * xf, axis=-1, keepdims=True) + NORM_EPS)
    return (y * g.astype(jnp.float32)).astype(x.dtype)


def swiglu(x, w_gate, w_up, w_down):
    return (jax.nn.silu(x @ w_gate) * (x @ w_up)) @ w_down


def moe_swiglu(x, w_router, w_gate, w_up, w_down):
    bsz, t, d = x.shape
    xt = x.reshape(bsz * t, d)
    logits = (xt @ w_router).astype(jnp.float32)
    top_val, top_idx = lax.top_k(logits, TOP_K)
    gates = jax.nn.softmax(top_val, axis=-1)
    combine = jnp.sum(jax.nn.one_hot(top_idx, N_EXPERTS, dtype=jnp.float32) * gates[..., None], axis=1)
    out = jnp.zeros((bsz * t, d), jnp.float32)
    for e in range(N_EXPERTS):
        he = jax.nn.silu(xt @ w_gate[e]) * (xt @ w_up[e])
        out = out + combine[:, e:e + 1] * (he @ w_down[e]).astype(jnp.float32)
    return out.astype(x.dtype).reshape(bsz, t, d)


def rope_angles(pos):
    inv = ROPE_THETA ** (-jnp.arange(0, MLA_ROPE, 2, dtype=jnp.float32) / MLA_ROPE)
    ang = pos.astype(jnp.float32)[:, None] * inv[None, :]
    return jnp.cos(ang), jnp.sin(ang)


def apply_rope(x, cos, sin):
    xf = x.astype(jnp.float32)
    x1, x2 = xf[..., :MLA_ROPE // 2], xf[..., MLA_ROPE // 2:]
    return jnp.concatenate([x1 * cos - x2 * sin, x1 * sin + x2 * cos], axis=-1).astype(x.dtype)


def _complex_scan_combine(e1, e2):
    a1r, a1i, b1r, b1i = e1
    a2r, a2i, b2r, b2i = e2
    return (a1r * a2r - a1i * a2i, a1r * a2i + a1i * a2r,
            a2r * b1r - a2i * b1i + b2r, a2r * b1i + a2i * b1r + b2i)


def s5_mixer(u, h0_re, h0_im, a_re, a_im, log_dt, b_re, b_im, c_re, c_im, d_skip, w_glu, b_glu):
    bsz, t = u.shape[:2]
    uf = u.astype(jnp.float32)
    ug = uf.reshape(bsz, t, S5_GROUPS, S5_GROUP_CH)
    dt = jnp.exp(log_dt.astype(jnp.float32))[:, None]
    ar = a_re.astype(jnp.float32)
    ai = a_im.astype(jnp.float32)
    mag = jnp.exp(ar * dt)
    lr = mag * jnp.cos(ai * dt)
    li = mag * jnp.sin(ai * dt)
    den = ar * ar + ai * ai
    fr = ((lr - 1.0) * ar + li * ai) / den
    fi = (li * ar - (lr - 1.0) * ai) / den
    br = b_re.astype(jnp.float32)
    bi = b_im.astype(jnp.float32)
    bbr = fr[..., None] * br - fi[..., None] * bi
    bbi = fr[..., None] * bi + fi[..., None] * br
    bur = jnp.einsum('btgn,gpn->btgp', ug, bbr)
    bui = jnp.einsum('btgn,gpn->btgp', ug, bbi)
    shape_a = (1, t, S5_GROUPS, S5_STATE)
    acr, aci, xr, xi = lax.associative_scan(
        _complex_scan_combine,
        (jnp.broadcast_to(lr, shape_a), jnp.broadcast_to(li, shape_a), bur, bui), axis=1)
    h0r = h0_re.astype(jnp.float32)[:, None]
    h0i = h0_im.astype(jnp.float32)[:, None]
    xr = xr + acr * h0r - aci * h0i
    xi = xi + acr * h0i + aci * h0r
    y = (jnp.einsum('btgp,gnp->btgn', xr, c_re.astype(jnp.float32))
         - jnp.einsum('btgp,gnp->btgn', xi, c_im.astype(jnp.float32)))
    y = y.reshape(bsz, t, S5_WIDTH) + d_skip.astype(jnp.float32) * uf
    y = jax.nn.gelu(y)
    y = y * jax.nn.sigmoid(y @ w_glu.astype(jnp.float32) + b_glu.astype(jnp.float32))
    return y.astype(u.dtype), xr[:, -1], xi[:, -1]


def mla_core(q_nope, q_pe, ckv, kpe, w_uk, w_uv, mask):
    q_abs = jnp.einsum('bqhn,lhn->bqhl', q_nope, w_uk)
    s = (jnp.einsum('bqhl,bkl->bhqk', q_abs, ckv)
         + jnp.einsum('bqhr,bkr->bhqk', q_pe, kpe)).astype(jnp.float32) * MLA_SCALE
    if mask is not None:
        s = jnp.where(mask, s, NEG_INF)
    p = jax.nn.softmax(s, axis=-1).astype(ckv.dtype)
    o_lat = jnp.einsum('bhqk,bkl->bqhl', p, ckv)
    return jnp.einsum('bqhl,lhv->bqhv', o_lat, w_uv)


def mla_prompt_attention(q_nope, q_pe, ckv, kpe, w_uk, w_uv):
    bsz, t = q_nope.shape[:2]
    kchunk = jnp.arange(t) // CHUNK

    def block(i):
        start = i * QBLOCK
        qn = lax.dynamic_slice_in_dim(q_nope, start, QBLOCK, axis=1)
        qp = lax.dynamic_slice_in_dim(q_pe, start, QBLOCK, axis=1)
        qchunk = (start + jnp.arange(QBLOCK)) // CHUNK
        mask = kchunk[None, :] <= qchunk[:, None]
        return mla_core(qn, qp, ckv, kpe, w_uk, w_uv, mask[None, None])

    out = lax.map(block, jnp.arange(t // QBLOCK))
    return jnp.moveaxis(out, 0, 1).reshape(bsz, t, MLA_HEADS, MLA_V)


def rwkv_scan(s0, r, log_w, k, v, kk, a):
    def step(s, inp):
        r_t, lw_t, k_t, v_t, kk_t, a_t = inp
        sa = jnp.einsum('bhvk,bhk->bhv', s, -kk_t)
        s = (s * jnp.exp(lw_t)[:, :, None, :] + sa[..., None] * (kk_t * a_t)[:, :, None, :]
             + v_t[..., None] * k_t[:, :, None, :])
        return s, jnp.einsum('bhvk,bhk->bhv', s, r_t)

    xs = tuple(jnp.moveaxis(z, 1, 0) for z in (r, log_w, k, v, kk, a))
    s, ys = lax.scan(step, s0, xs)
    return jnp.moveaxis(ys, 0, 1), s


def rwkv_mixer(zr, shift_prev, s0, mu, w0, w2, a0, a2, g2, k_k, k_a, r_k, ln_w, ln_b):
    bsz, t = zr.shape[:2]
    z = zr.astype(jnp.float32)
    z_prev = jnp.concatenate([shift_prev.astype(jnp.float32)[:, None], z[:, :-1]], axis=1)
    zm = z + (z_prev - z) * mu.astype(jnp.float32)
    w = RWKV_WIDTH
    r, k, v, zw, za, zg = jnp.split(
        zm, [w, 2 * w, 3 * w, 3 * w + RWKV_DECAY_LORA, 3 * w + RWKV_DECAY_LORA + RWKV_A_LORA], axis=-1)
    w_raw = -jax.nn.softplus(-(w0.astype(jnp.float32) + jnp.tanh(zw) @ w2.astype(jnp.float32))) - 0.5
    log_w = -jnp.exp(w_raw)
    a = jax.nn.sigmoid(a0.astype(jnp.float32) + za @ a2.astype(jnp.float32))
    g = jax.nn.sigmoid(zg) @ g2.astype(jnp.float32)

    def heads(x):
        return x.reshape(bsz, t, RWKV_HEADS, RWKV_HEAD)

    r, k, v, log_w, a = heads(r), heads(k), heads(v), heads(log_w), heads(a)
    kk = k * k_k.astype(jnp.float32)
    kk = kk / jnp.maximum(jnp.sqrt(jnp.sum(kk * kk, axis=-1, keepdims=True)), 1e-12)
    k = k * (1.0 + (a - 1.0) * k_a.astype(jnp.float32))
    y, s = rwkv_scan(s0.astype(jnp.float32), r, log_w, k, v, kk, a)
    mean = jnp.mean(y, axis=-1, keepdims=True)
    var = jnp.mean(jnp.square(y - mean), axis=-1, keepdims=True)
    y = ((y - mean) * lax.rsqrt(var + RWKV_LN_EPS)).reshape(bsz, t, w) * ln_w.astype(jnp.float32) + ln_b.astype(jnp.float32)
    bonus = (jnp.sum(r * k * r_k.astype(jnp.float32), axis=-1, keepdims=True) * v).reshape(bsz, t, w)
    y = (y + bonus) * g
    return y.astype(zr.dtype), s, zr[:, -1]


def fox_core(q, k, v, fq, fk, mask):
    s = (jnp.einsum('bqhd,bkhd->bhqk', q, k).astype(jnp.float32) * FOX_SCALE
         + fq[..., :, None] - fk[..., None, :])
    s = jnp.where(mask, s, NEG_INF)
    p = jax.nn.softmax(s, axis=-1).astype(v.dtype)
    return jnp.einsum('bhqk,bkhd->bqhd', p, v)


def fox_prompt_attention(q, k, v, cum_f):
    bsz, t = q.shape[:2]
    kidx = jnp.arange(t)

    def block(i):
        start = i * QBLOCK
        qb = lax.dynamic_slice_in_dim(q, start, QBLOCK, axis=1)
        fq = lax.dynamic_slice_in_dim(cum_f, start, QBLOCK, axis=2)
        qidx = start + jnp.arange(QBLOCK)
        mask = kidx[None, :] <= qidx[:, None]
        return fox_core(qb, k, v, fq, cum_f, mask[None, None])

    out = lax.map(block, jnp.arange(t // QBLOCK))
    return jnp.moveaxis(out, 0, 1).reshape(bsz, t, FOX_HEADS, FOX_HEAD)


def fox_mixer(zq, zk, zv, zf, b_f, past_k, past_v, past_logf):
    bsz, t = zq.shape[:2]
    q = zq.reshape(bsz, t, FOX_HEADS, FOX_HEAD)
    k = zk.reshape(bsz, t, FOX_HEADS, FOX_HEAD)
    v = zv.reshape(bsz, t, FOX_HEADS, FOX_HEAD)
    logf = jax.nn.log_sigmoid(zf.astype(jnp.float32) + b_f.astype(jnp.float32))
    if past_k is None:
        cum_f = jnp.transpose(jnp.cumsum(logf, axis=1), (0, 2, 1))
        y = fox_prompt_attention(q, k, v, cum_f)
    else:
        p = past_k.shape[1]
        k_all = jnp.concatenate([past_k.astype(k.dtype), k], axis=1)
        v_all = jnp.concatenate([past_v.astype(v.dtype), v], axis=1)
        cum_f = jnp.transpose(jnp.cumsum(jnp.concatenate([past_logf.astype(jnp.float32), logf], axis=1), axis=1), (0, 2, 1))
        mask = jnp.arange(p + t)[None, :] <= (p + jnp.arange(t))[:, None]
        y = fox_core(q, k_all, v_all, cum_f[:, :, p:], cum_f, mask[None, None])
    return y.reshape(bsz, t, FOX_WIDTH), k, v, logf.astype(zq.dtype)


def even_layer(h, pos, past_ckv, past_kpe, s5_h0_re, s5_h0_im,
               ln_mix, w_in, s5_a_re, s5_a_im, s5_log_dt, s5_b_re, s5_b_im, s5_c_re, s5_c_im, s5_d,
               s5_w_glu, s5_b_glu, mla_q_norm, mla_w_q_up, mla_kv_norm, mla_w_uk, mla_w_uv, w_out,
               ln_ffn, ffn_w_gate, ffn_w_up, ffn_w_down):
    bsz, t = h.shape[:2]
    z = rmsnorm(h, ln_mix) @ w_in
    u, cq, ckv_raw, kpe_raw = jnp.split(
        z, [S5_WIDTH, S5_WIDTH + MLA_Q_RANK, S5_WIDTH + MLA_Q_RANK + MLA_KV_RANK], axis=-1)
    y_s5, s5_re, s5_im = s5_mixer(u, s5_h0_re, s5_h0_im, s5_a_re, s5_a_im, s5_log_dt, s5_b_re, s5_b_im,
                                  s5_c_re, s5_c_im, s5_d, s5_w_glu, s5_b_glu)
    cos, sin = rope_angles(pos)
    q = jnp.einsum('btc,chd->bthd', rmsnorm(cq, mla_q_norm), mla_w_q_up)
    q_nope = q[..., :MLA_NOPE]
    q_pe = apply_rope(q[..., MLA_NOPE:], cos[None, :, None, :], sin[None, :, None, :])
    ckv = rmsnorm(ckv_raw, mla_kv_norm)
    kpe = apply_rope(kpe_raw, cos[None], sin[None])
    if past_ckv is None:
        y_mla = mla_prompt_attention(q_nope, q_pe, ckv, kpe, mla_w_uk, mla_w_uv)
    else:
        ckv_all = jnp.concatenate([past_ckv.astype(ckv.dtype), ckv], axis=1)
        kpe_all = jnp.concatenate([past_kpe.astype(kpe.dtype), kpe], axis=1)
        y_mla = mla_core(q_nope, q_pe, ckv_all, kpe_all, mla_w_uk, mla_w_uv, None)
    mix = jnp.concatenate([y_s5, y_mla.reshape(bsz, t, MLA_HEADS * MLA_V)], axis=-1)
    h = h + mix @ w_out
    h = h + swiglu(rmsnorm(h, ln_ffn), ffn_w_gate, ffn_w_up, ffn_w_down)
    return h, ckv, kpe, s5_re.astype(h.dtype), s5_im.astype(h.dtype)


def odd_layer(h, shift_prev, wkv0, past_k, past_v, past_logf,
              ln_mix, w_in, rwkv_mu, rwkv_w0, rwkv_w2, rwkv_a0, rwkv_a2, rwkv_g2, rwkv_k_k, rwkv_k_a,
              rwkv_r_k, rwkv_ln_w, rwkv_ln_b, fox_b_f, w_out, ln_ffn, moe_w_router, moe_w_gate,
              moe_w_up, moe_w_down):
    z = rmsnorm(h, ln_mix) @ w_in
    zr, zq, zk, zv, zf = jnp.split(
        z, [RWKV_SHIFT_WIDTH, RWKV_SHIFT_WIDTH + FOX_WIDTH, RWKV_SHIFT_WIDTH + 2 * FOX_WIDTH,
            RWKV_SHIFT_WIDTH + 3 * FOX_WIDTH], axis=-1)
    y_r, wkv, shift_new = rwkv_mixer(zr, shift_prev, wkv0, rwkv_mu, rwkv_w0, rwkv_w2, rwkv_a0, rwkv_a2,
                                     rwkv_g2, rwkv_k_k, rwkv_k_a, rwkv_r_k, rwkv_ln_w, rwkv_ln_b)
    y_f, fk, fv, flogf = fox_mixer(zq, zk, zv, zf, fox_b_f, past_k, past_v, past_logf)
    mix = jnp.concatenate([y_r, y_f], axis=-1)
    h = h + mix @ w_out
    h = h + moe_swiglu(rmsnorm(h, ln_ffn), moe_w_router, moe_w_gate, moe_w_up, moe_w_down)
    return h, wkv.astype(h.dtype), shift_new, fk, fv, flogf


def setup_inputs(seed: int = 0) -> dict:
    key = jax.random.key(seed)
    ks = list(jax.random.split(key, 96))
    f32 = jnp.float32

    def nrm(shape, scale):
        return jax.random.normal(ks.pop(), shape, f32) * scale

    d = D_MODEL
    g, p = S5_GROUPS, S5_STATE
    inputs = {}
    inputs['x_prompt'] = nrm((BATCH, SEQ, d), 1.0)
    inputs['x_sample'] = nrm((DEC_BATCH, DEC_SEQ, d), 1.0)
    inputs['cache_mla_ckv'] = nrm((DEC_BATCH, PAST_LEN, MLA_KV_RANK), 1.0)
    inputs['cache_mla_kpe'] = nrm((DEC_BATCH, PAST_LEN, MLA_ROPE), 1.0)
    inputs['state_s5_re'] = nrm((DEC_BATCH, g, p), 0.5)
    inputs['state_s5_im'] = nrm((DEC_BATCH, g, p), 0.5)
    inputs['state_rwkv_wkv'] = nrm((DEC_BATCH, RWKV_HEADS, RWKV_HEAD, RWKV_HEAD), 0.3)
    inputs['state_rwkv_shift'] = nrm((DEC_BATCH, RWKV_SHIFT_WIDTH), 1.0)
    inputs['cache_fox_k'] = nrm((DEC_BATCH, PAST_LEN, FOX_HEADS, FOX_HEAD), 1.0)
    inputs['cache_fox_v'] = nrm((DEC_BATCH, PAST_LEN, FOX_HEADS, FOX_HEAD), 1.0)
    inputs['cache_fox_logf'] = jax.nn.log_sigmoid(FOX_FORGET_BIAS + nrm((DEC_BATCH, PAST_LEN, FOX_HEADS), 1.0))
    inputs['ln0_mix'] = 1.0 + nrm((d,), 0.01)
    inputs['w_in0'] = nrm((d, IN0_WIDTH), d ** -0.5)
    inputs['s5_a_re'] = -0.5 + nrm((g, p), 0.01)
    inputs['s5_a_im'] = jnp.pi * jnp.arange(p, dtype=f32)[None, :] + nrm((g, p), 0.01)
    inputs['s5_log_dt'] = jax.random.uniform(ks.pop(), (g,), f32, math.log(1e-3), math.log(1e-1))
    inputs['s5_b_re'] = nrm((g, p, S5_GROUP_CH), (2 * S5_GROUP_CH) ** -0.5)
    inputs['s5_b_im'] = nrm((g, p, S5_GROUP_CH), (2 * S5_GROUP_CH) ** -0.5)
    inputs['s5_c_re'] = nrm((g, S5_GROUP_CH, p), p ** -0.5)
    inputs['s5_c_im'] = nrm((g, S5_GROUP_CH, p), p ** -0.5)
    inputs['s5_d'] = nrm((S5_WIDTH,), 1.0)
    inputs['s5_w_glu'] = nrm((S5_WIDTH, S5_WIDTH), S5_WIDTH ** -0.5)
    inputs['s5_b_glu'] = nrm((S5_WIDTH,), 0.01)
    inputs['mla_q_norm'] = 1.0 + nrm((MLA_Q_RANK,), 0.01)
    inputs['mla_w_q_up'] = nrm((MLA_Q_RANK, MLA_HEADS, MLA_NOPE + MLA_ROPE), MLA_Q_RANK ** -0.5)
    inputs['mla_kv_norm'] = 1.0 + nrm((MLA_KV_RANK,), 0.01)
    inputs['mla_w_uk'] = nrm((MLA_KV_RANK, MLA_HEADS, MLA_NOPE), MLA_KV_RANK ** -0.5)
    inputs['mla_w_uv'] = nrm((MLA_KV_RANK, MLA_HEADS, MLA_V), MLA_KV_RANK ** -0.5)
    inputs['w_out0'] = nrm((MIX0_WIDTH, d), MIX0_WIDTH ** -0.5)
    inputs['ln0_ffn'] = 1.0 + nrm((d,), 0.01)
    inputs['ffn_w_gate'] = nrm((d, D_FF), d ** -0.5)
    inputs['ffn_w_up'] = nrm((d, D_FF), d ** -0.5)
    inputs['ffn_w_down'] = nrm((D_FF, d), D_FF ** -0.5)
    inputs['ln1_mix'] = 1.0 + nrm((d,), 0.01)
    inputs['w_in1'] = nrm((d, IN1_WIDTH), d ** -0.5)
    inputs['rwkv_mu'] = jax.random.uniform(ks.pop(), (RWKV_SHIFT_WIDTH,), f32)
    inputs['rwkv_w0'] = jnp.linspace(-6.5, -1.5, RWKV_WIDTH, dtype=f32) + nrm((RWKV_WIDTH,), 0.1)
    inputs['rwkv_w2'] = nrm((RWKV_DECAY_LORA, RWKV_WIDTH), 0.1 * RWKV_DECAY_LORA ** -0.5)
    inputs['rwkv_a0'] = nrm((RWKV_WIDTH,), 0.1)
    inputs['rwkv_a2'] = nrm((RWKV_A_LORA, RWKV_WIDTH), 0.1 * RWKV_A_LORA ** -0.5)
    inputs['rwkv_g2'] = nrm((RWKV_GATE_LORA, RWKV_WIDTH), RWKV_GATE_LORA ** -0.5)
    inputs['rwkv_k_k'] = 0.85 + nrm((RWKV_HEADS, RWKV_HEAD), 0.01)
    inputs['rwkv_k_a'] = 1.0 + nrm((RWKV_HEADS, RWKV_HEAD), 0.01)
    inputs['rwkv_r_k'] = nrm((RWKV_HEADS, RWKV_HEAD), 0.1)
    inputs['rwkv_ln_w'] = 1.0 + nrm((RWKV_WIDTH,), 0.01)
    inputs['rwkv_ln_b'] = nrm((RWKV_WIDTH,), 0.01)
    inputs['fox_b_f'] = FOX_FORGET_BIAS + nrm((FOX_HEADS,), 0.5)
    inputs['w_out1'] = nrm((MIX1_WIDTH, d), MIX1_WIDTH ** -0.5)
    inputs['ln1_ffn'] = 1.0 + nrm((d,), 0.01)
    inputs['moe_w_router'] = nrm((d, N_EXPERTS), d ** -0.5)
    inputs['moe_w_gate'] = nrm((N_EXPERTS, d, D_FF_EXPERT), d ** -0.5)
    inputs['moe_w_up'] = nrm((N_EXPERTS, d, D_FF_EXPERT), d ** -0.5)
    inputs['moe_w_down'] = nrm((N_EXPERTS, D_FF_EXPERT, d), D_FF_EXPERT ** -0.5)
    inputs['final_norm'] = 1.0 + nrm((d,), 0.01)
    return inputs


def reference(x_prompt, x_sample, cache_mla_ckv, cache_mla_kpe, state_s5_re, state_s5_im,
              state_rwkv_wkv, state_rwkv_shift, cache_fox_k, cache_fox_v, cache_fox_logf,
              ln0_mix, w_in0, s5_a_re, s5_a_im, s5_log_dt, s5_b_re, s5_b_im, s5_c_re, s5_c_im, s5_d,
              s5_w_glu, s5_b_glu, mla_q_norm, mla_w_q_up, mla_kv_norm, mla_w_uk, mla_w_uv, w_out0,
              ln0_ffn, ffn_w_gate, ffn_w_up, ffn_w_down,
              ln1_mix, w_in1, rwkv_mu, rwkv_w0, rwkv_w2, rwkv_a0, rwkv_a2, rwkv_g2, rwkv_k_k, rwkv_k_a,
              rwkv_r_k, rwkv_ln_w, rwkv_ln_b, fox_b_f, w_out1, ln1_ffn, moe_w_router, moe_w_gate,
              moe_w_up, moe_w_down, final_norm):
    even_w = (ln0_mix, w_in0, s5_a_re, s5_a_im, s5_log_dt, s5_b_re, s5_b_im, s5_c_re, s5_c_im, s5_d,
              s5_w_glu, s5_b_glu, mla_q_norm, mla_w_q_up, mla_kv_norm, mla_w_uk, mla_w_uv, w_out0,
              ln0_ffn, ffn_w_gate, ffn_w_up, ffn_w_down)
    odd_w = (ln1_mix, w_in1, rwkv_mu, rwkv_w0, rwkv_w2, rwkv_a0, rwkv_a2, rwkv_g2, rwkv_k_k, rwkv_k_a,
             rwkv_r_k, rwkv_ln_w, rwkv_ln_b, fox_b_f, w_out1, ln1_ffn, moe_w_router, moe_w_gate,
             moe_w_up, moe_w_down)
    bp, tp = x_prompt.shape[:2]
    ts = x_sample.shape[1]
    past = cache_mla_ckv.shape[1]
    pos_p = jnp.arange(tp)
    pos_s = past + jnp.arange(ts)
    hp, hs = x_prompt, x_sample
    for layer in range(DEPTH):
        if layer % 2 == 0:
            hp, p_ckv, p_kpe, p_s5_re, p_s5_im = even_layer(
                hp, pos_p, None, None,
                jnp.zeros((bp, S5_GROUPS, S5_STATE), jnp.float32),
                jnp.zeros((bp, S5_GROUPS, S5_STATE), jnp.float32), *even_w)
            hs, s_ckv, s_kpe, s_s5_re, s_s5_im = even_layer(
                hs, pos_s, cache_mla_ckv, cache_mla_kpe, state_s5_re, state_s5_im, *even_w)
        else:
            hp, p_wkv, p_shift, p_fk, p_fv, p_flogf = odd_layer(
                hp, jnp.zeros((bp, RWKV_SHIFT_WIDTH), hp.dtype),
                jnp.zeros((bp, RWKV_HEADS, RWKV_HEAD, RWKV_HEAD), jnp.float32), None, None, None, *odd_w)
            hs, s_wkv, s_shift, s_fk, s_fv, s_flogf = odd_layer(
                hs, state_rwkv_shift, state_rwkv_wkv, cache_fox_k, cache_fox_v, cache_fox_logf, *odd_w)
    y_prompt = rmsnorm(hp, final_norm)
    y_sample = rmsnorm(hs, final_norm)
    return (y_prompt, y_sample,
            p_ckv, p_kpe, p_s5_re, p_s5_im, p_wkv, p_shift, p_fk, p_fv, p_flogf,
            s_ckv, s_kpe, s_s5_re, s_s5_im, s_wkv, s_shift, s_fk, s_fv, s_flogf)
```

```python
import functools
import math

import jax
import jax.numpy as jnp
from jax import lax
from jax.experimental import pallas as pl
from jax.experimental.pallas import tpu as pltpu

F32 = jnp.float32
BF16 = jnp.bfloat16

V7X_VMEM_BYTES = 64 * 1024 * 1024
VMEM_CAP = V7X_VMEM_BYTES - 8 * 1024 * 1024
LANES = 128

NORM_EPS = 1e-6
NEG_INF = -1e30
CHUNK = 64
ROPE_THETA = 10000.0
RWKV_LN_EPS = 64e-5
S5_BLOCK_GROUPS = 8


def _pick(n, cands):
    for c in cands:
        if c <= n and n % c == 0:
            return c
    return n


def _cparams(sem, vmem_bytes):
    limit = int(min(max(vmem_bytes * 1.25 + (4 << 20), 24 << 20), VMEM_CAP))
    return pltpu.CompilerParams(dimension_semantics=sem, vmem_limit_bytes=limit)


def _rmsnorm_kernel(x_ref, g_ref, *o_refs):
    x = x_ref[...].astype(F32)
    y = x * lax.rsqrt(jnp.mean(x * x, axis=-1, keepdims=True) + NORM_EPS)
    y = y * g_ref[...]
    for o in o_refs:
        o[...] = y.astype(o.dtype)


def rmsnorm(x, g, out_dtypes, col_block=0, width=None):
    m = x.shape[0]
    width = x.shape[1] if width is None else width
    tm = _pick(m, (512, 320, 256, 128, 64, 32, 16, 8))
    outs = tuple(jax.ShapeDtypeStruct((m, width), d) for d in out_dtypes)
    res = pl.pallas_call(
        _rmsnorm_kernel,
        grid=(m // tm,),
        in_specs=[pl.BlockSpec((tm, width), lambda i: (i, col_block)),
                  pl.BlockSpec((1, width), lambda i: (0, 0))],
        out_specs=tuple(pl.BlockSpec((tm, width), lambda i: (i, 0)) for _ in out_dtypes),
        out_shape=outs,
        compiler_params=_cparams(("parallel",), tm * width * 4 * 2 * (1 + len(out_dtypes))),
        name="rmsnorm",
    )(x, g.reshape(1, width).astype(F32))
    return res


def _mm_kernel(*refs, nk, has_res, has_scale, n_out):
    x_ref, w_ref = refs[0], refs[1]
    pos = 2
    res_ref = scale_ref = None
    if has_res:
        res_ref = refs[pos]
        pos += 1
    if has_scale:
        scale_ref = refs[pos]
        pos += 1
    o_refs = refs[pos:pos + n_out]
    acc_ref = refs[pos + n_out] if nk > 1 else None

    part = jnp.dot(x_ref[...].astype(BF16), w_ref[...].astype(BF16), preferred_element_type=F32)

    def finish(acc):
        if has_scale:
            acc = acc * scale_ref[...]
        if has_res:
            acc = res_ref[...] + acc
        for o in o_refs:
            o[...] = acc.astype(o.dtype)

    if nk == 1:
        finish(part)
    else:
        k = pl.program_id(2)

        @pl.when(k == 0)
        def _():
            acc_ref[...] = part

        @pl.when(k > 0)
        def _():
            acc_ref[...] += part

        @pl.when(k == nk - 1)
        def _():
            finish(acc_ref[...])


def matmul(x, w, out_dtypes=(F32,), res=None, scale=None, w_index=None, tm=None, tn=None, tk=None):
    m, kdim = x.shape
    n = w.shape[-1]
    tm = tm or _pick(m, (1280, 1024, 640, 512, 320, 256, 128, 64, 32, 16, 8))
    tn = tn or _pick(n, (512, 384, 256, 128))
    tk = tk or (kdim if kdim <= 4096 else _pick(kdim, (2048, 1792, 1024, 512, 256, 128)))
    nk = kdim // tk
    grid = (m // tm, n // tn, nk)
    in_specs = [pl.BlockSpec((tm, tk), lambda i, j, k: (i, k))]
    if w.ndim == 3:
        in_specs.append(pl.BlockSpec((None, tk, tn), lambda i, j, k: (w_index, k, j)))
    else:
        in_specs.append(pl.BlockSpec((tk, tn), lambda i, j, k: (k, j)))
    args = [x, w]
    if res is not None:
        in_specs.append(pl.BlockSpec((tm, tn), lambda i, j, k: (i, j)))
        args.append(res)
    if scale is not None:
        in_specs.append(pl.BlockSpec((tm, 1), lambda i, j, k: (i, 0)))
        args.append(scale)
    out_specs = tuple(pl.BlockSpec((tm, tn), lambda i, j, k: (i, j)) for _ in out_dtypes)
    out_shape = tuple(jax.ShapeDtypeStruct((m, n), d) for d in out_dtypes)
    scratch = [pltpu.VMEM((tm, tn), F32)] if nk > 1 else []
    vmem = (2 * tm * tk * x.dtype.itemsize + 2 * tk * tn * w.dtype.itemsize
            + tm * tn * 4 * (2 * len(out_dtypes) + 1 + (2 if res is not None else 0)))
    outs = pl.pallas_call(
        functools.partial(_mm_kernel, nk=nk, has_res=res is not None, has_scale=scale is not None,
                          n_out=len(out_dtypes)),
        grid=grid, in_specs=in_specs, out_specs=out_specs, out_shape=out_shape,
        scratch_shapes=scratch,
        compiler_params=_cparams(("parallel", "parallel", "arbitrary"), vmem),
        name="matmul",
    )(*args)
    return outs


def _rope_mm_kernel(x_ref, w1_ref, w2_ref, c_ref, s_ref, o_ref):
    x = x_ref[...]
    a = jnp.dot(x, w1_ref[...], preferred_element_type=F32)
    b = jnp.dot(x, w2_ref[...], preferred_element_type=F32)
    o_ref[...] = (a * c_ref[...] + b * s_ref[...]).astype(o_ref.dtype)


def rope_matmul(x, w1, w2, ctab, stab, out_dtype, tn):
    m, kdim = x.shape
    n = w1.shape[1]
    est = lambda rows: 2 * rows * kdim * 2 + 4 * kdim * tn * 2 + 8 * rows * tn * 4
    tm = next((c for c in (1280, 1024, 640, 512, 320, 256, 128, 64, 32, 16, 8)
               if m % c == 0 and est(c) <= VMEM_CAP // 2), 8)
    vmem = est(tm)
    return pl.pallas_call(
        _rope_mm_kernel,
        grid=(m // tm, n // tn),
        in_specs=[pl.BlockSpec((tm, kdim), lambda i, j: (i, 0)),
                  pl.BlockSpec((kdim, tn), lambda i, j: (0, j)),
                  pl.BlockSpec((kdim, tn), lambda i, j: (0, j)),
                  pl.BlockSpec((tm, tn), lambda i, j: (i, 0)),
                  pl.BlockSpec((tm, tn), lambda i, j: (i, 0))],
        out_specs=pl.BlockSpec((tm, tn), lambda i, j: (i, j)),
        out_shape=jax.ShapeDtypeStruct((m, n), out_dtype),
        compiler_params=_cparams(("parallel", "parallel"), vmem),
        name="rope_matmul",
    )(x, w1, w2, ctab, stab)


def _swiglu_up_kernel(x_ref, wg_ref, wu_ref, o_ref):
    x = x_ref[...]
    g = jnp.dot(x, wg_ref[...].astype(BF16), preferred_element_type=F32)
    u = jnp.dot(x, wu_ref[...].astype(BF16), preferred_element_type=F32)
    o_ref[...] = (g * jax.nn.sigmoid(g) * u).astype(o_ref.dtype)


def swiglu_up(x, wg, wu, w_index=None):
    m, kdim = x.shape
    n = wg.shape[-1]
    tm = _pick(m, (1280, 1024, 640, 512, 320, 256, 128, 64, 32, 16, 8))
    tn = _pick(n, (256, 128))
    if wg.ndim == 3:
        wspec = pl.BlockSpec((None, kdim, tn), lambda i, j: (w_index, 0, j))
    else:
        wspec = pl.BlockSpec((kdim, tn), lambda i, j: (0, j))
    vmem = 2 * tm * kdim * 2 + 4 * kdim * tn * wg.dtype.itemsize + 6 * tm * tn * 4
    return pl.pallas_call(
        _swiglu_up_kernel,
        grid=(m // tm, n // tn),
        in_specs=[pl.BlockSpec((tm, kdim), lambda i, j: (i, 0)), wspec, wspec],
        out_specs=pl.BlockSpec((tm, tn), lambda i, j: (i, j)),
        out_shape=jax.ShapeDtypeStruct((m, n), BF16),
        compiler_params=_cparams(("parallel", "parallel"), vmem),
        name="swiglu_up",
    )(x, wg, wu)


def _s5_kernel(u_ref, h0r_ref, h0i_ref, lr_ref, li_ref, bbr_ref, bbi_ref, ccr_ref, cci_ref,
               d_ref, wglu_ref, bglu_ref, y_ref, hr_ref, hi_ref, xr_s, xi_s, st_r, st_i, *, tc, nblk):
    c = pl.program_id(1)

    @pl.when(c == 0)
    def _():
        st_r[...] = h0r_ref[0]
        st_i[...] = h0i_ref[0]

    u = u_ref[...]
    ub = u.astype(BF16)
    sw = xr_s.shape[1] // nblk
    for k in range(nblk):
        uk = ub[:, k * LANES:(k + 1) * LANES]
        xr_s[:, k * sw:(k + 1) * sw] = jnp.dot(uk, bbr_ref[k], preferred_element_type=F32)
        xi_s[:, k * sw:(k + 1) * sw] = jnp.dot(uk, bbi_ref[k], preferred_element_type=F32)

    scan_w = 1024
    for q in range(xr_s.shape[1] // scan_w):
        cols = slice(q * scan_w, (q + 1) * scan_w)
        lr = lr_ref[:, cols]
        li = li_ref[:, cols]

        def body(t, carry, cols=cols, lr=lr, li=li):
            hr, hi = carry
            nr = lr * hr - li * hi + xr_s[pl.ds(t, 1), cols]
            ni = lr * hi + li * hr + xi_s[pl.ds(t, 1), cols]
            xr_s[pl.ds(t, 1), cols] = nr
            xi_s[pl.ds(t, 1), cols] = ni
            return nr, ni

        hr, hi = lax.fori_loop(0, tc, body, (st_r[:, cols], st_i[:, cols]))
        st_r[:, cols] = hr
        st_i[:, cols] = hi

    hr_ref[0] = st_r[...]
    hi_ref[0] = st_i[...]

    ys = []
    for k in range(nblk):
        xr = xr_s[:, k * sw:(k + 1) * sw].astype(BF16)
        xi = xi_s[:, k * sw:(k + 1) * sw].astype(BF16)
        ys.append(jnp.dot(xr, ccr_ref[k], preferred_element_type=F32)
                  - jnp.dot(xi, cci_ref[k], preferred_element_type=F32))
    y = jnp.concatenate(ys, axis=1) + d_ref[...] * u
    y = jax.nn.gelu(y)
    gate = jax.nn.sigmoid(jnp.dot(y.astype(BF16), wglu_ref[...], preferred_element_type=F32) + bglu_ref[...])
    y_ref[...] = (y * gate).astype(y_ref.dtype)


def s5_mixer(z, row_off, bsz, t, h0_re, h0_im, prm):
    width = prm["d"].shape[1]
    nblk = width // LANES
    nstate = prm["lr"].shape[1]
    tc = _pick(t, (256, 128, 64, 32, 16, 8))
    nt = t // tc
    rb0 = row_off // tc
    full = lambda shape: pl.BlockSpec(shape, lambda b, c: (0,) * len(shape))
    vmem = (4 * tc * width * 4 + 2 * tc * nstate * 4 + 4 * nblk * LANES * (nstate // nblk) * 2 * 2
            + 2 * width * width * 2 + 8 * tc * width * 4)
    y, hr, hi = pl.pallas_call(
        functools.partial(_s5_kernel, tc=tc, nblk=nblk),
        grid=(bsz, nt),
        in_specs=[pl.BlockSpec((tc, width), lambda b, c: (rb0 + b * nt + c, 0)),
                  pl.BlockSpec((1, 1, nstate), lambda b, c: (b, 0, 0)),
                  pl.BlockSpec((1, 1, nstate), lambda b, c: (b, 0, 0)),
                  full((1, nstate)), full((1, nstate)),
                  full(prm["bbr"].shape), full(prm["bbi"].shape),
                  full(prm["ccr"].shape), full(prm["cci"].shape),
                  full((1, width)), full((width, width)), full((1, width))],
        out_specs=(pl.BlockSpec((tc, width), lambda b, c: (b * nt + c, 0)),
                   pl.BlockSpec((1, 1, nstate), lambda b, c: (b, 0, 0)),
                   pl.BlockSpec((1, 1, nstate), lambda b, c: (b, 0, 0))),
        out_shape=(jax.ShapeDtypeStruct((bsz * t, width), BF16),
                   jax.ShapeDtypeStruct((bsz, 1, nstate), F32),
                   jax.ShapeDtypeStruct((bsz, 1, nstate), F32)),
        scratch_shapes=[pltpu.VMEM((tc, nstate), F32), pltpu.VMEM((tc, nstate), F32),
                        pltpu.VMEM((1, nstate), F32), pltpu.VMEM((1, nstate), F32)],
        compiler_params=_cparams(("parallel", "arbitrary"), vmem),
        name="s5_mixer",
    )(z, h0_re.reshape(bsz, 1, nstate), h0_im.reshape(bsz, 1, nstate), prm["lr"], prm["li"],
      prm["bbr"], prm["bbi"], prm["ccr"], prm["cci"], prm["d"], prm["wglu"], prm["bglu"])
    return y, hr, hi


def s5_params(a_re, a_im, log_dt, b_re, b_im, c_re, c_im, d_skip, w_glu, b_glu):
    g, p = a_re.shape
    nch = b_re.shape[2]
    dt = jnp.exp(log_dt.astype(F32))[:, None]
    ar, ai = a_re.astype(F32), a_im.astype(F32)
    mag = jnp.exp(ar * dt)
    lr = mag * jnp.cos(ai * dt)
    li = mag * jnp.sin(ai * dt)
    den = ar * ar + ai * ai
    fr = ((lr - 1.0) * ar + li * ai) / den
    fi = (li * ar - (lr - 1.0) * ai) / den
    br, bi = b_re.astype(F32), b_im.astype(F32)
    bbr = fr[..., None] * br - fi[..., None] * bi
    bbi = fr[..., None] * bi + fi[..., None] * br
    gb = S5_BLOCK_GROUPS
    nblk = g // gb
    eye = jnp.eye(gb, dtype=F32)

    def blk_in(m):
        m = m.reshape(nblk, gb, p, nch)
        return jnp.einsum("kgpn,gh->kgnhp", m, eye).reshape(nblk, gb * nch, gb * p).astype(BF16)

    def blk_out(m):
        m = m.astype(F32).reshape(nblk, gb, nch, p)
        return jnp.einsum("kgnp,gh->kgphn", m, eye).reshape(nblk, gb * p, gb * nch).astype(BF16)

    width = g * nch
    return dict(lr=lr.reshape(1, g * p), li=li.reshape(1, g * p), bbr=blk_in(bbr), bbi=blk_in(bbi),
                ccr=blk_out(c_re), cci=blk_out(c_im), d=d_skip.astype(F32).reshape(1, width),
                wglu=w_glu.astype(BF16), bglu=b_glu.astype(F32).reshape(1, width))


def _mla_kernel(q_ref, ckv_ref, kpe_ref, wuk_ref, wuv_ref, o_ref, qa_s, m_s, l_s, acc_s,
                *, tq, tk, nk_total, klen, causal, scale):
    i = pl.program_id(1)
    nope = wuk_ref.shape[1]
    qa_s[...] = jnp.dot(q_ref[:, :nope], wuk_ref[0], preferred_element_type=F32)
    m_s[...] = jnp.full(m_s.shape, NEG_INF, F32)
    l_s[...] = jnp.zeros(l_s.shape, F32)
    acc_s[...] = jnp.zeros(acc_s.shape, F32)
    qpos = i * tq + lax.broadcasted_iota(jnp.int32, (tq, tk), 0)

    def body(j, _):
        ks = pl.multiple_of(j * tk, tk)
        ckv = ckv_ref[0, pl.ds(ks, tk), :]
        kpe = kpe_ref[0, pl.ds(ks, tk), :]
        s = lax.dot_general(qa_s[...].astype(BF16), ckv, (((1,), (1,)), ((), ())), preferred_element_type=F32)
        s = s + lax.dot_general(q_ref[:, nope:], kpe, (((1,), (1,)), ((), ())), preferred_element_type=F32)
        s = s * scale
        kpos = ks + lax.broadcasted_iota(jnp.int32, (tq, tk), 1)
        if causal:
            s = jnp.where(kpos // CHUNK <= qpos // CHUNK, s, NEG_INF)
        if klen < nk_total * tk:
            s = jnp.where(kpos < klen, s, NEG_INF)
        m_prev = m_s[...]
        m_new = jnp.maximum(m_prev, jnp.max(s, axis=1, keepdims=True))
        alpha = jnp.exp(m_prev - m_new)
        p = jnp.exp(s - m_new)
        l_s[...] = alpha * l_s[...] + jnp.sum(p, axis=1, keepdims=True)
        acc_s[...] = alpha * acc_s[...] + jnp.dot(p.astype(BF16), ckv, preferred_element_type=F32)
        m_s[...] = m_new
        return 0

    if causal:
        nk = ((i + 1) * tq + tk - 1) // tk
    else:
        nk = nk_total
    lax.fori_loop(0, nk, body, 0)
    o_lat = (acc_s[...] / l_s[...]).astype(BF16)
    o_ref[...] = jnp.dot(o_lat, wuv_ref[0], preferred_element_type=F32).astype(o_ref.dtype)


def mla_attention(q_all, row_off, bsz, t, ckv, kpe, wuk, wuv, klen, causal, scale):
    nheads, nope, lat = wuk.shape
    vdim = wuv.shape[2]
    qw = q_all.shape[1] // nheads
    tkeys = ckv.shape[1]
    tq = _pick(t, (256, 128, 64, 32, 16, 8))
    tk = _pick(tkeys, (256, 128))
    if causal:
        assert tq % CHUNK == 0 and tkeys == t
    nq = t // tq
    rb0 = row_off // tq
    vmem = 2 * tkeys * (lat + LANES) * 2 + 4 * tq * qw * 2 + 2 * tq * lat * 4 + 8 * tq * tk * 4 + (2 << 20)
    return pl.pallas_call(
        functools.partial(_mla_kernel, tq=tq, tk=tk, nk_total=tkeys // tk, klen=klen, causal=causal, scale=scale),
        grid=(bsz, nq, nheads),
        in_specs=[pl.BlockSpec((tq, qw), lambda b, i, h: (rb0 + b * nq + i, h)),
                  pl.BlockSpec((1, tkeys, lat), lambda b, i, h: (b, 0, 0)),
                  pl.BlockSpec((1, tkeys, LANES), lambda b, i, h: (b, 0, 0)),
                  pl.BlockSpec((1, nope, lat), lambda b, i, h: (h, 0, 0)),
                  pl.BlockSpec((1, lat, vdim), lambda b, i, h: (h, 0, 0))],
        out_specs=pl.BlockSpec((tq, vdim), lambda b, i, h: (b * nq + i, h)),
        out_shape=jax.ShapeDtypeStruct((bsz * t, nheads * vdim), BF16),
        scratch_shapes=[pltpu.VMEM((tq, lat), F32), pltpu.VMEM((tq, 1), F32), pltpu.VMEM((tq, 1), F32),
                        pltpu.VMEM((tq, lat), F32)],
        compiler_params=_cparams(("parallel", "parallel", "arbitrary"), vmem),
        name="mla_attention",
    )(q_all, ckv, kpe, wuk, wuv)


def _fox_gate_kernel(pre_ref, zf_ref, bf_ref, logf_ref, cum_ref, cumt_ref, lf_s, *, npre, t, blk):
    total = lf_s.shape[0]
    z = zf_ref[0] + bf_ref[...]
    logf = jnp.minimum(z, 0.0) - jnp.log1p(jnp.exp(-jnp.abs(z)))
    logf_ref[0] = logf
    if npre + t < total:
        lf_s[...] = jnp.zeros(lf_s.shape, F32)
    if npre:
        lf_s[0:npre, :] = pre_ref[0]
    lf_s[npre:npre + t, :] = logf
    tri = (lax.broadcasted_iota(jnp.int32, (blk, blk), 1)
           <= lax.broadcasted_iota(jnp.int32, (blk, blk), 0)).astype(F32)
    carry = jnp.zeros((1, LANES), F32)
    for c in range(total // blk):
        rows = slice(c * blk, (c + 1) * blk)
        cum = jnp.dot(tri, lf_s[rows, :], preferred_element_type=F32, precision=lax.Precision.HIGHEST) + carry
        cum_ref[0, rows, :] = cum
        cumt_ref[0, c] = cum.T[:cumt_ref.shape[2], :]
        carry = cum[blk - 1:blk, :]


def fox_gate(zf, row_off, bsz, t, b_f, pre):
    nheads = b_f.shape[0]
    npre = 0 if pre is None else pre.shape[1]
    total = -(-(npre + t) // LANES) * LANES
    if pre is None:
        pre = jnp.zeros((bsz, 8, LANES), F32)
    pp = pre.shape[1]
    hrows = -(-nheads // 8) * 8
    zf3 = zf[row_off:row_off + bsz * t].reshape(bsz, t, LANES)
    bfp = jnp.zeros((1, LANES), F32).at[0, :nheads].set(b_f.astype(F32))
    return pl.pallas_call(
        functools.partial(_fox_gate_kernel, npre=npre, t=t, blk=LANES),
        grid=(bsz,),
        in_specs=[pl.BlockSpec((1, pp, LANES), lambda b: (b, 0, 0)),
                  pl.BlockSpec((1, t, LANES), lambda b: (b, 0, 0)),
                  pl.BlockSpec((1, LANES), lambda b: (0, 0))],
        out_specs=(pl.BlockSpec((1, t, LANES), lambda b: (b, 0, 0)),
                   pl.BlockSpec((1, total, LANES), lambda b: (b, 0, 0)),
                   pl.BlockSpec((1, total // LANES, hrows, LANES), lambda b: (b, 0, 0, 0))),
        out_shape=(jax.ShapeDtypeStruct((bsz, t, LANES), F32),
                   jax.ShapeDtypeStruct((bsz, total, LANES), F32),
                   jax.ShapeDtypeStruct((bsz, total // LANES, hrows, LANES), F32)),
        scratch_shapes=[pltpu.VMEM((total, LANES), F32)],
        compiler_params=_cparams(("parallel",), 12 * total * LANES * 4),
        name="fox_gate",
    )(pre, zf3, bfp)


def _fox_attn_kernel(q_ref, k_ref, v_ref, cq_ref, ck_ref, o_ref, m_s, l_s, acc_s,
                     *, tq, tk, nk_total, qoff, scale):
    h = pl.program_id(1)
    i = pl.program_id(2)
    q = q_ref[...]
    lane = lax.broadcasted_iota(jnp.int32, cq_ref.shape[1:], 1)
    fq = jnp.sum(jnp.where(lane == h, cq_ref[0], 0.0), axis=1, keepdims=True)
    m_s[...] = jnp.full(m_s.shape, NEG_INF, F32)
    l_s[...] = jnp.zeros(l_s.shape, F32)
    acc_s[...] = jnp.zeros(acc_s.shape, F32)
    qpos = qoff + i * tq + lax.broadcasted_iota(jnp.int32, (tq, tk), 0)

    def body(j, _):
        ks = pl.multiple_of(j * tk, tk)
        k = k_ref[0, pl.ds(ks, tk), :]
        v = v_ref[0, pl.ds(ks, tk), :]
        fk = jnp.concatenate([ck_ref[0, j * (tk // LANES) + c, pl.ds(h, 1), :] for c in range(tk // LANES)],
                             axis=1)
        s = lax.dot_general(q, k, (((1,), (1,)), ((), ())), preferred_element_type=F32) * scale
        s = s + fq - fk
        kpos = ks + lax.broadcasted_iota(jnp.int32, (tq, tk), 1)
        s = jnp.where(kpos <= qpos, s, NEG_INF)
        m_prev = m_s[...]
        m_new = jnp.maximum(m_prev, jnp.max(s, axis=1, keepdims=True))
        alpha = jnp.exp(m_prev - m_new)
        p = jnp.exp(s - m_new)
        l_s[...] = alpha * l_s[...] + jnp.sum(p, axis=1, keepdims=True)
        acc_s[...] = alpha * acc_s[...] + jnp.dot(p.astype(BF16), v, preferred_element_type=F32)
        m_s[...] = m_new
        return 0

    nk = jnp.minimum((qoff + (i + 1) * tq + tk - 1) // tk, nk_total)
    lax.fori_loop(0, nk, body, 0)
    o_ref[...] = (acc_s[...] / l_s[...]).astype(o_ref.dtype)


def fox_attention(q_all, row_off, bsz, t, k, v, cum, cumt, qoff, scale):
    tkeys = k.shape[1]
    hd = LANES
    nheads = q_all.shape[1] // hd
    tq = _pick(t, (256, 128, 64, 32, 16, 8))
    tk = _pick(tkeys, (256, 128))
    nq = t // tq
    rb0 = row_off // tq
    cq0 = qoff // tq
    vmem = 4 * tkeys * hd * 2 + 2 * cumt.shape[2] * tkeys * 4 + 8 * tq * tk * 4 + 8 * tq * hd * 4 + (2 << 20)
    return pl.pallas_call(
        functools.partial(_fox_attn_kernel, tq=tq, tk=tk, nk_total=tkeys // tk, qoff=qoff, scale=scale),
        grid=(bsz, nheads, nq),
        in_specs=[pl.BlockSpec((tq, hd), lambda b, h, i: (rb0 + b * nq + i, h)),
                  pl.BlockSpec((1, tkeys, hd), lambda b, h, i: (b, 0, h)),
                  pl.BlockSpec((1, tkeys, hd), lambda b, h, i: (b, 0, h)),
                  pl.BlockSpec((1, tq, LANES), lambda b, h, i: (b, cq0 + i, 0)),
                  pl.BlockSpec((1,) + cumt.shape[1:], lambda b, h, i: (b, 0, 0, 0))],
        out_specs=pl.BlockSpec((tq, hd), lambda b, h, i: (b * nq + i, h)),
        out_shape=jax.ShapeDtypeStruct((bsz * t, nheads * hd), BF16),
        scratch_shapes=[pltpu.VMEM((tq, 1), F32), pltpu.VMEM((tq, 1), F32), pltpu.VMEM((tq, hd), F32)],
        compiler_params=_cparams(("parallel", "parallel", "arbitrary"), vmem),
        name="fox_attention",
    )(q_all, k, v, cum, cumt)


def _rwkv_prep_kernel(z_ref, zp_ref, sh_ref, mu_ref, w0_ref, a0_ref, w2_ref, a2_ref, g2_ref,
                      r_ref, k_ref, v_ref, lw_ref, a_ref, g_ref, *, w):
    i = pl.program_id(1)
    z = z_ref[...]
    prev_row = jnp.where(i == 0, sh_ref[0], zp_ref[7:8, :])
    row = lax.broadcasted_iota(jnp.int32, z.shape, 0)
    z_prev = jnp.where(row == 0, prev_row, pltpu.roll(z, 1, 0))
    zm = z + (z_prev - z) * mu_ref[...]
    r_ref[...] = zm[:, 0:w]
    k_ref[...] = zm[:, w:2 * w]
    v_ref[...] = zm[:, 2 * w:3 * w]
    slab = zm[:, 3 * w:]
    lora_w = jnp.dot(jnp.tanh(slab).astype(BF16), w2_ref[...], preferred_element_type=F32)
    x = -(w0_ref[...] + lora_w)
    softplus = jnp.maximum(x, 0.0) + jnp.log1p(jnp.exp(-jnp.abs(x)))
    lw_ref[...] = -jnp.exp(-softplus - 0.5)
    a_ref[...] = jax.nn.sigmoid(a0_ref[...] + jnp.dot(slab.astype(BF16), a2_ref[...], preferred_element_type=F32))
    g_ref[...] = jnp.dot(jax.nn.sigmoid(slab).astype(BF16), g2_ref[...], preferred_element_type=F32)


def rwkv_prep(zr, row_off, bsz, t, shift_prev, prm):
    wz = zr.shape[1]
    w = prm["w0"].shape[1]
    tt = _pick(t, (128, 64, 32, 16, 8))
    nt = t // tt
    rb0 = row_off // tt
    full = lambda shape: pl.BlockSpec(shape, lambda b, i: (0,) * len(shape))
    ospec = pl.BlockSpec((tt, w), lambda b, i: (b * nt + i, 0))
    oshape = jax.ShapeDtypeStruct((bsz * t, w), F32)
    ls = wz - 3 * w
    vmem = 6 * tt * wz * 4 + 12 * tt * w * 4 + 6 * ls * w * 2 + 8 * tt * w * 4
    return pl.pallas_call(
        functools.partial(_rwkv_prep_kernel, w=w),
        grid=(bsz, nt),
        in_specs=[pl.BlockSpec((tt, wz), lambda b, i: (rb0 + b * nt + i, 0)),
                  pl.BlockSpec((8, wz), lambda b, i: (jnp.maximum((row_off + (b * nt + i) * tt) // 8 - 1, 0), 0)),
                  pl.BlockSpec((1, 1, wz), lambda b, i: (b, 0, 0)),
                  full((1, wz)), full((1, w)), full((1, w)), full((ls, w)), full((ls, w)), full((ls, w))],
        out_specs=(ospec,) * 6,
        out_shape=(oshape,) * 6,
        compiler_params=_cparams(("parallel", "arbitrary"), vmem),
        name="rwkv_prep",
    )(zr, zr, shift_prev.reshape(bsz, 1, wz), prm["mu"], prm["w0"], prm["a0"], prm["w2"], prm["a2"], prm["g2"])


def _bdot(a, b, dims):
    return lax.dot_general(a, b, dims, preferred_element_type=F32, precision=lax.Precision.HIGHEST)


_NT = (((2,), (2,)), ((0,), (0,)))
_NN = (((2,), (1,)), ((0,), (0,)))
_TN = (((1,), (1,)), ((0,), (0,)))


def _rwkv_scan_kernel(r_ref, k_ref, v_ref, lw_ref, a_ref, g_ref, s0_ref, kk_ref, ka_ref, rk_ref,
                      lnw_ref, lnb_ref, y_ref, sout_ref, st_s, *, chunk):
    c = pl.program_id(2)

    @pl.when(c == 0)
    def _():
        st_s[...] = s0_ref[0]

    r = r_ref[0]
    k = k_ref[0]
    v = v_ref[0]
    lw = lw_ref[0]
    a = a_ref[0]
    hb = r.shape[0]
    s0 = st_s[...]

    kk = k * kk_ref[...]
    kk = kk / jnp.maximum(jnp.sqrt(jnp.sum(kk * kk, axis=-1, keepdims=True)), 1e-12)
    kmod = k * (1.0 + (a - 1.0) * ka_ref[...])

    li = lax.broadcasted_iota(jnp.int32, (chunk, chunk), 0)
    mi = lax.broadcasted_iota(jnp.int32, (chunk, chunk), 1)
    incl = (mi <= li).astype(F32)
    strict = (mi < li).astype(F32)
    cs = _bdot(jnp.broadcast_to(incl, (hb, chunk, chunk)), lw, _NN)
    dec_in = jnp.exp(cs)
    dec_ex = jnp.exp(cs - lw)
    inv = jnp.exp(-cs)
    p_rows = jnp.concatenate([-kk * dec_ex, r * dec_in], axis=1)
    q_rows = jnp.concatenate([kk * a * inv, kmod * inv], axis=1)
    mm = _bdot(p_rows, q_rows, _NT)
    a_ab = mm[:, :chunk, :chunk] * strict
    a_ak = mm[:, :chunk, chunk:] * strict
    r_b = mm[:, chunk:, :chunk] * incl
    r_k = mm[:, chunk:, chunk:] * incl
    ps = _bdot(p_rows, s0, _NT)
    x = ps[:, :chunk] + _bdot(a_ak, v, _NN)
    pw = a_ab
    n_iter = chunk.bit_length() - 1
    for it in range(n_iter):
        x = x + _bdot(pw, x, _NN)
        if it + 1 < n_iter:
            pw = _bdot(pw, pw, _NN)
    y = ps[:, chunk:] + _bdot(r_b, x, _NN) + _bdot(r_k, v, _NN)
    uv = jnp.concatenate([x, v], axis=1)
    s_new = (s0 + _bdot(uv, q_rows, _TN)) * dec_in[:, chunk - 1:chunk, :]
    st_s[...] = s_new
    sout_ref[0] = s_new

    mean = jnp.mean(y, axis=-1, keepdims=True)
    var = jnp.mean(jnp.square(y - mean), axis=-1, keepdims=True)
    yn = (y - mean) * lax.rsqrt(var + RWKV_LN_EPS) * lnw_ref[...] + lnb_ref[...]
    bonus = jnp.sum(r * kmod * rk_ref[...], axis=-1, keepdims=True) * v
    y_ref[0] = (yn + bonus) * g_ref[0]


def rwkv_scan(r, k, v, lw, a, g, s0, prm):
    bsz, nh, t, hd = r.shape
    chunk = _pick(t, (64, 32, 16, 8))
    hb = _pick(nh, (8, 4, 2, 1))
    nc = t // chunk
    xspec = pl.BlockSpec((1, hb, chunk, hd), lambda b, h, c: (b, h, c, 0))
    pspec = pl.BlockSpec((hb, 1, hd), lambda b, h, c: (h, 0, 0))
    sspec = pl.BlockSpec((1, hb, hd, hd), lambda b, h, c: (b, h, 0, 0))
    vmem = 16 * hb * chunk * LANES * 4 + 40 * hb * 2 * chunk * LANES * 4 + 6 * hb * hd * LANES * 4
    y, s = pl.pallas_call(
        functools.partial(_rwkv_scan_kernel, chunk=chunk),
        grid=(bsz, nh // hb, nc),
        in_specs=[xspec] * 6 + [sspec] + [pspec] * 5,
        out_specs=(xspec, sspec),
        out_shape=(jax.ShapeDtypeStruct((bsz, nh, t, hd), F32), jax.ShapeDtypeStruct((bsz, nh, hd, hd), F32)),
        scratch_shapes=[pltpu.VMEM((hb, hd, hd), F32)],
        compiler_params=_cparams(("parallel", "parallel", "arbitrary"), vmem),
        name="rwkv_scan",
    )(r, k, v, lw, a, g, s0, prm["k_k"], prm["k_a"], prm["r_k"], prm["ln_w"], prm["ln_b"])
    return y, s


def _router_kernel(x_ref, w_ref, comb_ref, *, n_experts):
    logits = jnp.dot(x_ref[...], w_ref[...], preferred_element_type=F32)
    lane = lax.broadcasted_iota(jnp.int32, logits.shape, 1)
    big = jnp.int32(LANES)
    logits = jnp.where(lane < n_experts, logits, -jnp.inf)
    t1 = jnp.max(logits, axis=1, keepdims=True)
    i1 = jnp.min(jnp.where(logits == t1, lane, big), axis=1, keepdims=True)
    rest = jnp.where(lane == i1, -jnp.inf, logits)
    t2 = jnp.max(rest, axis=1, keepdims=True)
    i2 = jnp.min(jnp.where(rest == t2, lane, big), axis=1, keepdims=True)
    e2 = jnp.exp(t2 - t1)
    den = 1.0 + e2
    comb_ref[...] = jnp.where(lane == i1, 1.0 / den, jnp.where(lane == i2, e2 / den, 0.0))


def moe_router(x, w_router):
    m, d = x.shape
    ne = w_router.shape[1]
    wp = jnp.zeros((d, LANES), BF16).at[:, :ne].set(w_router.astype(BF16))
    tm = _pick(m, (1280, 1024, 640, 512, 320, 256, 128, 64, 32, 16, 8))
    return pl.pallas_call(
        functools.partial(_router_kernel, n_experts=ne),
        grid=(m // tm,),
        in_specs=[pl.BlockSpec((tm, d), lambda i: (i, 0)), pl.BlockSpec((d, LANES), lambda i: (0, 0))],
        out_specs=pl.BlockSpec((tm, LANES), lambda i: (i, 0)),
        out_shape=jax.ShapeDtypeStruct((m, LANES), F32),
        compiler_params=_cparams(("parallel",), 2 * tm * d * 2 + 2 * d * LANES * 2 + 8 * tm * LANES * 4),
        name="moe_router",
    )(x, wp)


def _rope_tables(pos, rope, width, lead):
    inv = ROPE_THETA ** (-jnp.arange(0, rope, 2, dtype=F32) / rope)
    ang = pos.astype(F32)[:, None] * inv[None, :]
    cos, sin = jnp.cos(ang), jnp.sin(ang)
    n = pos.shape[0]
    ctab = jnp.concatenate([jnp.ones((n, lead), F32), cos, cos, jnp.zeros((n, width - lead - rope), F32)], axis=1)
    stab = jnp.concatenate([jnp.zeros((n, lead), F32), -sin, sin, jnp.zeros((n, width - lead - rope), F32)], axis=1)
    return ctab, stab


def _swap_halves(w):
    half = w.shape[-1] // 2
    return jnp.concatenate([w[..., half:], w[..., :half]], axis=-1)


def kernel(x_prompt, x_sample, cache_mla_ckv, cache_mla_kpe, state_s5_re, state_s5_im, state_rwkv_wkv, state_rwkv_shift, cache_fox_k, cache_fox_v, cache_fox_logf, ln0_mix, w_in0, s5_a_re, s5_a_im, s5_log_dt, s5_b_re, s5_b_im, s5_c_re, s5_c_im, s5_d, s5_w_glu, s5_b_glu, mla_q_norm, mla_w_q_up, mla_kv_norm, mla_w_uk, mla_w_uv, w_out0, ln0_ffn, ffn_w_gate, ffn_w_up, ffn_w_down, ln1_mix, w_in1, rwkv_mu, rwkv_w0, rwkv_w2, rwkv_a0, rwkv_a2, rwkv_g2, rwkv_k_k, rwkv_k_a, rwkv_r_k, rwkv_ln_w, rwkv_ln_b, fox_b_f, w_out1, ln1_ffn, moe_w_router, moe_w_gate, moe_w_up, moe_w_down, final_norm):
    bp, tp, d = x_prompt.shape
    bs, ts, _ = x_sample.shape
    past = cache_mla_ckv.shape[1]
    n_p, n_s = bp * tp, bs * ts
    ntok = n_p + n_s
    streams = ((0, bp, tp), (n_p, bs, ts))

    h = jnp.concatenate([x_prompt.reshape(n_p, d), x_sample.reshape(n_s, d)], axis=0)

    s5_w = s5_d.shape[0]
    q_rank = mla_q_norm.shape[0]
    kv_rank = mla_kv_norm.shape[0]
    n_mh, qk = mla_w_q_up.shape[1], mla_w_q_up.shape[2]
    nope = mla_w_uk.shape[2]
    rope = qk - nope
    vdim = mla_w_uv.shape[2]
    mla_scale = float(qk) ** -0.5
    qw = 2 * LANES
    kvw = kv_rank + LANES

    (xn,) = rmsnorm(h, ln0_mix, (BF16,))
    w_in0b = w_in0.astype(BF16)
    (z_uq,) = matmul(xn, w_in0b[:, :s5_w + q_rank])
    off_kv = s5_w + q_rank
    w_kpe = w_in0b[:, off_kv + kv_rank:]
    zpad = jnp.zeros((d, kvw - kv_rank - rope), BF16)
    w_kv1 = jnp.concatenate([w_in0b[:, off_kv:off_kv + kv_rank], w_kpe, zpad], axis=1)
    w_kv2 = jnp.concatenate([jnp.zeros((d, kv_rank), BF16), _swap_halves(w_kpe), zpad], axis=1)
    pos_p = jnp.arange(tp)
    pos_s = past + jnp.arange(ts)

    def token_tables(width, lead):
        cp, sp = _rope_tables(pos_p, rope, width, lead)
        cs, ss = _rope_tables(pos_s, rope, width, lead)
        return (jnp.concatenate([jnp.tile(cp, (bp, 1)), jnp.tile(cs, (bs, 1))], axis=0),
                jnp.concatenate([jnp.tile(sp, (bp, 1)), jnp.tile(ss, (bs, 1))], axis=0))

    ckv_c, ckv_s = token_tables(kvw, kv_rank)
    z_kv = rope_matmul(xn, w_kv1, w_kv2, ckv_c, ckv_s, F32, tn=kvw)
    ckv_f, ckv_b = rmsnorm(z_kv, mla_kv_norm, (F32, BF16), col_block=0, width=kv_rank)
    kpe_f = z_kv[:, kv_rank:kv_rank + rope]
    kpe_b = z_kv[:, kv_rank:].astype(BF16)

    (cqn,) = rmsnorm(z_uq, mla_q_norm, (BF16,), col_block=s5_w // q_rank, width=q_rank)
    wq = mla_w_q_up.astype(BF16)
    zq = jnp.zeros((q_rank, n_mh, qw - qk), BF16)
    wq1 = jnp.concatenate([wq, zq], axis=2).reshape(q_rank, n_mh * qw)
    wq2 = jnp.concatenate([jnp.zeros((q_rank, n_mh, nope), BF16), _swap_halves(wq[:, :, nope:]), zq],
                          axis=2).reshape(q_rank, n_mh * qw)
    q_c, q_s = token_tables(qw, nope)
    q_all = rope_matmul(cqn, wq1, wq2, q_c, q_s, BF16, tn=qw)

    s5p = s5_params(s5_a_re, s5_a_im, s5_log_dt, s5_b_re, s5_b_im, s5_c_re, s5_c_im, s5_d, s5_w_glu, s5_b_glu)
    g5, p5 = s5_a_re.shape
    wuk = jnp.transpose(mla_w_uk, (1, 2, 0)).astype(BF16)
    wuv = jnp.transpose(mla_w_uv, (1, 0, 2)).astype(BF16)

    y_s5, s5_re, s5_im, y_mla = [], [], [], []
    for si, (off, bsz, t) in enumerate(streams):
        if si == 0:
            h0r = jnp.zeros((bsz, g5 * p5), F32)
            h0i = h0r
            ckv_k = ckv_b[off:off + bsz * t].reshape(bsz, t, kv_rank)
            kpe_k = kpe_b[off:off + bsz * t].reshape(bsz, t, LANES)
            klen, causal = t, True
        else:
            h0r, h0i = state_s5_re.astype(F32), state_s5_im.astype(F32)
            klen, causal = past + t, False
            padk = -(-klen // LANES) * LANES - klen
            ckv_k = jnp.concatenate([cache_mla_ckv.astype(BF16), ckv_b[off:off + bsz * t].reshape(bsz, t, kv_rank),
                                     jnp.zeros((bsz, padk, kv_rank), BF16)], axis=1)
            kpe_cache = jnp.concatenate([cache_mla_kpe.astype(BF16),
                                         jnp.zeros((bsz, past, LANES - rope), BF16)], axis=2)
            kpe_k = jnp.concatenate([kpe_cache, kpe_b[off:off + bsz * t].reshape(bsz, t, LANES),
                                     jnp.zeros((bsz, padk, LANES), BF16)], axis=1)
        ys, hr, hi = s5_mixer(z_uq, off, bsz, t, h0r, h0i, s5p)
        y_s5.append(ys)
        s5_re.append(hr.reshape(bsz, g5, p5))
        s5_im.append(hi.reshape(bsz, g5, p5))
        y_mla.append(mla_attention(q_all, off, bsz, t, ckv_k, kpe_k, wuk, wuv, klen, causal, mla_scale))

    mix0 = jnp.concatenate([jnp.concatenate(y_s5, axis=0), jnp.concatenate(y_mla, axis=0)], axis=1)
    (h,) = matmul(mix0, w_out0.astype(BF16), res=h)
    (hn,) = rmsnorm(h, ln0_ffn, (BF16,))
    dff = ffn_w_gate.shape[1]
    dff_pad = -(-dff // 512) * 512
    padc = lambda wgt: jnp.pad(wgt.astype(BF16), ((0, 0), (0, dff_pad - dff)))
    act = swiglu_up(hn, padc(ffn_w_gate), padc(ffn_w_up))
    (h,) = matmul(act, jnp.pad(ffn_w_down.astype(BF16), ((0, dff_pad - dff), (0, 0))), res=h, tk=_pick(dff_pad, (1024, 512)))

    rw = rwkv_w0.shape[0]
    nh_r, hd_r = rwkv_k_k.shape
    shift_w = rwkv_mu.shape[0]
    lora_w = shift_w - 3 * rw
    slab = -(-lora_w // LANES) * LANES
    wz = 3 * rw + slab
    nh_f = fox_b_f.shape[0]
    fw = (w_in1.shape[1] - shift_w - nh_f) // 3
    fox_scale = float(fw // nh_f) ** -0.5
    d_lw, d_la = rwkv_w2.shape[0], rwkv_a2.shape[0]

    (xn,) = rmsnorm(h, ln1_mix, (BF16,))
    w_in1b = w_in1.astype(BF16)
    (zr,) = matmul(xn, jnp.pad(w_in1b[:, :shift_w], ((0, 0), (0, wz - shift_w))))
    (fq,) = matmul(xn, w_in1b[:, shift_w:shift_w + fw], (BF16,))
    fk_f, fk_b = matmul(xn, w_in1b[:, shift_w + fw:shift_w + 2 * fw], (F32, BF16))
    fv_f, fv_b = matmul(xn, w_in1b[:, shift_w + 2 * fw:shift_w + 3 * fw], (F32, BF16))
    (zf,) = matmul(xn, jnp.pad(w_in1b[:, shift_w + 3 * fw:], ((0, 0), (0, LANES - nh_f))))

    padrow = lambda wgt, lo: jnp.zeros((slab, rw), BF16).at[lo:lo + wgt.shape[0]].set(wgt.astype(BF16))
    row2 = lambda x_, n_: x_.astype(F32).reshape(1, n_)
    head3 = lambda x_: x_.astype(F32).reshape(nh_r, 1, hd_r)
    rprm = dict(mu=jnp.pad(row2(rwkv_mu, shift_w), ((0, 0), (0, wz - shift_w))), w0=row2(rwkv_w0, rw),
                a0=row2(rwkv_a0, rw), w2=padrow(rwkv_w2, 0), a2=padrow(rwkv_a2, d_lw),
                g2=padrow(rwkv_g2, d_lw + d_la), k_k=head3(rwkv_k_k), k_a=head3(rwkv_k_a), r_k=head3(rwkv_r_k),
                ln_w=head3(rwkv_ln_w), ln_b=head3(rwkv_ln_b))

    y_r, wkv, shift_new, y_f, logf_out = [], [], [], [], []
    for si, (off, bsz, t) in enumerate(streams):
        if si == 0:
            shift_prev = jnp.zeros((bsz, wz), F32)
            s0 = jnp.zeros((bsz, nh_r, hd_r, hd_r), F32)
            pre = None
            k_all = fk_b[off:off + bsz * t].reshape(bsz, t, fw)
            v_all = fv_b[off:off + bsz * t].reshape(bsz, t, fw)
            qoff = 0
        else:
            shift_prev = jnp.pad(state_rwkv_shift.astype(F32), ((0, 0), (0, wz - shift_w)))
            s0 = state_rwkv_wkv.astype(F32)
            pre = jnp.pad(cache_fox_logf.astype(F32), ((0, 0), (0, 0), (0, LANES - nh_f)))
            qoff = cache_fox_k.shape[1]
            padk = -(-(qoff + t) // LANES) * LANES - (qoff + t)
            cat = lambda cache, new: jnp.concatenate(
                [cache.reshape(bsz, qoff, fw).astype(BF16), new[off:off + bsz * t].reshape(bsz, t, fw),
                 jnp.zeros((bsz, padk, fw), BF16)], axis=1)
            k_all, v_all = cat(cache_fox_k, fk_b), cat(cache_fox_v, fv_b)
        parts = rwkv_prep(zr, off, bsz, t, shift_prev, rprm)
        to_heads = lambda x_: jnp.transpose(x_.reshape(bsz, t, nh_r, hd_r), (0, 2, 1, 3))
        yh, s_fin = rwkv_scan(*[to_heads(x_) for x_ in parts], s0, rprm)
        y_r.append(jnp.transpose(yh, (0, 2, 1, 3)).reshape(bsz * t, rw).astype(BF16))
        wkv.append(s_fin)
        shift_new.append(zr[off:off + bsz * t].reshape(bsz, t, wz)[:, -1, :shift_w])
        logf, cum, cumt = fox_gate(zf, off, bsz, t, fox_b_f, pre)
        logf_out.append(logf[:, :, :nh_f])
        y_f.append(fox_attention(fq, off, bsz, t, k_all, v_all, cum, cumt, qoff, fox_scale))

    mix1 = jnp.concatenate([jnp.concatenate(y_r, axis=0), jnp.concatenate(y_f, axis=0)], axis=1)
    (h,) = matmul(mix1, w_out1.astype(BF16), res=h)
    (hn,) = rmsnorm(h, ln1_ffn, (BF16,))
    comb = moe_router(hn, moe_w_router)
    n_exp = moe_w_gate.shape[0]
    wg_b, wu_b, wd_b = moe_w_gate.astype(BF16), moe_w_up.astype(BF16), moe_w_down.astype(BF16)
    for e in range(n_exp):
        act = swiglu_up(hn, wg_b, wu_b, w_index=e)
        (h,) = matmul(act, wd_b, res=h, scale=comb[:, e:e + 1], w_index=e)

    (y_all,) = rmsnorm(h, final_norm, (F32,))
    nfh = fw // nh_f
    outs = [y_all[:n_p].reshape(bp, tp, d), y_all[n_p:].reshape(bs, ts, d)]
    for si, (off, bsz, t) in enumerate(streams):
        rows = slice(off, off + bsz * t)
        outs += [ckv_f[rows].reshape(bsz, t, kv_rank), kpe_f[rows].reshape(bsz, t, rope), s5_re[si], s5_im[si],
                 wkv[si], shift_new[si], fk_f[rows].reshape(bsz, t, nh_f, nfh), fv_f[rows].reshape(bsz, t, nh_f, nfh),
                 logf_out[si]]
    return tuple(outs)
```

```python
import functools
import math

import jax
import jax.numpy as jnp
from jax import lax
from jax.experimental import pallas as pl
from jax.experimental.pallas import tpu as pltpu

F32 = jnp.float32
BF16 = jnp.bfloat16

V7X_VMEM_BYTES = 64 * 1024 * 1024
VMEM_CAP = V7X_VMEM_BYTES - 8 * 1024 * 1024
LANES = 128

NORM_EPS = 1e-6
NEG_INF = -1e30
CHUNK = 64
ROPE_THETA = 10000.0
RWKV_LN_EPS = 64e-5
S5_BLOCK_GROUPS = 8


def _pick(n, cands):
    for c in cands:
        if c <= n and n % c == 0:
            return c
    return n


def _cparams(sem, vmem_bytes):
    limit = int(min(max(vmem_bytes * 1.25 + (4 << 20), 24 << 20), VMEM_CAP))
    return pltpu.CompilerParams(dimension_semantics=sem, vmem_limit_bytes=limit)


def _rmsnorm_kernel(x_ref, g_ref, *o_refs):
    x = x_ref[...].astype(F32)
    y = x * lax.rsqrt(jnp.mean(x * x, axis=-1, keepdims=True) + NORM_EPS)
    y = y * g_ref[...]
    for o in o_refs:
        o[...] = y.astype(o.dtype)


def rmsnorm(x, g, out_dtypes, col_block=0, width=None):
    m = x.shape[0]
    width = x.shape[1] if width is None else width
    tm = _pick(m, (512, 320, 256, 128, 64, 32, 16, 8))
    outs = tuple(jax.ShapeDtypeStruct((m, width), d) for d in out_dtypes)
    res = pl.pallas_call(
        _rmsnorm_kernel,
        grid=(m // tm,),
        in_specs=[pl.BlockSpec((tm, width), lambda i: (i, col_block)),
                  pl.BlockSpec((1, width), lambda i: (0, 0))],
        out_specs=tuple(pl.BlockSpec((tm, width), lambda i: (i, 0)) for _ in out_dtypes),
        out_shape=outs,
        compiler_params=_cparams(("parallel",), tm * width * 4 * 2 * (1 + len(out_dtypes))),
        name="rmsnorm",
    )(x, g.reshape(1, width).astype(F32))
    return res


def _mm_kernel(*refs, nk, has_res, has_scale, n_out):
    x_ref, w_ref = refs[0], refs[1]
    pos = 2
    res_ref = scale_ref = None
    if has_res:
        res_ref = refs[pos]
        pos += 1
    if has_scale:
        scale_ref = refs[pos]
        pos += 1
    o_refs = refs[pos:pos + n_out]
    acc_ref = refs[pos + n_out] if nk > 1 else None

    part = jnp.dot(x_ref[...].astype(BF16), w_ref[...].astype(BF16), preferred_element_type=F32)

    def finish(acc):
        if has_scale:
            acc = acc * scale_ref[...]
        if has_res:
            acc = res_ref[...] + acc
        for o in o_refs:
            o[...] = acc.astype(o.dtype)

    if nk == 1:
        finish(part)
    else:
        k = pl.program_id(2)

        @pl.when(k == 0)
        def _():
            acc_ref[...] = part

        @pl.when(k > 0)
        def _():
            acc_ref[...] += part

        @pl.when(k == nk - 1)
        def _():
            finish(acc_ref[...])


def matmul(x, w, out_dtypes=(F32,), res=None, scale=None, w_index=None, tm=None, tn=None, tk=None):
    m, kdim = x.shape
    n = w.shape[-1]
    tm = tm or _pick(m, (1280, 1024, 640, 512, 320, 256, 128, 64, 32, 16, 8))
    tn = tn or _pick(n, (512, 384, 256, 128))
    tk = tk or (kdim if kdim <= 4096 else _pick(kdim, (2048, 1792, 1024, 512, 256, 128)))
    nk = kdim // tk
    grid = (m // tm, n // tn, nk)
    in_specs = [pl.BlockSpec((tm, tk), lambda i, j, k: (i, k))]
    if w.ndim == 3:
        in_specs.append(pl.BlockSpec((None, tk, tn), lambda i, j, k: (w_index, k, j)))
    else:
        in_specs.append(pl.BlockSpec((tk, tn), lambda i, j, k: (k, j)))
    args = [x, w]
    if res is not None:
        in_specs.append(pl.BlockSpec((tm, tn), lambda i, j, k: (i, j)))
        args.append(res)
    if scale is not None:
        in_specs.append(pl.BlockSpec((tm, 1), lambda i, j, k: (i, 0)))
        args.append(scale)
    out_specs = tuple(pl.BlockSpec((tm, tn), lambda i, j, k: (i, j)) for _ in out_dtypes)
    out_shape = tuple(jax.ShapeDtypeStruct((m, n), d) for d in out_dtypes)
    scratch = [pltpu.VMEM((tm, tn), F32)] if nk > 1 else []
    vmem = (2 * tm * tk * x.dtype.itemsize + 2 * tk * tn * w.dtype.itemsize
            + tm * tn * 4 * (2 * len(out_dtypes) + 1 + (2 if res is not None else 0)))
    outs = pl.pallas_call(
        functools.partial(_mm_kernel, nk=nk, has_res=res is not None, has_scale=scale is not None,
                          n_out=len(out_dtypes)),
        grid=grid, in_specs=in_specs, out_specs=out_specs, out_shape=out_shape,
        scratch_shapes=scratch,
        compiler_params=_cparams(("parallel", "parallel", "arbitrary"), vmem),
        name="matmul",
    )(*args)
    return outs


def _rope_mm_kernel(x_ref, w1_ref, w2_ref, c_ref, s_ref, o_ref):
    x = x_ref[...]
    a = jnp.dot(x, w1_ref[...], preferred_element_type=F32)
    b = jnp.dot(x, w2_ref[...], preferred_element_type=F32)
    o_ref[...] = (a * c_ref[...] + b * s_ref[...]).astype(o_ref.dtype)


def rope_matmul(x, w1, w2, ctab, stab, out_dtype, tn):
    m, kdim = x.shape
    n = w1.shape[1]
    est = lambda rows: 2 * rows * kdim * 2 + 4 * kdim * tn * 2 + 8 * rows * tn * 4
    tm = next((c for c in (1280, 1024, 640, 512, 320, 256, 128, 64, 32, 16, 8)
               if m % c == 0 and est(c) <= VMEM_CAP // 2), 8)
    vmem = est(tm)
    return pl.pallas_call(
        _rope_mm_kernel,
        grid=(m // tm, n // tn),
        in_specs=[pl.BlockSpec((tm, kdim), lambda i, j: (i, 0)),
                  pl.BlockSpec((kdim, tn), lambda i, j: (0, j)),
                  pl.BlockSpec((kdim, tn), lambda i, j: (0, j)),
                  pl.BlockSpec((tm, tn), lambda i, j: (i, 0)),
                  pl.BlockSpec((tm, tn), lambda i, j: (i, 0))],
        out_specs=pl.BlockSpec((tm, tn), lambda i, j: (i, j)),
        out_shape=jax.ShapeDtypeStruct((m, n), out_dtype),
        compiler_params=_cparams(("parallel", "parallel"), vmem),
        name="rope_matmul",
    )(x, w1, w2, ctab, stab)


def _swiglu_up_kernel(x_ref, wg_ref, wu_ref, o_ref):
    x = x_ref[...]
    g = jnp.dot(x, wg_ref[...].astype(BF16), preferred_element_type=F32)
    u = jnp.dot(x, wu_ref[...].astype(BF16), preferred_element_type=F32)
    o_ref[...] = (g * jax.nn.sigmoid(g) * u).astype(o_ref.dtype)


def swiglu_up(x, wg, wu, w_index=None):
    m, kdim = x.shape
    n = wg.shape[-1]
    tm = _pick(m, (1280, 1024, 640, 512, 320, 256, 128, 64, 32, 16, 8))
    tn = _pick(n, (256, 128))
    if wg.ndim == 3:
        wspec = pl.BlockSpec((None, kdim, tn), lambda i, j: (w_index, 0, j))
    else:
        wspec = pl.BlockSpec((kdim, tn), lambda i, j: (0, j))
    vmem = 2 * tm * kdim * 2 + 4 * kdim * tn * wg.dtype.itemsize + 6 * tm * tn * 4
    return pl.pallas_call(
        _swiglu_up_kernel,
        grid=(m // tm, n // tn),
        in_specs=[pl.BlockSpec((tm, kdim), lambda i, j: (i, 0)), wspec, wspec],
        out_specs=pl.BlockSpec((tm, tn), lambda i, j: (i, j)),
        out_shape=jax.ShapeDtypeStruct((m, n), BF16),
        compiler_params=_cparams(("parallel", "parallel"), vmem),
        name="swiglu_up",
    )(x, wg, wu)


def _s5_kernel(u_ref, h0r_ref, h0i_ref, lr_ref, li_ref, bbr_ref, bbi_ref, ccr_ref, cci_ref,
               d_ref, wglu_ref, bglu_ref, y_ref, hr_ref, hi_ref, xr_s, xi_s, st_r, st_i, *, tc, nblk):
    c = pl.program_id(1)

    @pl.when(c == 0)
    def _():
        st_r[...] = h0r_ref[0]
        st_i[...] = h0i_ref[0]

    u = u_ref[...]
    ub = u.astype(BF16)
    sw = xr_s.shape[1] // nblk
    for k in range(nblk):
        uk = ub[:, k * LANES:(k + 1) * LANES]
        xr_s[:, k * sw:(k + 1) * sw] = jnp.dot(uk, bbr_ref[k], preferred_element_type=F32)
        xi_s[:, k * sw:(k + 1) * sw] = jnp.dot(uk, bbi_ref[k], preferred_element_type=F32)

    scan_w = 1024
    for q in range(xr_s.shape[1] // scan_w):
        cols = slice(q * scan_w, (q + 1) * scan_w)
        lr = lr_ref[:, cols]
        li = li_ref[:, cols]

        def body(t, carry, cols=cols, lr=lr, li=li):
            hr, hi = carry
            nr = lr * hr - li * hi + xr_s[pl.ds(t, 1), cols]
            ni = lr * hi + li * hr + xi_s[pl.ds(t, 1), cols]
            xr_s[pl.ds(t, 1), cols] = nr
            xi_s[pl.ds(t, 1), cols] = ni
            return nr, ni

        hr, hi = lax.fori_loop(0, tc, body, (st_r[:, cols], st_i[:, cols]))
        st_r[:, cols] = hr
        st_i[:, cols] = hi

    hr_ref[0] = st_r[...]
    hi_ref[0] = st_i[...]

    ys = []
    for k in range(nblk):
        xr = xr_s[:, k * sw:(k + 1) * sw].astype(BF16)
        xi = xi_s[:, k * sw:(k + 1) * sw].astype(BF16)
        ys.append(jnp.dot(xr, ccr_ref[k], preferred_element_type=F32)
                  - jnp.dot(xi, cci_ref[k], preferred_element_type=F32))
    y = jnp.concatenate(ys, axis=1) + d_ref[...] * u
    y = jax.nn.gelu(y)
    gate = jax.nn.sigmoid(jnp.dot(y.astype(BF16), wglu_ref[...], preferred_element_type=F32) + bglu_ref[...])
    y_ref[...] = (y * gate).astype(y_ref.dtype)


def s5_mixer(z, row_off, bsz, t, h0_re, h0_im, prm):
    width = prm["d"].shape[1]
    nblk = width // LANES
    nstate = prm["lr"].shape[1]
    tc = _pick(t, (256, 128, 64, 32, 16, 8))
    nt = t // tc
    rb0 = row_off // tc
    full = lambda shape: pl.BlockSpec(shape, lambda b, c: (0,) * len(shape))
    vmem = (4 * tc * width * 4 + 2 * tc * nstate * 4 + 4 * nblk * LANES * (nstate // nblk) * 2 * 2
            + 2 * width * width * 2 + 8 * tc * width * 4)
    y, hr, hi = pl.pallas_call(
        functools.partial(_s5_kernel, tc=tc, nblk=nblk),
        grid=(bsz, nt),
        in_specs=[pl.BlockSpec((tc, width), lambda b, c: (rb0 + b * nt + c, 0)),
                  pl.BlockSpec((1, 1, nstate), lambda b, c: (b, 0, 0)),
                  pl.BlockSpec((1, 1, nstate), lambda b, c: (b, 0, 0)),
                  full((1, nstate)), full((1, nstate)),
                  full(prm["bbr"].shape), full(prm["bbi"].shape),
                  full(prm["ccr"].shape), full(prm["cci"].shape),
                  full((1, width)), full((width, width)), full((1, width))],
        out_specs=(pl.BlockSpec((tc, width), lambda b, c: (b * nt + c, 0)),
                   pl.BlockSpec((1, 1, nstate), lambda b, c: (b, 0, 0)),
                   pl.BlockSpec((1, 1, nstate), lambda b, c: (b, 0, 0))),
        out_shape=(jax.ShapeDtypeStruct((bsz * t, width), BF16),
                   jax.ShapeDtypeStruct((bsz, 1, nstate), F32),
                   jax.ShapeDtypeStruct((bsz, 1, nstate), F32)),
        scratch_shapes=[pltpu.VMEM((tc, nstate), F32), pltpu.VMEM((tc, nstate), F32),
                        pltpu.VMEM((1, nstate), F32), pltpu.VMEM((1, nstate), F32)],
        compiler_params=_cparams(("parallel", "arbitrary"), vmem),
        name="s5_mixer",
    )(z, h0_re.reshape(bsz, 1, nstate), h0_im.reshape(bsz, 1, nstate), prm["lr"], prm["li"],
      prm["bbr"], prm["bbi"], prm["ccr"], prm["cci"], prm["d"], prm["wglu"], prm["bglu"])
    return y, hr, hi


def s5_params(a_re, a_im, log_dt, b_re, b_im, c_re, c_im, d_skip, w_glu, b_glu):
    g, p = a_re.shape
    nch = b_re.shape[2]
    dt = jnp.exp(log_dt.astype(F32))[:, None]
    ar, ai = a_re.astype(F32), a_im.astype(F32)
    mag = jnp.exp(ar * dt)
    lr = mag * jnp.cos(ai * dt)
    li = mag * jnp.sin(ai * dt)
    den = ar * ar + ai * ai
    fr = ((lr - 1.0) * ar + li * ai) / den
    fi = (li * ar - (lr - 1.0) * ai) / den
    br, bi = b_re.astype(F32), b_im.astype(F32)
    bbr = fr[..., None] * br - fi[..., None] * bi
    bbi = fr[..., None] * bi + fi[..., None] * br
    gb = S5_BLOCK_GROUPS
    nblk = g // gb
    eye = jnp.eye(gb, dtype=F32)

    def blk_in(m):
        m = m.reshape(nblk, gb, p, nch)
        return jnp.einsum("kgpn,gh->kgnhp", m, eye).reshape(nblk, gb * nch, gb * p).astype(BF16)

    def blk_out(m):
        m = m.astype(F32).reshape(nblk, gb, nch, p)
        return jnp.einsum("kgnp,gh->kgphn", m, eye).reshape(nblk, gb * p, gb * nch).astype(BF16)

    width = g * nch
    return dict(lr=lr.reshape(1, g * p), li=li.reshape(1, g * p), bbr=blk_in(bbr), bbi=blk_in(bbi),
                ccr=blk_out(c_re), cci=blk_out(c_im), d=d_skip.astype(F32).reshape(1, width),
                wglu=w_glu.astype(BF16), bglu=b_glu.astype(F32).reshape(1, width))


MLA_ROWS = 1024
MLA_CHAIN_ROWS = 256


def _mla_kernel(q_ref, ckv_ref, kpe_ref, wuk_ref, wuv_ref, o_ref, qa_s, qpe_s, m_s, l_s, acc_s,
                *, tq, tk, nk_total, klen, causal, scale, hps):
    i = pl.program_id(1)
    nope = wuk_ref.shape[1]
    qw = q_ref.shape[1] // hps
    vdim = wuv_ref.shape[2]
    rows_all = hps * tq
    rc = MLA_CHAIN_ROWS if rows_all % MLA_CHAIN_ROWS == 0 else rows_all
    for hh in range(hps):
        rows = slice(hh * tq, (hh + 1) * tq)
        qa = jnp.dot(q_ref[:, hh * qw:hh * qw + nope], wuk_ref[hh], preferred_element_type=F32)
        qa_s[rows, :] = qa.astype(qa_s.dtype)
        qpe_s[rows, :] = q_ref[:, hh * qw + nope:(hh + 1) * qw]
    m_s[...] = jnp.full(m_s.shape, NEG_INF, F32)
    l_s[...] = jnp.zeros(l_s.shape, F32)
    acc_s[...] = jnp.zeros(acc_s.shape, F32)

    def tile(j, masked):
        ks = pl.multiple_of(j * tk, tk)
        ckv = ckv_ref[0, pl.ds(ks, tk), :]
        kpe = kpe_ref[0, pl.ds(ks, tk), :]
        scores = []
        for c0 in range(0, rows_all, rc):
            rows = slice(c0, c0 + rc)
            s = lax.dot_general(qa_s[rows, :].astype(BF16), ckv, (((1,), (1,)), ((), ())), preferred_element_type=F32)
            s = s + lax.dot_general(qpe_s[rows, :], kpe, (((1,), (1,)), ((), ())), preferred_element_type=F32)
            scores.append(s * scale)
        for c0, s in zip(range(0, rows_all, rc), scores):
            rows = slice(c0, c0 + rc)
            if masked:
                kpos = ks + lax.broadcasted_iota(jnp.int32, (rc, tk), 1)
                if causal:
                    qpos = i * tq + (c0 + lax.broadcasted_iota(jnp.int32, (rc, tk), 0)) % tq
                    s = jnp.where(kpos // CHUNK <= qpos // CHUNK, s, NEG_INF)
                if klen < nk_total * tk:
                    s = jnp.where(kpos < klen, s, NEG_INF)
            m_prev = m_s[rows, :]
            m_new = jnp.maximum(m_prev, jnp.max(s, axis=1, keepdims=True))
            alpha = jnp.exp(m_prev - m_new)
            p = jnp.exp(s - m_new)
            l_s[rows, :] = alpha * l_s[rows, :] + jnp.sum(p, axis=1, keepdims=True)
            acc_s[rows, :] = alpha * acc_s[rows, :] + jnp.dot(p.astype(BF16), ckv, preferred_element_type=F32)
            m_s[rows, :] = m_new

    def full_tile(j, c):
        tile(j, False)
        return c

    n_full = i if causal else nk_total - 1
    lax.fori_loop(0, n_full, full_tile, 0)
    tile(n_full, True)
    o_lat = (acc_s[...] / l_s[...]).astype(BF16)
    for hh in range(hps):
        o_ref[:, hh * vdim:(hh + 1) * vdim] = jnp.dot(
            o_lat[hh * tq:(hh + 1) * tq], wuv_ref[hh], preferred_element_type=F32).astype(o_ref.dtype)


def mla_attention(q_all, row_off, bsz, t, ckv, kpe, wuk, wuv, klen, causal, scale):
    nheads, nope, lat = wuk.shape
    vdim = wuv.shape[2]
    qw = q_all.shape[1] // nheads
    tkeys = ckv.shape[1]
    tq = _pick(t, (256, 128, 64, 32, 16, 8))
    tk = tq if causal else _pick(tkeys, (256, 128))
    if causal:
        assert tq % CHUNK == 0 and tkeys == t
    hps = _pick(nheads, tuple(c for c in (24, 16, 12, 8, 6, 4, 3, 2, 1) if c * tq <= MLA_ROWS))
    nq = t // tq
    rb0 = row_off // tq
    rows_all = hps * tq
    vmem = (2 * tkeys * (lat + LANES) * 2 + 4 * tq * hps * qw * 2 + 3 * rows_all * lat * 4 + 8 * rows_all * tk * 4
            + 4 * hps * (nope + vdim) * lat * 2 + (2 << 20))
    return pl.pallas_call(
        functools.partial(_mla_kernel, tq=tq, tk=tk, nk_total=tkeys // tk, klen=klen, causal=causal, scale=scale,
                          hps=hps),
        grid=(bsz, nq, nheads // hps),
        in_specs=[pl.BlockSpec((tq, hps * qw), lambda b, i, h: (rb0 + b * nq + i, h)),
                  pl.BlockSpec((1, tkeys, lat), lambda b, i, h: (b, 0, 0)),
                  pl.BlockSpec((1, tkeys, LANES), lambda b, i, h: (b, 0, 0)),
                  pl.BlockSpec((hps, nope, lat), lambda b, i, h: (h, 0, 0)),
                  pl.BlockSpec((hps, lat, vdim), lambda b, i, h: (h, 0, 0))],
        out_specs=pl.BlockSpec((tq, hps * vdim), lambda b, i, h: (b * nq + i, h)),
        out_shape=jax.ShapeDtypeStruct((bsz * t, nheads * vdim), BF16),
        scratch_shapes=[pltpu.VMEM((rows_all, lat), F32), pltpu.VMEM((rows_all, LANES), BF16),
                        pltpu.VMEM((rows_all, 1), F32), pltpu.VMEM((rows_all, 1), F32),
                        pltpu.VMEM((rows_all, lat), F32)],
        compiler_params=_cparams(("parallel", "parallel", "arbitrary"), vmem),
        name="mla_attention",
    )(q_all, ckv, kpe, wuk, wuv)


def _fox_gate_kernel(pre_ref, zf_ref, bf_ref, logf_ref, cum_ref, cumt_ref, lf_s, *, npre, t, blk):
    total = lf_s.shape[0]
    z = zf_ref[0] + bf_ref[...]
    logf = jnp.minimum(z, 0.0) - jnp.log1p(jnp.exp(-jnp.abs(z)))
    logf_ref[0] = logf
    if npre + t < total:
        lf_s[...] = jnp.zeros(lf_s.shape, F32)
    if npre:
        lf_s[0:npre, :] = pre_ref[0]
    lf_s[npre:npre + t, :] = logf
    tri = (lax.broadcasted_iota(jnp.int32, (blk, blk), 1)
           <= lax.broadcasted_iota(jnp.int32, (blk, blk), 0)).astype(F32)
    carry = jnp.zeros((1, LANES), F32)
    for c in range(total // blk):
        rows = slice(c * blk, (c + 1) * blk)
        cum = jnp.dot(tri, lf_s[rows, :], preferred_element_type=F32, precision=lax.Precision.HIGHEST) + carry
        cum_ref[0, rows, :] = cum
        cumt_ref[0, c] = cum.T[:cumt_ref.shape[2], :]
        carry = cum[blk - 1:blk, :]


def fox_gate(zf, row_off, bsz, t, b_f, pre):
    nheads = b_f.shape[0]
    npre = 0 if pre is None else pre.shape[1]
    total = -(-(npre + t) // LANES) * LANES
    if pre is None:
        pre = jnp.zeros((bsz, 8, LANES), F32)
    pp = pre.shape[1]
    hrows = -(-nheads // 8) * 8
    zf3 = zf[row_off:row_off + bsz * t].reshape(bsz, t, LANES)
    bfp = jnp.zeros((1, LANES), F32).at[0, :nheads].set(b_f.astype(F32))
    return pl.pallas_call(
        functools.partial(_fox_gate_kernel, npre=npre, t=t, blk=LANES),
        grid=(bsz,),
        in_specs=[pl.BlockSpec((1, pp, LANES), lambda b: (b, 0, 0)),
                  pl.BlockSpec((1, t, LANES), lambda b: (b, 0, 0)),
                  pl.BlockSpec((1, LANES), lambda b: (0, 0))],
        out_specs=(pl.BlockSpec((1, t, LANES), lambda b: (b, 0, 0)),
                   pl.BlockSpec((1, total, LANES), lambda b: (b, 0, 0)),
                   pl.BlockSpec((1, total // LANES, hrows, LANES), lambda b: (b, 0, 0, 0))),
        out_shape=(jax.ShapeDtypeStruct((bsz, t, LANES), F32),
                   jax.ShapeDtypeStruct((bsz, total, LANES), F32),
                   jax.ShapeDtypeStruct((bsz, total // LANES, hrows, LANES), F32)),
        scratch_shapes=[pltpu.VMEM((total, LANES), F32)],
        compiler_params=_cparams(("parallel",), 12 * total * LANES * 4),
        name="fox_gate",
    )(pre, zf3, bfp)


FOX_HEADS_PER_STEP = 4


def _fox_attn_kernel(q_ref, k_ref, v_ref, cq_ref, ck_ref, o_ref, m_s, l_s, acc_s,
                     *, tq, tk, qoff, scale, hps):
    hg = pl.program_id(1)
    i = pl.program_id(2)
    hd = acc_s.shape[2]
    lane = lax.broadcasted_iota(jnp.int32, cq_ref.shape[1:], 1)
    cq = cq_ref[0]
    fq = [jnp.sum(jnp.where(lane == hg * hps + hh, cq, 0.0), axis=1, keepdims=True) for hh in range(hps)]
    m_s[...] = jnp.full(m_s.shape, NEG_INF, F32)
    l_s[...] = jnp.zeros(l_s.shape, F32)
    acc_s[...] = jnp.zeros(acc_s.shape, F32)

    def tile(j, masked):
        ks = pl.multiple_of(j * tk, tk)
        scores = []
        for hh in range(hps):
            cols = slice(hh * hd, (hh + 1) * hd)
            k = k_ref[0, pl.ds(ks, tk), cols]
            fk = jnp.concatenate([ck_ref[0, j * (tk // LANES) + c, pl.ds(hg * hps + hh, 1), :]
                                  for c in range(tk // LANES)], axis=1)
            s = lax.dot_general(q_ref[:, cols], k, (((1,), (1,)), ((), ())), preferred_element_type=F32) * scale
            scores.append(s + fq[hh] - fk)
        for hh, s in enumerate(scores):
            cols = slice(hh * hd, (hh + 1) * hd)
            if masked:
                qpos = qoff + i * tq + lax.broadcasted_iota(jnp.int32, (tq, tk), 0)
                kpos = ks + lax.broadcasted_iota(jnp.int32, (tq, tk), 1)
                s = jnp.where(kpos <= qpos, s, NEG_INF)
            m_prev = m_s[hh]
            m_new = jnp.maximum(m_prev, jnp.max(s, axis=1, keepdims=True))
            alpha = jnp.exp(m_prev - m_new)
            p = jnp.exp(s - m_new)
            l_s[hh] = alpha * l_s[hh] + jnp.sum(p, axis=1, keepdims=True)
            v = v_ref[0, pl.ds(ks, tk), cols]
            acc_s[hh] = alpha * acc_s[hh] + jnp.dot(p.astype(BF16), v, preferred_element_type=F32)
            m_s[hh] = m_new

    def full_tile(j, c):
        tile(j, False)
        return c

    n_full = (qoff + i * tq) // tk
    lax.fori_loop(0, n_full, full_tile, 0)
    tile(n_full, True)
    for hh in range(hps):
        o_ref[:, hh * hd:(hh + 1) * hd] = (acc_s[hh] / l_s[hh]).astype(o_ref.dtype)


def fox_attention(q_all, row_off, bsz, t, k, v, cum, cumt, qoff, scale):
    tkeys = k.shape[1]
    hd = LANES
    nheads = q_all.shape[1] // hd
    hps = _pick(nheads, (FOX_HEADS_PER_STEP, 2, 1))
    tq = _pick(t, (256, 128, 64, 32, 16, 8))
    tk = tkeys if tq * tkeys * 4 <= (256 << 10) else _pick(tkeys, (256, 128))
    assert tk % tq == 0 and qoff % tq == 0 and qoff + t <= tkeys
    nq = t // tq
    rb0 = row_off // tq
    cq0 = qoff // tq
    vmem = (8 * tkeys * hps * hd * 2 + 2 * cumt.shape[2] * tkeys * 4 + 4 * hps * tq * tk * 4
            + 8 * hps * tq * hd * 4 + (2 << 20))
    return pl.pallas_call(
        functools.partial(_fox_attn_kernel, tq=tq, tk=tk, qoff=qoff, scale=scale, hps=hps),
        grid=(bsz, nheads // hps, nq),
        in_specs=[pl.BlockSpec((tq, hps * hd), lambda b, h, i: (rb0 + b * nq + i, h)),
                  pl.BlockSpec((1, tkeys, hps * hd), lambda b, h, i: (b, 0, h)),
                  pl.BlockSpec((1, tkeys, hps * hd), lambda b, h, i: (b, 0, h)),
                  pl.BlockSpec((1, tq, LANES), lambda b, h, i: (b, cq0 + i, 0)),
                  pl.BlockSpec((1,) + cumt.shape[1:], lambda b, h, i: (b, 0, 0, 0))],
        out_specs=pl.BlockSpec((tq, hps * hd), lambda b, h, i: (b * nq + i, h)),
        out_shape=jax.ShapeDtypeStruct((bsz * t, nheads * hd), BF16),
        scratch_shapes=[pltpu.VMEM((hps, tq, 1), F32), pltpu.VMEM((hps, tq, 1), F32), pltpu.VMEM((hps, tq, hd), F32)],
        compiler_params=_cparams(("parallel", "parallel", "arbitrary"), vmem),
        name="fox_attention",
    )(q_all, k, v, cum, cumt)


def _rwkv_prep_kernel(z_ref, zp_ref, sh_ref, mu_ref, w0_ref, a0_ref, w2_ref, a2_ref, g2_ref,
                      r_ref, k_ref, v_ref, lw_ref, a_ref, g_ref, *, w):
    i = pl.program_id(1)
    z = z_ref[...]
    prev_row = jnp.where(i == 0, sh_ref[0], zp_ref[7:8, :])
    row = lax.broadcasted_iota(jnp.int32, z.shape, 0)
    z_prev = jnp.where(row == 0, prev_row, pltpu.roll(z, 1, 0))
    zm = z + (z_prev - z) * mu_ref[...]
    r_ref[...] = zm[:, 0:w]
    k_ref[...] = zm[:, w:2 * w]
    v_ref[...] = zm[:, 2 * w:3 * w]
    slab = zm[:, 3 * w:]
    lora_w = jnp.dot(jnp.tanh(slab).astype(BF16), w2_ref[...], preferred_element_type=F32)
    x = -(w0_ref[...] + lora_w)
    softplus = jnp.maximum(x, 0.0) + jnp.log1p(jnp.exp(-jnp.abs(x)))
    lw_ref[...] = -jnp.exp(-softplus - 0.5)
    a_ref[...] = jax.nn.sigmoid(a0_ref[...] + jnp.dot(slab.astype(BF16), a2_ref[...], preferred_element_type=F32))
    g_ref[...] = jnp.dot(jax.nn.sigmoid(slab).astype(BF16), g2_ref[...], preferred_element_type=F32)


def rwkv_prep(zr, row_off, bsz, t, shift_prev, prm):
    wz = zr.shape[1]
    w = prm["w0"].shape[1]
    tt = _pick(t, (128, 64, 32, 16, 8))
    nt = t // tt
    rb0 = row_off // tt
    full = lambda shape: pl.BlockSpec(shape, lambda b, i: (0,) * len(shape))
    ospec = pl.BlockSpec((tt, w), lambda b, i: (b * nt + i, 0))
    oshape = jax.ShapeDtypeStruct((bsz * t, w), F32)
    ls = wz - 3 * w
    vmem = 6 * tt * wz * 4 + 12 * tt * w * 4 + 6 * ls * w * 2 + 8 * tt * w * 4
    return pl.pallas_call(
        functools.partial(_rwkv_prep_kernel, w=w),
        grid=(bsz, nt),
        in_specs=[pl.BlockSpec((tt, wz), lambda b, i: (rb0 + b * nt + i, 0)),
                  pl.BlockSpec((8, wz), lambda b, i: (jnp.maximum((row_off + (b * nt + i) * tt) // 8 - 1, 0), 0)),
                  pl.BlockSpec((1, 1, wz), lambda b, i: (b, 0, 0)),
                  full((1, wz)), full((1, w)), full((1, w)), full((ls, w)), full((ls, w)), full((ls, w))],
        out_specs=(ospec,) * 6,
        out_shape=(oshape,) * 6,
        compiler_params=_cparams(("parallel", "arbitrary"), vmem),
        name="rwkv_prep",
    )(zr, zr, shift_prev.reshape(bsz, 1, wz), prm["mu"], prm["w0"], prm["a0"], prm["w2"], prm["a2"], prm["g2"])


def _bdot(a, b, dims):
    return lax.dot_general(a.astype(BF16), b.astype(BF16), dims, preferred_element_type=F32)


def _cumsum_rows(tri, x):
    hi = x.astype(BF16)
    r1 = x - hi.astype(F32)
    mid = r1.astype(BF16)
    lo = (r1 - mid.astype(F32)).astype(BF16)
    t = tri.astype(BF16)
    dot = lambda p: lax.dot_general(t, p, _NN, preferred_element_type=F32)
    return dot(hi) + dot(mid) + dot(lo)


_NT = (((2,), (2,)), ((0,), (0,)))
_NN = (((2,), (1,)), ((0,), (0,)))
_TN = (((1,), (1,)), ((0,), (0,)))


def _rwkv_scan_kernel(r_ref, k_ref, v_ref, lw_ref, a_ref, g_ref, s0_ref, kk_ref, ka_ref, rk_ref,
                      lnw_ref, lnb_ref, y_ref, sout_ref, st_s, *, chunk):
    c = pl.program_id(2)

    @pl.when(c == 0)
    def _():
        st_s[...] = s0_ref[0]

    r = r_ref[0]
    k = k_ref[0]
    v = v_ref[0]
    lw = lw_ref[0]
    a = a_ref[0]
    hb = r.shape[0]
    s0 = st_s[...]

    kk = k * kk_ref[...]
    kk = kk / jnp.maximum(jnp.sqrt(jnp.sum(kk * kk, axis=-1, keepdims=True)), 1e-12)
    kmod = k * (1.0 + (a - 1.0) * ka_ref[...])

    li = lax.broadcasted_iota(jnp.int32, (chunk, chunk), 0)
    mi = lax.broadcasted_iota(jnp.int32, (chunk, chunk), 1)
    incl = (mi <= li).astype(F32)
    strict = (mi < li).astype(F32)
    cs = _cumsum_rows(jnp.broadcast_to(incl, (hb, chunk, chunk)), lw)
    dec_in = jnp.exp(cs)
    dec_ex = jnp.exp(cs - lw)
    inv = jnp.exp(-cs)
    p_rows = jnp.concatenate([-kk * dec_ex, r * dec_in], axis=1)
    q_rows = jnp.concatenate([kk * a * inv, kmod * inv], axis=1)
    mm = _bdot(p_rows, q_rows, _NT)
    a_ab = mm[:, :chunk, :chunk] * strict
    a_ak = mm[:, :chunk, chunk:] * strict
    r_b = mm[:, chunk:, :chunk] * incl
    r_k = mm[:, chunk:, chunk:] * incl
    ps = _bdot(p_rows, s0, _NT)
    x = ps[:, :chunk] + _bdot(a_ak, v, _NN)
    pw = a_ab
    n_iter = chunk.bit_length() - 1
    for it in range(n_iter):
        x = x + _bdot(pw, x, _NN)
        if it + 1 < n_iter:
            pw = _bdot(pw, pw, _NN)
    y = ps[:, chunk:] + _bdot(r_b, x, _NN) + _bdot(r_k, v, _NN)
    uv = jnp.concatenate([x, v], axis=1)
    s_new = (s0 + _bdot(uv, q_rows, _TN)) * dec_in[:, chunk - 1:chunk, :]
    st_s[...] = s_new
    sout_ref[0] = s_new

    mean = jnp.mean(y, axis=-1, keepdims=True)
    var = jnp.mean(jnp.square(y - mean), axis=-1, keepdims=True)
    yn = (y - mean) * lax.rsqrt(var + RWKV_LN_EPS) * lnw_ref[...] + lnb_ref[...]
    bonus = jnp.sum(r * kmod * rk_ref[...], axis=-1, keepdims=True) * v
    y_ref[0] = (yn + bonus) * g_ref[0]


def rwkv_scan(r, k, v, lw, a, g, s0, prm):
    bsz, nh, t, hd = r.shape
    chunk = _pick(t, (64, 32, 16, 8))
    hb = _pick(nh, (32, 16, 8, 4, 2, 1))
    nc = t // chunk
    xspec = pl.BlockSpec((1, hb, chunk, hd), lambda b, h, c: (b, h, c, 0))
    pspec = pl.BlockSpec((hb, 1, hd), lambda b, h, c: (h, 0, 0))
    sspec = pl.BlockSpec((1, hb, hd, hd), lambda b, h, c: (b, h, 0, 0))
    vmem = 16 * hb * chunk * LANES * 4 + 40 * hb * 2 * chunk * LANES * 4 + 6 * hb * hd * LANES * 4
    y, s = pl.pallas_call(
        functools.partial(_rwkv_scan_kernel, chunk=chunk),
        grid=(bsz, nh // hb, nc),
        in_specs=[xspec] * 6 + [sspec] + [pspec] * 5,
        out_specs=(xspec, sspec),
        out_shape=(jax.ShapeDtypeStruct((bsz, nh, t, hd), F32), jax.ShapeDtypeStruct((bsz, nh, hd, hd), F32)),
        scratch_shapes=[pltpu.VMEM((hb, hd, hd), F32)],
        compiler_params=_cparams(("parallel", "parallel", "arbitrary"), vmem),
        name="rwkv_scan",
    )(r, k, v, lw, a, g, s0, prm["k_k"], prm["k_a"], prm["r_k"], prm["ln_w"], prm["ln_b"])
    return y, s


def _router_kernel(x_ref, w_ref, comb_ref, *, n_experts):
    logits = jnp.dot(x_ref[...].astype(BF16), w_ref[...], preferred_element_type=F32)
    lane = lax.broadcasted_iota(jnp.int32, logits.shape, 1)
    big = jnp.int32(LANES)
    logits = jnp.where(lane < n_experts, logits, -jnp.inf)
    t1 = jnp.max(logits, axis=1, keepdims=True)
    i1 = jnp.min(jnp.where(logits == t1, lane, big), axis=1, keepdims=True)
    rest = jnp.where(lane == i1, -jnp.inf, logits)
    t2 = jnp.max(rest, axis=1, keepdims=True)
    i2 = jnp.min(jnp.where(rest == t2, lane, big), axis=1, keepdims=True)
    e2 = jnp.exp(t2 - t1)
    den = 1.0 + e2
    comb_ref[...] = jnp.where(lane == 0, 1.0 / den, jnp.where(lane == 1, e2 / den, jnp.where(
        lane == 2, i1.astype(F32), jnp.where(lane == 3, i2.astype(F32), 0.0))))


def moe_router(x, w_router):
    m, d = x.shape
    ne = w_router.shape[1]
    wp = jnp.zeros((d, LANES), BF16).at[:, :ne].set(w_router.astype(BF16))
    tm = _pick(m, (640, 512, 320, 256, 128, 64, 32, 16, 8))
    return pl.pallas_call(
        functools.partial(_router_kernel, n_experts=ne),
        grid=(m // tm,),
        in_specs=[pl.BlockSpec((tm, d), lambda i: (i, 0)), pl.BlockSpec((d, LANES), lambda i: (0, 0))],
        out_specs=pl.BlockSpec((tm, LANES), lambda i: (i, 0)),
        out_shape=jax.ShapeDtypeStruct((m, LANES), F32),
        compiler_params=_cparams(("parallel",), 3 * tm * d * 4 + 2 * d * LANES * 2 + 8 * tm * LANES * 4),
        name="moe_router",
    )(x, wp)


MOE_TILE = 256


def _moe_plan(rout, n_exp, tile):
    n = rout.shape[0]
    pair_e = rout[:, 2:4].astype(jnp.int32).reshape(-1)
    onehot = (pair_e[:, None] == jnp.arange(n_exp, dtype=jnp.int32)[None, :]).astype(jnp.int32)
    csum = jnp.cumsum(onehot, axis=0)
    rank = jnp.take_along_axis(csum - onehot, pair_e[:, None], axis=1)[:, 0]
    gsz = (csum[-1] + tile - 1) // tile * tile
    gend = jnp.cumsum(gsz)
    slot = ((gend - gsz)[pair_e] + rank).astype(jnp.int32)
    n_tiles = -(-2 * n // tile) + n_exp
    tok = jnp.zeros((n_tiles * tile,), jnp.int32).at[slot].set(jnp.arange(2 * n, dtype=jnp.int32) // 2)
    tile_start = jnp.arange(n_tiles, dtype=jnp.int32) * tile
    te = jnp.minimum(jnp.searchsorted(gend, tile_start, side="right"), n_exp - 1).astype(jnp.int32)
    used = (gend[-1] // tile).astype(jnp.int32).reshape(1)
    return slot, tok, te, used, n_tiles


def _row_copy(src_hbm, row, dst, r, sem):
    return pltpu.make_async_copy(src_hbm.at[pl.ds(row, 1)], dst.at[pl.ds(r, 1)], sem)


def _moe_gather_kernel(tok_ref, used_ref, x_hbm, o_ref, buf, sem, *, tile):
    i = pl.program_id(0)

    @pl.when(i < used_ref[0])
    def _():
        base = i * tile

        def issue(r, c):
            _row_copy(x_hbm, tok_ref[base + r], buf, r, sem).start()
            return c

        def wait(r, c):
            _row_copy(x_hbm, 0, buf, r, sem).wait()
            return c

        lax.fori_loop(0, tile, issue, 0)
        lax.fori_loop(0, tile, wait, 0)
        o_ref[...] = buf[...].astype(o_ref.dtype)

    @pl.when(i >= used_ref[0])
    def _():
        o_ref[...] = jnp.zeros(o_ref.shape, o_ref.dtype)


def moe_gather(x, tok, used, n_tiles, tile):
    d = x.shape[1]
    return pl.pallas_call(
        functools.partial(_moe_gather_kernel, tile=tile),
        grid_spec=pltpu.PrefetchScalarGridSpec(
            num_scalar_prefetch=2, grid=(n_tiles,),
            in_specs=[pl.BlockSpec(memory_space=pl.ANY)],
            out_specs=pl.BlockSpec((tile, d), lambda i, tok_r, used_r: (i, 0)),
            scratch_shapes=[pltpu.VMEM((tile, d), x.dtype), pltpu.SemaphoreType.DMA(())]),
        out_shape=jax.ShapeDtypeStruct((n_tiles * tile, d), BF16),
        compiler_params=_cparams(("arbitrary",), 4 * tile * d * 4),
        name="moe_gather",
    )(tok, used, x)


def _moe_up_kernel(te_ref, used_ref, x_ref, wg_ref, wu_ref, o_ref):
    i = pl.program_id(1)

    @pl.when(i < used_ref[0])
    def _():
        x = x_ref[...]
        g = jnp.dot(x, wg_ref[...], preferred_element_type=F32)
        u = jnp.dot(x, wu_ref[...], preferred_element_type=F32)
        o_ref[...] = (g * jax.nn.sigmoid(g) * u).astype(o_ref.dtype)

    @pl.when(i >= used_ref[0])
    def _():
        o_ref[...] = jnp.zeros(o_ref.shape, o_ref.dtype)


def moe_up(xs, wg, wu, te, used, tile):
    p_rows, d = xs.shape
    f = wg.shape[2]
    tn = _pick(f, (1024, 512, 256, 128))
    wspec = pl.BlockSpec((None, d, tn), lambda j, i, te_r, used_r: (te_r[i], 0, j))
    vmem = 2 * tile * d * 2 + 4 * d * tn * 2 + 6 * tile * tn * 4
    return pl.pallas_call(
        _moe_up_kernel,
        grid_spec=pltpu.PrefetchScalarGridSpec(
            num_scalar_prefetch=2, grid=(f // tn, p_rows // tile),
            in_specs=[pl.BlockSpec((tile, d), lambda j, i, te_r, used_r: (i, 0)), wspec, wspec],
            out_specs=pl.BlockSpec((tile, tn), lambda j, i, te_r, used_r: (i, j))),
        out_shape=jax.ShapeDtypeStruct((p_rows, f), BF16),
        compiler_params=_cparams(("parallel", "arbitrary"), vmem),
        name="moe_up",
    )(te, used, xs, wg, wu)


def _moe_down_kernel(te_ref, used_ref, x_ref, w_ref, o_ref):
    i = pl.program_id(1)

    @pl.when(i < used_ref[0])
    def _():
        o_ref[...] = jnp.dot(x_ref[...], w_ref[...], preferred_element_type=F32)

    @pl.when(i >= used_ref[0])
    def _():
        o_ref[...] = jnp.zeros(o_ref.shape, o_ref.dtype)


def moe_down(act, wd, te, used, tile):
    p_rows, f = act.shape
    d = wd.shape[2]
    tn = _pick(d, (512, 256, 128))
    vmem = 2 * tile * f * 2 + 2 * f * tn * 2 + 4 * tile * tn * 4
    return pl.pallas_call(
        _moe_down_kernel,
        grid_spec=pltpu.PrefetchScalarGridSpec(
            num_scalar_prefetch=2, grid=(d // tn, p_rows // tile),
            in_specs=[pl.BlockSpec((tile, f), lambda j, i, te_r, used_r: (i, 0)),
                      pl.BlockSpec((None, f, tn), lambda j, i, te_r, used_r: (te_r[i], 0, j))],
            out_specs=pl.BlockSpec((tile, tn), lambda j, i, te_r, used_r: (i, j))),
        out_shape=jax.ShapeDtypeStruct((p_rows, d), F32),
        compiler_params=_cparams(("parallel", "arbitrary"), vmem),
        name="moe_down",
    )(te, used, act, wd)


def _moe_combine_kernel(slot_ref, h_ref, g_ref, ys_hbm, fn_ref, o_ref, buf, sem, *, tc):
    base = pl.program_id(0) * tc

    def issue(r, c):
        p = 2 * (base + r)
        _row_copy(ys_hbm, slot_ref[p], buf.at[0], r, sem).start()
        _row_copy(ys_hbm, slot_ref[p + 1], buf.at[1], r, sem).start()
        return c

    def wait(r, c):
        _row_copy(ys_hbm, 0, buf.at[0], r, sem).wait()
        _row_copy(ys_hbm, 0, buf.at[1], r, sem).wait()
        return c

    lax.fori_loop(0, tc, issue, 0)
    lax.fori_loop(0, tc, wait, 0)
    g = g_ref[...]
    x = h_ref[...] + (g[:, 0:1] * buf[0] + g[:, 1:2] * buf[1])
    y = x * lax.rsqrt(jnp.mean(x * x, axis=-1, keepdims=True) + NORM_EPS)
    o_ref[...] = y * fn_ref[...]


def moe_combine_norm(h, rout, ys, slot, final_norm):
    n, d = h.shape
    tc = _pick(n, (256, 128, 64, 32, 16, 8))
    return pl.pallas_call(
        functools.partial(_moe_combine_kernel, tc=tc),
        grid_spec=pltpu.PrefetchScalarGridSpec(
            num_scalar_prefetch=1, grid=(n // tc,),
            in_specs=[pl.BlockSpec((tc, d), lambda i, s: (i, 0)),
                      pl.BlockSpec((tc, LANES), lambda i, s: (i, 0)),
                      pl.BlockSpec(memory_space=pl.ANY),
                      pl.BlockSpec((1, d), lambda i, s: (0, 0))],
            out_specs=pl.BlockSpec((tc, d), lambda i, s: (i, 0)),
            scratch_shapes=[pltpu.VMEM((2, tc, d), F32), pltpu.SemaphoreType.DMA(())]),
        out_shape=jax.ShapeDtypeStruct((n, d), F32),
        compiler_params=_cparams(("arbitrary",), 8 * tc * d * 4),
        name="moe_combine_norm",
    )(slot, h, rout, ys, final_norm.astype(F32).reshape(1, d))


def _rope_tables(pos, rope, width, lead):
    inv = ROPE_THETA ** (-jnp.arange(0, rope, 2, dtype=F32) / rope)
    ang = pos.astype(F32)[:, None] * inv[None, :]
    cos, sin = jnp.cos(ang), jnp.sin(ang)
    n = pos.shape[0]
    ctab = jnp.concatenate([jnp.ones((n, lead), F32), cos, cos, jnp.zeros((n, width - lead - rope), F32)], axis=1)
    stab = jnp.concatenate([jnp.zeros((n, lead), F32), -sin, sin, jnp.zeros((n, width - lead - rope), F32)], axis=1)
    return ctab, stab


def _swap_halves(w):
    half = w.shape[-1] // 2
    return jnp.concatenate([w[..., half:], w[..., :half]], axis=-1)


def kernel(x_prompt, x_sample, cache_mla_ckv, cache_mla_kpe, state_s5_re, state_s5_im, state_rwkv_wkv, state_rwkv_shift, cache_fox_k, cache_fox_v, cache_fox_logf, ln0_mix, w_in0, s5_a_re, s5_a_im, s5_log_dt, s5_b_re, s5_b_im, s5_c_re, s5_c_im, s5_d, s5_w_glu, s5_b_glu, mla_q_norm, mla_w_q_up, mla_kv_norm, mla_w_uk, mla_w_uv, w_out0, ln0_ffn, ffn_w_gate, ffn_w_up, ffn_w_down, ln1_mix, w_in1, rwkv_mu, rwkv_w0, rwkv_w2, rwkv_a0, rwkv_a2, rwkv_g2, rwkv_k_k, rwkv_k_a, rwkv_r_k, rwkv_ln_w, rwkv_ln_b, fox_b_f, w_out1, ln1_ffn, moe_w_router, moe_w_gate, moe_w_up, moe_w_down, final_norm):
    bp, tp, d = x_prompt.shape
    bs, ts, _ = x_sample.shape
    past = cache_mla_ckv.shape[1]
    n_p, n_s = bp * tp, bs * ts
    ntok = n_p + n_s
    streams = ((0, bp, tp), (n_p, bs, ts))

    h = jnp.concatenate([x_prompt.reshape(n_p, d), x_sample.reshape(n_s, d)], axis=0)

    s5_w = s5_d.shape[0]
    q_rank = mla_q_norm.shape[0]
    kv_rank = mla_kv_norm.shape[0]
    n_mh, qk = mla_w_q_up.shape[1], mla_w_q_up.shape[2]
    nope = mla_w_uk.shape[2]
    rope = qk - nope
    vdim = mla_w_uv.shape[2]
    mla_scale = float(qk) ** -0.5
    qw = 2 * LANES
    kvw = kv_rank + LANES

    (xn,) = rmsnorm(h, ln0_mix, (BF16,))
    w_in0b = w_in0.astype(BF16)
    (z_uq,) = matmul(xn, w_in0b[:, :s5_w + q_rank])
    off_kv = s5_w + q_rank
    w_kpe = w_in0b[:, off_kv + kv_rank:]
    zpad = jnp.zeros((d, kvw - kv_rank - rope), BF16)
    w_kv1 = jnp.concatenate([w_in0b[:, off_kv:off_kv + kv_rank], w_kpe, zpad], axis=1)
    w_kv2 = jnp.concatenate([jnp.zeros((d, kv_rank), BF16), _swap_halves(w_kpe), zpad], axis=1)
    pos_p = jnp.arange(tp)
    pos_s = past + jnp.arange(ts)

    def token_tables(width, lead):
        cp, sp = _rope_tables(pos_p, rope, width, lead)
        cs, ss = _rope_tables(pos_s, rope, width, lead)
        return (jnp.concatenate([jnp.tile(cp, (bp, 1)), jnp.tile(cs, (bs, 1))], axis=0),
                jnp.concatenate([jnp.tile(sp, (bp, 1)), jnp.tile(ss, (bs, 1))], axis=0))

    ckv_c, ckv_s = token_tables(kvw, kv_rank)
    z_kv = rope_matmul(xn, w_kv1, w_kv2, ckv_c, ckv_s, F32, tn=kvw)
    ckv_f, ckv_b = rmsnorm(z_kv, mla_kv_norm, (F32, BF16), col_block=0, width=kv_rank)
    kpe_f = z_kv[:, kv_rank:kv_rank + rope]
    kpe_b = z_kv[:, kv_rank:].astype(BF16)

    (cqn,) = rmsnorm(z_uq, mla_q_norm, (BF16,), col_block=s5_w // q_rank, width=q_rank)
    wq = mla_w_q_up.astype(BF16)
    zq = jnp.zeros((q_rank, n_mh, qw - qk), BF16)
    wq1 = jnp.concatenate([wq, zq], axis=2).reshape(q_rank, n_mh * qw)
    wq2 = jnp.concatenate([jnp.zeros((q_rank, n_mh, nope), BF16), _swap_halves(wq[:, :, nope:]), zq],
                          axis=2).reshape(q_rank, n_mh * qw)
    q_c, q_s = token_tables(qw, nope)
    q_all = rope_matmul(cqn, wq1, wq2, q_c, q_s, BF16, tn=qw)

    s5p = s5_params(s5_a_re, s5_a_im, s5_log_dt, s5_b_re, s5_b_im, s5_c_re, s5_c_im, s5_d, s5_w_glu, s5_b_glu)
    g5, p5 = s5_a_re.shape
    wuk = jnp.transpose(mla_w_uk, (1, 2, 0)).astype(BF16)
    wuv = jnp.transpose(mla_w_uv, (1, 0, 2)).astype(BF16)

    y_s5, s5_re, s5_im, y_mla = [], [], [], []
    for si, (off, bsz, t) in enumerate(streams):
        if si == 0:
            h0r = jnp.zeros((bsz, g5 * p5), F32)
            h0i = h0r
            ckv_k = ckv_b[off:off + bsz * t].reshape(bsz, t, kv_rank)
            kpe_k = kpe_b[off:off + bsz * t].reshape(bsz, t, LANES)
            klen, causal = t, True
        else:
            h0r, h0i = state_s5_re.astype(F32), state_s5_im.astype(F32)
            klen, causal = past + t, False
            padk = -(-klen // LANES) * LANES - klen
            ckv_k = jnp.concatenate([cache_mla_ckv.astype(BF16), ckv_b[off:off + bsz * t].reshape(bsz, t, kv_rank),
                                     jnp.zeros((bsz, padk, kv_rank), BF16)], axis=1)
            kpe_cache = jnp.concatenate([cache_mla_kpe.astype(BF16),
                                         jnp.zeros((bsz, past, LANES - rope), BF16)], axis=2)
            kpe_k = jnp.concatenate([kpe_cache, kpe_b[off:off + bsz * t].reshape(bsz, t, LANES),
                                     jnp.zeros((bsz, padk, LANES), BF16)], axis=1)
        ys, hr, hi = s5_mixer(z_uq, off, bsz, t, h0r, h0i, s5p)
        y_s5.append(ys)
        s5_re.append(hr.reshape(bsz, g5, p5))
        s5_im.append(hi.reshape(bsz, g5, p5))
        y_mla.append(mla_attention(q_all, off, bsz, t, ckv_k, kpe_k, wuk, wuv, klen, causal, mla_scale))

    mix0 = jnp.concatenate([jnp.concatenate(y_s5, axis=0), jnp.concatenate(y_mla, axis=0)], axis=1)
    (h,) = matmul(mix0, w_out0.astype(BF16), res=h)
    (hn,) = rmsnorm(h, ln0_ffn, (BF16,))
    dff = ffn_w_gate.shape[1]
    dff_pad = -(-dff // 512) * 512
    padc = lambda wgt: jnp.pad(wgt.astype(BF16), ((0, 0), (0, dff_pad - dff)))
    act = swiglu_up(hn, padc(ffn_w_gate), padc(ffn_w_up))
    (h,) = matmul(act, jnp.pad(ffn_w_down.astype(BF16), ((0, dff_pad - dff), (0, 0))), res=h, tk=_pick(dff_pad, (1024, 512)))

    rw = rwkv_w0.shape[0]
    nh_r, hd_r = rwkv_k_k.shape
    shift_w = rwkv_mu.shape[0]
    lora_w = shift_w - 3 * rw
    slab = -(-lora_w // LANES) * LANES
    wz = 3 * rw + slab
    nh_f = fox_b_f.shape[0]
    fw = (w_in1.shape[1] - shift_w - nh_f) // 3
    fox_scale = float(fw // nh_f) ** -0.5
    d_lw, d_la = rwkv_w2.shape[0], rwkv_a2.shape[0]

    (xn,) = rmsnorm(h, ln1_mix, (BF16,))
    w_in1b = w_in1.astype(BF16)
    (zr,) = matmul(xn, jnp.pad(w_in1b[:, :shift_w], ((0, 0), (0, wz - shift_w))))
    (fq,) = matmul(xn, w_in1b[:, shift_w:shift_w + fw], (BF16,))
    fk_f, fk_b = matmul(xn, w_in1b[:, shift_w + fw:shift_w + 2 * fw], (F32, BF16))
    fv_f, fv_b = matmul(xn, w_in1b[:, shift_w + 2 * fw:shift_w + 3 * fw], (F32, BF16))
    (zf,) = matmul(xn, jnp.pad(w_in1b[:, shift_w + 3 * fw:], ((0, 0), (0, LANES - nh_f))))

    padrow = lambda wgt, lo: jnp.zeros((slab, rw), BF16).at[lo:lo + wgt.shape[0]].set(wgt.astype(BF16))
    row2 = lambda x_, n_: x_.astype(F32).reshape(1, n_)
    head3 = lambda x_: x_.astype(F32).reshape(nh_r, 1, hd_r)
    rprm = dict(mu=jnp.pad(row2(rwkv_mu, shift_w), ((0, 0), (0, wz - shift_w))), w0=row2(rwkv_w0, rw),
                a0=row2(rwkv_a0, rw), w2=padrow(rwkv_w2, 0), a2=padrow(rwkv_a2, d_lw),
                g2=padrow(rwkv_g2, d_lw + d_la), k_k=head3(rwkv_k_k), k_a=head3(rwkv_k_a), r_k=head3(rwkv_r_k),
                ln_w=head3(rwkv_ln_w), ln_b=head3(rwkv_ln_b))

    y_r, wkv, shift_new, y_f, logf_out = [], [], [], [], []
    for si, (off, bsz, t) in enumerate(streams):
        if si == 0:
            shift_prev = jnp.zeros((bsz, wz), F32)
            s0 = jnp.zeros((bsz, nh_r, hd_r, hd_r), F32)
            pre = None
            k_all = fk_b[off:off + bsz * t].reshape(bsz, t, fw)
            v_all = fv_b[off:off + bsz * t].reshape(bsz, t, fw)
            qoff = 0
        else:
            shift_prev = jnp.pad(state_rwkv_shift.astype(F32), ((0, 0), (0, wz - shift_w)))
            s0 = state_rwkv_wkv.astype(F32)
            pre = jnp.pad(cache_fox_logf.astype(F32), ((0, 0), (0, 0), (0, LANES - nh_f)))
            qoff = cache_fox_k.shape[1]
            padk = -(-(qoff + t) // LANES) * LANES - (qoff + t)
            cat = lambda cache, new: jnp.concatenate(
                [cache.reshape(bsz, qoff, fw).astype(BF16), new[off:off + bsz * t].reshape(bsz, t, fw),
                 jnp.zeros((bsz, padk, fw), BF16)], axis=1)
            k_all, v_all = cat(cache_fox_k, fk_b), cat(cache_fox_v, fv_b)
        parts = rwkv_prep(zr, off, bsz, t, shift_prev, rprm)
        to_heads = lambda x_: jnp.transpose(x_.reshape(bsz, t, nh_r, hd_r), (0, 2, 1, 3))
        yh, s_fin = rwkv_scan(*[to_heads(x_) for x_ in parts], s0, rprm)
        y_r.append(jnp.transpose(yh, (0, 2, 1, 3)).reshape(bsz * t, rw).astype(BF16))
        wkv.append(s_fin)
        shift_new.append(zr[off:off + bsz * t].reshape(bsz, t, wz)[:, -1, :shift_w])
        logf, cum, cumt = fox_gate(zf, off, bsz, t, fox_b_f, pre)
        logf_out.append(logf[:, :, :nh_f])
        y_f.append(fox_attention(fq, off, bsz, t, k_all, v_all, cum, cumt, qoff, fox_scale))

    mix1 = jnp.concatenate([jnp.concatenate(y_r, axis=0), jnp.concatenate(y_f, axis=0)], axis=1)
    (h,) = matmul(mix1, w_out1.astype(BF16), res=h)
    (hn,) = rmsnorm(h, ln1_ffn, (F32,))
    rout = moe_router(hn, moe_w_router)
    n_exp = moe_w_gate.shape[0]
    slot, tok, te, used, n_tiles = _moe_plan(rout, n_exp, MOE_TILE)
    xs = moe_gather(hn, tok, used, n_tiles, MOE_TILE)
    act = moe_up(xs, moe_w_gate.astype(BF16), moe_w_up.astype(BF16), te, used, MOE_TILE)
    ys = moe_down(act, moe_w_down.astype(BF16), te, used, MOE_TILE)
    y_all = moe_combine_norm(h, rout, ys, slot, final_norm)
    nfh = fw // nh_f
    outs = [y_all[:n_p].reshape(bp, tp, d), y_all[n_p:].reshape(bs, ts, d)]
    for si, (off, bsz, t) in enumerate(streams):
        rows = slice(off, off + bsz * t)
        outs += [ckv_f[rows].reshape(bsz, t, kv_rank), kpe_f[rows].reshape(bsz, t, rope), s5_re[si], s5_im[si],
                 wkv[si], shift_new[si], fk_f[rows].reshape(bsz, t, nh_f, nfh), fv_f[rows].reshape(bsz, t, nh_f, nfh),
                 logf_out[si]]
    return tuple(outs)
```

```python
import functools
import math

import jax
import jax.numpy as jnp
from jax import lax
from jax.experimental import pallas as pl
from jax.experimental.pallas import tpu as pltpu

F32 = jnp.float32
BF16 = jnp.bfloat16

V7X_VMEM_BYTES = 64 * 1024 * 1024
VMEM_CAP = V7X_VMEM_BYTES - 8 * 1024 * 1024
LANES = 128

NORM_EPS = 1e-6
NEG_INF = -1e30
CHUNK = 64
ROPE_THETA = 10000.0
RWKV_LN_EPS = 64e-5
S5_BLOCK_GROUPS = 8


def _pick(n, cands):
    for c in cands:
        if c <= n and n % c == 0:
            return c
    return n


def _cparams(sem, vmem_bytes):
    limit = int(min(max(vmem_bytes * 1.25 + (4 << 20), 24 << 20), VMEM_CAP))
    return pltpu.CompilerParams(dimension_semantics=sem, vmem_limit_bytes=limit)


def _rmsnorm_kernel(x_ref, g_ref, *o_refs):
    x = x_ref[...].astype(F32)
    y = x * lax.rsqrt(jnp.mean(x * x, axis=-1, keepdims=True) + NORM_EPS)
    y = y * g_ref[...]
    for o in o_refs:
        o[...] = y.astype(o.dtype)


def rmsnorm(x, g, out_dtypes, col_block=0, width=None):
    m = x.shape[0]
    width = x.shape[1] if width is None else width
    tm = _pick(m, (512, 320, 256, 128, 64, 32, 16, 8))
    outs = tuple(jax.ShapeDtypeStruct((m, width), d) for d in out_dtypes)
    res = pl.pallas_call(
        _rmsnorm_kernel,
        grid=(m // tm,),
        in_specs=[pl.BlockSpec((tm, width), lambda i: (i, col_block)),
                  pl.BlockSpec((1, width), lambda i: (0, 0))],
        out_specs=tuple(pl.BlockSpec((tm, width), lambda i: (i, 0)) for _ in out_dtypes),
        out_shape=outs,
        compiler_params=_cparams(("parallel",), tm * width * 4 * 2 * (1 + len(out_dtypes))),
        name="rmsnorm",
    )(x, g.reshape(1, width).astype(F32))
    return res


def _mm_kernel(*refs, nk, has_res, has_scale, n_out):
    x_ref, w_ref = refs[0], refs[1]
    pos = 2
    res_ref = scale_ref = None
    if has_res:
        res_ref = refs[pos]
        pos += 1
    if has_scale:
        scale_ref = refs[pos]
        pos += 1
    o_refs = refs[pos:pos + n_out]
    acc_ref = refs[pos + n_out] if nk > 1 else None

    part = jnp.dot(x_ref[...].astype(BF16), w_ref[...].astype(BF16), preferred_element_type=F32)

    def finish(acc):
        if has_scale:
            acc = acc * scale_ref[...]
        if has_res:
            acc = res_ref[...] + acc
        for o in o_refs:
            o[...] = acc.astype(o.dtype)

    if nk == 1:
        finish(part)
    else:
        k = pl.program_id(2)

        @pl.when(k == 0)
        def _():
            acc_ref[...] = part

        @pl.when(k > 0)
        def _():
            acc_ref[...] += part

        @pl.when(k == nk - 1)
        def _():
            finish(acc_ref[...])


def matmul(x, w, out_dtypes=(F32,), res=None, scale=None, w_index=None, tm=None, tn=None, tk=None):
    m, kdim = x.shape
    n = w.shape[-1]
    tm = tm or _pick(m, (1280, 1024, 640, 512, 320, 256, 128, 64, 32, 16, 8))
    tn = tn or _pick(n, (512, 384, 256, 128))
    tk = tk or (kdim if kdim <= 4096 else _pick(kdim, (2048, 1792, 1024, 512, 256, 128)))
    nk = kdim // tk
    grid = (m // tm, n // tn, nk)
    in_specs = [pl.BlockSpec((tm, tk), lambda i, j, k: (i, k))]
    if w.ndim == 3:
        in_specs.append(pl.BlockSpec((None, tk, tn), lambda i, j, k: (w_index, k, j)))
    else:
        in_specs.append(pl.BlockSpec((tk, tn), lambda i, j, k: (k, j)))
    args = [x, w]
    if res is not None:
        in_specs.append(pl.BlockSpec((tm, tn), lambda i, j, k: (i, j)))
        args.append(res)
    if scale is not None:
        in_specs.append(pl.BlockSpec((tm, 1), lambda i, j, k: (i, 0)))
        args.append(scale)
    out_specs = tuple(pl.BlockSpec((tm, tn), lambda i, j, k: (i, j)) for _ in out_dtypes)
    out_shape = tuple(jax.ShapeDtypeStruct((m, n), d) for d in out_dtypes)
    scratch = [pltpu.VMEM((tm, tn), F32)] if nk > 1 else []
    vmem = (2 * tm * tk * x.dtype.itemsize + 2 * tk * tn * w.dtype.itemsize
            + tm * tn * 4 * (2 * len(out_dtypes) + 1 + (2 if res is not None else 0)))
    outs = pl.pallas_call(
        functools.partial(_mm_kernel, nk=nk, has_res=res is not None, has_scale=scale is not None,
                          n_out=len(out_dtypes)),
        grid=grid, in_specs=in_specs, out_specs=out_specs, out_shape=out_shape,
        scratch_shapes=scratch,
        compiler_params=_cparams(("parallel", "parallel", "arbitrary"), vmem),
        name="matmul",
    )(*args)
    return outs


def _rope_mm_kernel(x_ref, w1_ref, w2_ref, c_ref, s_ref, o_ref):
    x = x_ref[...]
    a = jnp.dot(x, w1_ref[...], preferred_element_type=F32)
    b = jnp.dot(x, w2_ref[...], preferred_element_type=F32)
    o_ref[...] = (a * c_ref[...] + b * s_ref[...]).astype(o_ref.dtype)


def rope_matmul(x, w1, w2, ctab, stab, out_dtype, tn):
    m, kdim = x.shape
    n = w1.shape[1]
    est = lambda rows: 2 * rows * kdim * 2 + 4 * kdim * tn * 2 + 8 * rows * tn * 4
    tm = next((c for c in (1280, 1024, 640, 512, 320, 256, 128, 64, 32, 16, 8)
               if m % c == 0 and est(c) <= VMEM_CAP // 2), 8)
    vmem = est(tm)
    return pl.pallas_call(
        _rope_mm_kernel,
        grid=(m // tm, n // tn),
        in_specs=[pl.BlockSpec((tm, kdim), lambda i, j: (i, 0)),
                  pl.BlockSpec((kdim, tn), lambda i, j: (0, j)),
                  pl.BlockSpec((kdim, tn), lambda i, j: (0, j)),
                  pl.BlockSpec((tm, tn), lambda i, j: (i, 0)),
                  pl.BlockSpec((tm, tn), lambda i, j: (i, 0))],
        out_specs=pl.BlockSpec((tm, tn), lambda i, j: (i, j)),
        out_shape=jax.ShapeDtypeStruct((m, n), out_dtype),
        compiler_params=_cparams(("parallel", "parallel"), vmem),
        name="rope_matmul",
    )(x, w1, w2, ctab, stab)


def _swiglu_up_kernel(x_ref, wg_ref, wu_ref, o_ref):
    x = x_ref[...]
    g = jnp.dot(x, wg_ref[...].astype(BF16), preferred_element_type=F32)
    u = jnp.dot(x, wu_ref[...].astype(BF16), preferred_element_type=F32)
    o_ref[...] = (g * jax.nn.sigmoid(g) * u).astype(o_ref.dtype)


def swiglu_up(x, wg, wu, w_index=None):
    m, kdim = x.shape
    n = wg.shape[-1]
    tm = _pick(m, (1280, 1024, 640, 512, 320, 256, 128, 64, 32, 16, 8))
    tn = _pick(n, (256, 128))
    if wg.ndim == 3:
        wspec = pl.BlockSpec((None, kdim, tn), lambda i, j: (w_index, 0, j))
    else:
        wspec = pl.BlockSpec((kdim, tn), lambda i, j: (0, j))
    vmem = 2 * tm * kdim * 2 + 4 * kdim * tn * wg.dtype.itemsize + 6 * tm * tn * 4
    return pl.pallas_call(
        _swiglu_up_kernel,
        grid=(m // tm, n // tn),
        in_specs=[pl.BlockSpec((tm, kdim), lambda i, j: (i, 0)), wspec, wspec],
        out_specs=pl.BlockSpec((tm, tn), lambda i, j: (i, j)),
        out_shape=jax.ShapeDtypeStruct((m, n), BF16),
        compiler_params=_cparams(("parallel", "parallel"), vmem),
        name="swiglu_up",
    )(x, wg, wu)


def _s5_kernel(u_ref, h0r_ref, h0i_ref, lr_ref, li_ref, bbr_ref, bbi_ref, ccr_ref, cci_ref,
               d_ref, wglu_ref, bglu_ref, y_ref, hr_ref, hi_ref, xr_s, xi_s, st_r, st_i, *, tc, nblk):
    c = pl.program_id(1)

    @pl.when(c == 0)
    def _():
        st_r[...] = h0r_ref[0]
        st_i[...] = h0i_ref[0]

    u = u_ref[...]
    ub = u.astype(BF16)
    sw = xr_s.shape[1] // nblk
    for k in range(nblk):
        uk = ub[:, k * LANES:(k + 1) * LANES]
        xr_s[:, k * sw:(k + 1) * sw] = jnp.dot(uk, bbr_ref[k], preferred_element_type=F32)
        xi_s[:, k * sw:(k + 1) * sw] = jnp.dot(uk, bbi_ref[k], preferred_element_type=F32)

    scan_w = 1024
    for q in range(xr_s.shape[1] // scan_w):
        cols = slice(q * scan_w, (q + 1) * scan_w)
        lr = lr_ref[:, cols]
        li = li_ref[:, cols]

        def body(t, carry, cols=cols, lr=lr, li=li):
            hr, hi = carry
            nr = lr * hr - li * hi + xr_s[pl.ds(t, 1), cols]
            ni = lr * hi + li * hr + xi_s[pl.ds(t, 1), cols]
            xr_s[pl.ds(t, 1), cols] = nr
            xi_s[pl.ds(t, 1), cols] = ni
            return nr, ni

        hr, hi = lax.fori_loop(0, tc, body, (st_r[:, cols], st_i[:, cols]))
        st_r[:, cols] = hr
        st_i[:, cols] = hi

    hr_ref[0] = st_r[...]
    hi_ref[0] = st_i[...]

    ys = []
    for k in range(nblk):
        xr = xr_s[:, k * sw:(k + 1) * sw].astype(BF16)
        xi = xi_s[:, k * sw:(k + 1) * sw].astype(BF16)
        ys.append(jnp.dot(xr, ccr_ref[k], preferred_element_type=F32)
                  - jnp.dot(xi, cci_ref[k], preferred_element_type=F32))
    y = jnp.concatenate(ys, axis=1) + d_ref[...] * u
    y = jax.nn.gelu(y)
    gate = jax.nn.sigmoid(jnp.dot(y.astype(BF16), wglu_ref[...], preferred_element_type=F32) + bglu_ref[...])
    y_ref[...] = (y * gate).astype(y_ref.dtype)


def s5_mixer(z, row_off, bsz, t, h0_re, h0_im, prm):
    width = prm["d"].shape[1]
    nblk = width // LANES
    nstate = prm["lr"].shape[1]
    tc = _pick(t, (256, 128, 64, 32, 16, 8))
    nt = t // tc
    rb0 = row_off // tc
    full = lambda shape: pl.BlockSpec(shape, lambda b, c: (0,) * len(shape))
    vmem = (4 * tc * width * 4 + 2 * tc * nstate * 4 + 4 * nblk * LANES * (nstate // nblk) * 2 * 2
            + 2 * width * width * 2 + 8 * tc * width * 4)
    y, hr, hi = pl.pallas_call(
        functools.partial(_s5_kernel, tc=tc, nblk=nblk),
        grid=(bsz, nt),
        in_specs=[pl.BlockSpec((tc, width), lambda b, c: (rb0 + b * nt + c, 0)),
                  pl.BlockSpec((1, 1, nstate), lambda b, c: (b, 0, 0)),
                  pl.BlockSpec((1, 1, nstate), lambda b, c: (b, 0, 0)),
                  full((1, nstate)), full((1, nstate)),
                  full(prm["bbr"].shape), full(prm["bbi"].shape),
                  full(prm["ccr"].shape), full(prm["cci"].shape),
                  full((1, width)), full((width, width)), full((1, width))],
        out_specs=(pl.BlockSpec((tc, width), lambda b, c: (b * nt + c, 0)),
                   pl.BlockSpec((1, 1, nstate), lambda b, c: (b, 0, 0)),
                   pl.BlockSpec((1, 1, nstate), lambda b, c: (b, 0, 0))),
        out_shape=(jax.ShapeDtypeStruct((bsz * t, width), BF16),
                   jax.ShapeDtypeStruct((bsz, 1, nstate), F32),
                   jax.ShapeDtypeStruct((bsz, 1, nstate), F32)),
        scratch_shapes=[pltpu.VMEM((tc, nstate), F32), pltpu.VMEM((tc, nstate), F32),
                        pltpu.VMEM((1, nstate), F32), pltpu.VMEM((1, nstate), F32)],
        compiler_params=_cparams(("parallel", "arbitrary"), vmem),
        name="s5_mixer",
    )(z, h0_re.reshape(bsz, 1, nstate), h0_im.reshape(bsz, 1, nstate), prm["lr"], prm["li"],
      prm["bbr"], prm["bbi"], prm["ccr"], prm["cci"], prm["d"], prm["wglu"], prm["bglu"])
    return y, hr, hi


def s5_params(a_re, a_im, log_dt, b_re, b_im, c_re, c_im, d_skip, w_glu, b_glu):
    g, p = a_re.shape
    nch = b_re.shape[2]
    dt = jnp.exp(log_dt.astype(F32))[:, None]
    ar, ai = a_re.astype(F32), a_im.astype(F32)
    mag = jnp.exp(ar * dt)
    lr = mag * jnp.cos(ai * dt)
    li = mag * jnp.sin(ai * dt)
    den = ar * ar + ai * ai
    fr = ((lr - 1.0) * ar + li * ai) / den
    fi = (li * ar - (lr - 1.0) * ai) / den
    br, bi = b_re.astype(F32), b_im.astype(F32)
    bbr = fr[..., None] * br - fi[..., None] * bi
    bbi = fr[..., None] * bi + fi[..., None] * br
    gb = S5_BLOCK_GROUPS
    nblk = g // gb
    eye = jnp.eye(gb, dtype=F32)

    def blk_in(m):
        m = m.reshape(nblk, gb, p, nch)
        return jnp.einsum("kgpn,gh->kgnhp", m, eye).reshape(nblk, gb * nch, gb * p).astype(BF16)

    def blk_out(m):
        m = m.astype(F32).reshape(nblk, gb, nch, p)
        return jnp.einsum("kgnp,gh->kgphn", m, eye).reshape(nblk, gb * p, gb * nch).astype(BF16)

    width = g * nch
    return dict(lr=lr.reshape(1, g * p), li=li.reshape(1, g * p), bbr=blk_in(bbr), bbi=blk_in(bbi),
                ccr=blk_out(c_re), cci=blk_out(c_im), d=d_skip.astype(F32).reshape(1, width),
                wglu=w_glu.astype(BF16), bglu=b_glu.astype(F32).reshape(1, width))


MLA_ROWS = 1024
MLA_CHAIN_ROWS = 256


def _mla_kernel(q_ref, ckv_ref, kpe_ref, wuk_ref, wuv_ref, o_ref, qa_s, qpe_s, m_s, l_s, acc_s,
                *, tq, tk, nk_total, klen, causal, scale, hps):
    i = pl.program_id(1)
    nope = wuk_ref.shape[1]
    qw = q_ref.shape[1] // hps
    vdim = wuv_ref.shape[2]
    rows_all = hps * tq
    rc = MLA_CHAIN_ROWS if rows_all % MLA_CHAIN_ROWS == 0 else rows_all
    for hh in range(hps):
        rows = slice(hh * tq, (hh + 1) * tq)
        qa = jnp.dot(q_ref[:, hh * qw:hh * qw + nope], wuk_ref[hh], preferred_element_type=F32)
        qa_s[rows, :] = qa.astype(qa_s.dtype)
        qpe_s[rows, :] = q_ref[:, hh * qw + nope:(hh + 1) * qw]
    m_s[...] = jnp.full(m_s.shape, NEG_INF, F32)
    l_s[...] = jnp.zeros(l_s.shape, F32)
    acc_s[...] = jnp.zeros(acc_s.shape, F32)

    def tile(j, masked):
        ks = pl.multiple_of(j * tk, tk)
        ckv = ckv_ref[0, pl.ds(ks, tk), :]
        kpe = kpe_ref[0, pl.ds(ks, tk), :]
        scores = []
        for c0 in range(0, rows_all, rc):
            rows = slice(c0, c0 + rc)
            s = lax.dot_general(qa_s[rows, :].astype(BF16), ckv, (((1,), (1,)), ((), ())), preferred_element_type=F32)
            s = s + lax.dot_general(qpe_s[rows, :], kpe, (((1,), (1,)), ((), ())), preferred_element_type=F32)
            scores.append(s * scale)
        for c0, s in zip(range(0, rows_all, rc), scores):
            rows = slice(c0, c0 + rc)
            if masked:
                kpos = ks + lax.broadcasted_iota(jnp.int32, (rc, tk), 1)
                if causal:
                    qpos = i * tq + (c0 + lax.broadcasted_iota(jnp.int32, (rc, tk), 0)) % tq
                    s = jnp.where(kpos // CHUNK <= qpos // CHUNK, s, NEG_INF)
                if klen < nk_total * tk:
                    s = jnp.where(kpos < klen, s, NEG_INF)
            m_prev = m_s[rows, :]
            m_new = jnp.maximum(m_prev, jnp.max(s, axis=1, keepdims=True))
            alpha = jnp.exp(m_prev - m_new)
            p = jnp.exp(s - m_new)
            l_s[rows, :] = alpha * l_s[rows, :] + jnp.sum(p, axis=1, keepdims=True)
            acc_s[rows, :] = alpha * acc_s[rows, :] + jnp.dot(p.astype(BF16), ckv, preferred_element_type=F32)
            m_s[rows, :] = m_new

    def full_tile(j, c):
        tile(j, False)
        return c

    n_full = i if causal else nk_total - 1
    lax.fori_loop(0, n_full, full_tile, 0)
    tile(n_full, True)
    o_lat = (acc_s[...] / l_s[...]).astype(BF16)
    for hh in range(hps):
        o_ref[:, hh * vdim:(hh + 1) * vdim] = jnp.dot(
            o_lat[hh * tq:(hh + 1) * tq], wuv_ref[hh], preferred_element_type=F32).astype(o_ref.dtype)


def mla_attention(q_all, row_off, bsz, t, ckv, kpe, wuk, wuv, klen, causal, scale):
    nheads, nope, lat = wuk.shape
    vdim = wuv.shape[2]
    qw = q_all.shape[1] // nheads
    tkeys = ckv.shape[1]
    tq = _pick(t, (256, 128, 64, 32, 16, 8))
    tk = tq if causal else _pick(tkeys, (256, 128))
    if causal:
        assert tq % CHUNK == 0 and tkeys == t
    hps = _pick(nheads, tuple(c for c in (24, 16, 12, 8, 6, 4, 3, 2, 1) if c * tq <= MLA_ROWS))
    nq = t // tq
    rb0 = row_off // tq
    rows_all = hps * tq
    vmem = (2 * tkeys * (lat + LANES) * 2 + 4 * tq * hps * qw * 2 + 3 * rows_all * lat * 4 + 8 * rows_all * tk * 4
            + 4 * hps * (nope + vdim) * lat * 2 + (2 << 20))
    return pl.pallas_call(
        functools.partial(_mla_kernel, tq=tq, tk=tk, nk_total=tkeys // tk, klen=klen, causal=causal, scale=scale,
                          hps=hps),
        grid=(bsz, nq, nheads // hps),
        in_specs=[pl.BlockSpec((tq, hps * qw), lambda b, i, h: (rb0 + b * nq + i, h)),
                  pl.BlockSpec((1, tkeys, lat), lambda b, i, h: (b, 0, 0)),
                  pl.BlockSpec((1, tkeys, LANES), lambda b, i, h: (b, 0, 0)),
                  pl.BlockSpec((hps, nope, lat), lambda b, i, h: (h, 0, 0)),
                  pl.BlockSpec((hps, lat, vdim), lambda b, i, h: (h, 0, 0))],
        out_specs=pl.BlockSpec((tq, hps * vdim), lambda b, i, h: (b * nq + i, h)),
        out_shape=jax.ShapeDtypeStruct((bsz * t, nheads * vdim), BF16),
        scratch_shapes=[pltpu.VMEM((rows_all, lat), F32), pltpu.VMEM((rows_all, LANES), BF16),
                        pltpu.VMEM((rows_all, 1), F32), pltpu.VMEM((rows_all, 1), F32),
                        pltpu.VMEM((rows_all, lat), F32)],
        compiler_params=_cparams(("parallel", "parallel", "arbitrary"), vmem),
        name="mla_attention",
    )(q_all, ckv, kpe, wuk, wuv)


def _fox_gate_kernel(pre_ref, zf_ref, bf_ref, logf_ref, cum_ref, cumt_ref, lf_s, *, npre, t, blk):
    total = lf_s.shape[0]
    z = zf_ref[0] + bf_ref[...]
    logf = jnp.minimum(z, 0.0) - jnp.log1p(jnp.exp(-jnp.abs(z)))
    logf_ref[0] = logf
    if npre + t < total:
        lf_s[...] = jnp.zeros(lf_s.shape, F32)
    if npre:
        lf_s[0:npre, :] = pre_ref[0]
    lf_s[npre:npre + t, :] = logf
    tri = (lax.broadcasted_iota(jnp.int32, (blk, blk), 1)
           <= lax.broadcasted_iota(jnp.int32, (blk, blk), 0)).astype(F32)
    carry = jnp.zeros((1, LANES), F32)
    for c in range(total // blk):
        rows = slice(c * blk, (c + 1) * blk)
        cum = jnp.dot(tri, lf_s[rows, :], preferred_element_type=F32, precision=lax.Precision.HIGHEST) + carry
        cum_ref[0, rows, :] = cum
        cumt_ref[0, c] = cum.T[:cumt_ref.shape[2], :]
        carry = cum[blk - 1:blk, :]


def fox_gate(zf, row_off, bsz, t, b_f, pre):
    nheads = b_f.shape[0]
    npre = 0 if pre is None else pre.shape[1]
    total = -(-(npre + t) // LANES) * LANES
    if pre is None:
        pre = jnp.zeros((bsz, 8, LANES), F32)
    pp = pre.shape[1]
    hrows = -(-nheads // 8) * 8
    zf3 = zf[row_off:row_off + bsz * t].reshape(bsz, t, LANES)
    bfp = jnp.zeros((1, LANES), F32).at[0, :nheads].set(b_f.astype(F32))
    return pl.pallas_call(
        functools.partial(_fox_gate_kernel, npre=npre, t=t, blk=LANES),
        grid=(bsz,),
        in_specs=[pl.BlockSpec((1, pp, LANES), lambda b: (b, 0, 0)),
                  pl.BlockSpec((1, t, LANES), lambda b: (b, 0, 0)),
                  pl.BlockSpec((1, LANES), lambda b: (0, 0))],
        out_specs=(pl.BlockSpec((1, t, LANES), lambda b: (b, 0, 0)),
                   pl.BlockSpec((1, total, LANES), lambda b: (b, 0, 0)),
                   pl.BlockSpec((1, total // LANES, hrows, LANES), lambda b: (b, 0, 0, 0))),
        out_shape=(jax.ShapeDtypeStruct((bsz, t, LANES), F32),
                   jax.ShapeDtypeStruct((bsz, total, LANES), F32),
                   jax.ShapeDtypeStruct((bsz, total // LANES, hrows, LANES), F32)),
        scratch_shapes=[pltpu.VMEM((total, LANES), F32)],
        compiler_params=_cparams(("parallel",), 12 * total * LANES * 4),
        name="fox_gate",
    )(pre, zf3, bfp)


FOX_HEADS_PER_STEP = 4


def _fox_attn_kernel(q_ref, k_ref, v_ref, cq_ref, ck_ref, o_ref, m_s, l_s, acc_s,
                     *, tq, tk, qoff, scale, hps):
    hg = pl.program_id(1)
    i = pl.program_id(2)
    hd = acc_s.shape[2]
    lane = lax.broadcasted_iota(jnp.int32, cq_ref.shape[1:], 1)
    cq = cq_ref[0]
    fq = [jnp.sum(jnp.where(lane == hg * hps + hh, cq, 0.0), axis=1, keepdims=True) for hh in range(hps)]
    m_s[...] = jnp.full(m_s.shape, NEG_INF, F32)
    l_s[...] = jnp.zeros(l_s.shape, F32)
    acc_s[...] = jnp.zeros(acc_s.shape, F32)

    def tile(j, masked):
        ks = pl.multiple_of(j * tk, tk)
        scores = []
        for hh in range(hps):
            cols = slice(hh * hd, (hh + 1) * hd)
            k = k_ref[pl.ds(ks, tk), cols]
            fk = jnp.concatenate([ck_ref[0, j * (tk // LANES) + c, pl.ds(hg * hps + hh, 1), :]
                                  for c in range(tk // LANES)], axis=1)
            s = lax.dot_general(q_ref[:, cols], k, (((1,), (1,)), ((), ())), preferred_element_type=F32) * scale
            scores.append(s + fq[hh] - fk)
        for hh, s in enumerate(scores):
            cols = slice(hh * hd, (hh + 1) * hd)
            if masked:
                qpos = qoff + i * tq + lax.broadcasted_iota(jnp.int32, (tq, tk), 0)
                kpos = ks + lax.broadcasted_iota(jnp.int32, (tq, tk), 1)
                s = jnp.where(kpos <= qpos, s, NEG_INF)
            m_prev = m_s[hh]
            m_new = jnp.maximum(m_prev, jnp.max(s, axis=1, keepdims=True))
            alpha = jnp.exp(m_prev - m_new)
            p = jnp.exp(s - m_new)
            l_s[hh] = alpha * l_s[hh] + jnp.sum(p, axis=1, keepdims=True)
            v = v_ref[pl.ds(ks, tk), cols]
            acc_s[hh] = alpha * acc_s[hh] + jnp.dot(p.astype(BF16), v, preferred_element_type=F32)
            m_s[hh] = m_new

    def full_tile(j, c):
        tile(j, False)
        return c

    n_full = (qoff + i * tq) // tk
    lax.fori_loop(0, n_full, full_tile, 0)
    tile(n_full, True)
    for hh in range(hps):
        o_ref[:, hh * hd:(hh + 1) * hd] = (acc_s[hh] / l_s[hh]).astype(o_ref.dtype)


def fox_attention(q_all, row_off, bsz, t, k, v, tkeys, cum, cumt, qoff, scale):
    hd = LANES
    nheads = q_all.shape[1] // hd
    hps = _pick(nheads, (FOX_HEADS_PER_STEP, 2, 1))
    tq = _pick(t, (256, 128, 64, 32, 16, 8))
    tk = tkeys if tq * tkeys * 4 <= (256 << 10) else _pick(tkeys, (256, 128))
    assert tk % tq == 0 and qoff % tq == 0 and qoff + t <= tkeys
    nq = t // tq
    rb0 = row_off // tq
    cq0 = qoff // tq
    vmem = (8 * tkeys * hps * hd * 2 + 2 * cumt.shape[2] * tkeys * 4 + 4 * hps * tq * tk * 4
            + 8 * hps * tq * hd * 4 + (2 << 20))
    return pl.pallas_call(
        functools.partial(_fox_attn_kernel, tq=tq, tk=tk, qoff=qoff, scale=scale, hps=hps),
        grid=(bsz, nheads // hps, nq),
        in_specs=[pl.BlockSpec((tq, hps * hd), lambda b, h, i: (rb0 + b * nq + i, h)),
                  pl.BlockSpec((tkeys, hps * hd), lambda b, h, i: (b, h)),
                  pl.BlockSpec((tkeys, hps * hd), lambda b, h, i: (b, h)),
                  pl.BlockSpec((1, tq, LANES), lambda b, h, i: (b, cq0 + i, 0)),
                  pl.BlockSpec((1,) + cumt.shape[1:], lambda b, h, i: (b, 0, 0, 0))],
        out_specs=pl.BlockSpec((tq, hps * hd), lambda b, h, i: (b * nq + i, h)),
        out_shape=jax.ShapeDtypeStruct((bsz * t, nheads * hd), BF16),
        scratch_shapes=[pltpu.VMEM((hps, tq, 1), F32), pltpu.VMEM((hps, tq, 1), F32), pltpu.VMEM((hps, tq, hd), F32)],
        compiler_params=_cparams(("parallel", "parallel", "arbitrary"), vmem),
        name="fox_attention",
    )(q_all, k, v, cum, cumt)


def _rwkv_prep_kernel(z_ref, zp_ref, sh_ref, mu_ref, w0_ref, a0_ref, w2_ref, a2_ref, g2_ref,
                      r_ref, k_ref, v_ref, lw_ref, a_ref, g_ref, *, w):
    i = pl.program_id(1)
    z = z_ref[...]
    prev_row = jnp.where(i == 0, sh_ref[0], zp_ref[7:8, :])
    row = lax.broadcasted_iota(jnp.int32, z.shape, 0)
    z_prev = jnp.where(row == 0, prev_row, pltpu.roll(z, 1, 0))
    zm = z + (z_prev - z) * mu_ref[...]
    slab = zm[:, 3 * w:]
    lora_w = jnp.dot(jnp.tanh(slab).astype(BF16), w2_ref[...], preferred_element_type=F32)
    x = -(w0_ref[...] + lora_w)
    softplus = jnp.maximum(x, 0.0) + jnp.log1p(jnp.exp(-jnp.abs(x)))
    lw = -jnp.exp(-softplus - 0.5)
    a = jax.nn.sigmoid(a0_ref[...] + jnp.dot(slab.astype(BF16), a2_ref[...], preferred_element_type=F32))
    g = jnp.dot(jax.nn.sigmoid(slab).astype(BF16), g2_ref[...], preferred_element_type=F32)
    nh, hd = r_ref.shape[1], r_ref.shape[3]
    for ref, val in ((r_ref, zm[:, 0:w]), (k_ref, zm[:, w:2 * w]), (v_ref, zm[:, 2 * w:3 * w]),
                     (lw_ref, lw), (a_ref, a), (g_ref, g)):
        for hh in range(nh):
            ref[0, hh] = val[:, hh * hd:(hh + 1) * hd]


def rwkv_prep(zr, row_off, bsz, t, shift_prev, prm, nh):
    wz = zr.shape[1]
    w = prm["w0"].shape[1]
    hd = w // nh
    tt = _pick(t, (128, 64, 32, 16, 8))
    nt = t // tt
    rb0 = row_off // tt
    full = lambda shape: pl.BlockSpec(shape, lambda b, i: (0,) * len(shape))
    ospec = pl.BlockSpec((1, nh, tt, hd), lambda b, i: (b, 0, i, 0))
    oshape = jax.ShapeDtypeStruct((bsz, nh, t, hd), F32)
    ls = wz - 3 * w
    vmem = 6 * tt * wz * 4 + 12 * tt * nh * LANES * 4 + 6 * ls * w * 2 + 8 * tt * w * 4
    return pl.pallas_call(
        functools.partial(_rwkv_prep_kernel, w=w),
        grid=(bsz, nt),
        in_specs=[pl.BlockSpec((tt, wz), lambda b, i: (rb0 + b * nt + i, 0)),
                  pl.BlockSpec((8, wz), lambda b, i: (jnp.maximum((row_off + (b * nt + i) * tt) // 8 - 1, 0), 0)),
                  pl.BlockSpec((1, 1, wz), lambda b, i: (b, 0, 0)),
                  full((1, wz)), full((1, w)), full((1, w)), full((ls, w)), full((ls, w)), full((ls, w))],
        out_specs=(ospec,) * 6,
        out_shape=(oshape,) * 6,
        compiler_params=_cparams(("parallel", "arbitrary"), vmem),
        name="rwkv_prep",
    )(zr, zr, shift_prev.reshape(bsz, 1, wz), prm["mu"], prm["w0"], prm["a0"], prm["w2"], prm["a2"], prm["g2"])


def _bdot(a, b, dims):
    return lax.dot_general(a.astype(BF16), b.astype(BF16), dims, preferred_element_type=F32)


def _cumsum_rows(tri, x):
    hi = x.astype(BF16)
    r1 = x - hi.astype(F32)
    mid = r1.astype(BF16)
    lo = (r1 - mid.astype(F32)).astype(BF16)
    t = tri.astype(BF16)
    dot = lambda p: lax.dot_general(t, p, _NN, preferred_element_type=F32)
    return dot(hi) + dot(mid) + dot(lo)


_NT = (((2,), (2,)), ((0,), (0,)))
_NN = (((2,), (1,)), ((0,), (0,)))
_TN = (((1,), (1,)), ((0,), (0,)))


def _rwkv_scan_kernel(r_ref, k_ref, v_ref, lw_ref, a_ref, g_ref, s0_ref, kk_ref, ka_ref, rk_ref,
                      lnw_ref, lnb_ref, y_ref, sout_ref, st_s, *, chunk):
    c = pl.program_id(2)

    @pl.when(c == 0)
    def _():
        st_s[...] = s0_ref[0]

    r = r_ref[0]
    k = k_ref[0]
    v = v_ref[0]
    lw = lw_ref[0]
    a = a_ref[0]
    hb = r.shape[0]
    s0 = st_s[...]

    kk = k * kk_ref[...]
    kk = kk / jnp.maximum(jnp.sqrt(jnp.sum(kk * kk, axis=-1, keepdims=True)), 1e-12)
    kmod = k * (1.0 + (a - 1.0) * ka_ref[...])

    li = lax.broadcasted_iota(jnp.int32, (chunk, chunk), 0)
    mi = lax.broadcasted_iota(jnp.int32, (chunk, chunk), 1)
    incl = (mi <= li).astype(F32)
    strict = (mi < li).astype(F32)
    cs = _cumsum_rows(jnp.broadcast_to(incl, (hb, chunk, chunk)), lw)
    dec_in = jnp.exp(cs)
    dec_ex = jnp.exp(cs - lw)
    inv = jnp.exp(-cs)
    p_rows = jnp.concatenate([-kk * dec_ex, r * dec_in], axis=1)
    q_rows = jnp.concatenate([kk * a * inv, kmod * inv], axis=1)
    mm = _bdot(p_rows, q_rows, _NT)
    a_ab = mm[:, :chunk, :chunk] * strict
    a_ak = mm[:, :chunk, chunk:] * strict
    r_b = mm[:, chunk:, :chunk] * incl
    r_k = mm[:, chunk:, chunk:] * incl
    ps = _bdot(p_rows, s0, _NT)
    x = ps[:, :chunk] + _bdot(a_ak, v, _NN)
    pw = a_ab
    n_iter = chunk.bit_length() - 1
    for it in range(n_iter):
        x = x + _bdot(pw, x, _NN)
        if it + 1 < n_iter:
            pw = _bdot(pw, pw, _NN)
    y = ps[:, chunk:] + _bdot(r_b, x, _NN) + _bdot(r_k, v, _NN)
    uv = jnp.concatenate([x, v], axis=1)
    s_new = (s0 + _bdot(uv, q_rows, _TN)) * dec_in[:, chunk - 1:chunk, :]
    st_s[...] = s_new
    sout_ref[0] = s_new

    mean = jnp.mean(y, axis=-1, keepdims=True)
    var = jnp.mean(jnp.square(y - mean), axis=-1, keepdims=True)
    yn = (y - mean) * lax.rsqrt(var + RWKV_LN_EPS) * lnw_ref[...] + lnb_ref[...]
    bonus = jnp.sum(r * kmod * rk_ref[...], axis=-1, keepdims=True) * v
    out = (yn + bonus) * g_ref[0]
    hd = out.shape[2]
    for hh in range(hb):
        y_ref[:, hh * hd:(hh + 1) * hd] = out[hh].astype(y_ref.dtype)


def rwkv_scan(r, k, v, lw, a, g, s0, prm):
    bsz, nh, t, hd = r.shape
    chunk = _pick(t, (64, 32, 16, 8))
    hb = _pick(nh, (32, 16, 8, 4, 2, 1))
    nc = t // chunk
    xspec = pl.BlockSpec((1, hb, chunk, hd), lambda b, h, c: (b, h, c, 0))
    pspec = pl.BlockSpec((hb, 1, hd), lambda b, h, c: (h, 0, 0))
    sspec = pl.BlockSpec((1, hb, hd, hd), lambda b, h, c: (b, h, 0, 0))
    vmem = 16 * hb * chunk * LANES * 4 + 40 * hb * 2 * chunk * LANES * 4 + 6 * hb * hd * LANES * 4
    y, s = pl.pallas_call(
        functools.partial(_rwkv_scan_kernel, chunk=chunk),
        grid=(bsz, nh // hb, nc),
        in_specs=[xspec] * 6 + [sspec] + [pspec] * 5,
        out_specs=(pl.BlockSpec((chunk, hb * hd), lambda b, h, c: (b * nc + c, h)), sspec),
        out_shape=(jax.ShapeDtypeStruct((bsz * t, nh * hd), BF16), jax.ShapeDtypeStruct((bsz, nh, hd, hd), F32)),
        scratch_shapes=[pltpu.VMEM((hb, hd, hd), F32)],
        compiler_params=_cparams(("parallel", "parallel", "arbitrary"), vmem),
        name="rwkv_scan",
    )(r, k, v, lw, a, g, s0, prm["k_k"], prm["k_a"], prm["r_k"], prm["ln_w"], prm["ln_b"])
    return y, s


def _router_kernel(x_ref, w_ref, comb_ref, *, n_experts):
    logits = jnp.dot(x_ref[...].astype(BF16), w_ref[...], preferred_element_type=F32)
    lane = lax.broadcasted_iota(jnp.int32, logits.shape, 1)
    big = jnp.int32(LANES)
    logits = jnp.where(lane < n_experts, logits, -jnp.inf)
    t1 = jnp.max(logits, axis=1, keepdims=True)
    i1 = jnp.min(jnp.where(logits == t1, lane, big), axis=1, keepdims=True)
    rest = jnp.where(lane == i1, -jnp.inf, logits)
    t2 = jnp.max(rest, axis=1, keepdims=True)
    i2 = jnp.min(jnp.where(rest == t2, lane, big), axis=1, keepdims=True)
    e2 = jnp.exp(t2 - t1)
    den = 1.0 + e2
    comb_ref[...] = jnp.where(lane == 0, 1.0 / den, jnp.where(lane == 1, e2 / den, jnp.where(
        lane == 2, i1.astype(F32), jnp.where(lane == 3, i2.astype(F32), 0.0))))


def moe_router(x, w_router):
    m, d = x.shape
    ne = w_router.shape[1]
    wp = jnp.zeros((d, LANES), BF16).at[:, :ne].set(w_router.astype(BF16))
    tm = _pick(m, (640, 512, 320, 256, 128, 64, 32, 16, 8))
    return pl.pallas_call(
        functools.partial(_router_kernel, n_experts=ne),
        grid=(m // tm,),
        in_specs=[pl.BlockSpec((tm, d), lambda i: (i, 0)), pl.BlockSpec((d, LANES), lambda i: (0, 0))],
        out_specs=pl.BlockSpec((tm, LANES), lambda i: (i, 0)),
        out_shape=jax.ShapeDtypeStruct((m, LANES), F32),
        compiler_params=_cparams(("parallel",), 3 * tm * d * 4 + 2 * d * LANES * 2 + 8 * tm * LANES * 4),
        name="moe_router",
    )(x, wp)


MOE_TILE = 256


def _moe_plan(rout, n_exp, tile):
    n = rout.shape[0]
    pair_e = rout[:, 2:4].astype(jnp.int32).reshape(-1)
    onehot = (pair_e[:, None] == jnp.arange(n_exp, dtype=jnp.int32)[None, :]).astype(jnp.int32)
    csum = jnp.cumsum(onehot, axis=0)
    rank = jnp.take_along_axis(csum - onehot, pair_e[:, None], axis=1)[:, 0]
    gsz = (csum[-1] + tile - 1) // tile * tile
    gend = jnp.cumsum(gsz)
    slot = ((gend - gsz)[pair_e] + rank).astype(jnp.int32)
    n_tiles = -(-2 * n // tile) + n_exp
    tok = jnp.zeros((n_tiles * tile,), jnp.int32).at[slot].set(jnp.arange(2 * n, dtype=jnp.int32) // 2)
    tile_start = jnp.arange(n_tiles, dtype=jnp.int32) * tile
    te = jnp.minimum(jnp.searchsorted(gend, tile_start, side="right"), n_exp - 1).astype(jnp.int32)
    used = (gend[-1] // tile).astype(jnp.int32).reshape(1)
    return slot, tok, te, used, n_tiles


def _row_copy(src_hbm, row, dst, r, sem):
    return pltpu.make_async_copy(src_hbm.at[pl.ds(row, 1)], dst.at[pl.ds(r, 1)], sem)


def _moe_gather_kernel(tok_ref, used_ref, x_hbm, o_ref, buf, sem, *, tile):
    i = pl.program_id(0)

    @pl.when(i < used_ref[0])
    def _():
        base = i * tile

        def issue(r, c):
            _row_copy(x_hbm, tok_ref[base + r], buf, r, sem).start()
            return c

        def wait(r, c):
            _row_copy(x_hbm, 0, buf, r, sem).wait()
            return c

        lax.fori_loop(0, tile, issue, 0)
        lax.fori_loop(0, tile, wait, 0)
        o_ref[...] = buf[...].astype(o_ref.dtype)

    @pl.when(i >= used_ref[0])
    def _():
        o_ref[...] = jnp.zeros(o_ref.shape, o_ref.dtype)


def moe_gather(x, tok, used, n_tiles, tile):
    d = x.shape[1]
    return pl.pallas_call(
        functools.partial(_moe_gather_kernel, tile=tile),
        grid_spec=pltpu.PrefetchScalarGridSpec(
            num_scalar_prefetch=2, grid=(n_tiles,),
            in_specs=[pl.BlockSpec(memory_space=pl.ANY)],
            out_specs=pl.BlockSpec((tile, d), lambda i, tok_r, used_r: (i, 0)),
            scratch_shapes=[pltpu.VMEM((tile, d), x.dtype), pltpu.SemaphoreType.DMA(())]),
        out_shape=jax.ShapeDtypeStruct((n_tiles * tile, d), BF16),
        compiler_params=_cparams(("arbitrary",), 4 * tile * d * 4),
        name="moe_gather",
    )(tok, used, x)


def _moe_up_kernel(te_ref, used_ref, x_ref, wg_ref, wu_ref, o_ref, wg_s, wu_s):
    i = pl.program_id(1)

    @pl.when((i == 0) | (te_ref[i] != te_ref[jnp.maximum(i - 1, 0)]))
    def _():
        wg_s[...] = wg_ref[...].astype(BF16)
        wu_s[...] = wu_ref[...].astype(BF16)

    @pl.when(i < used_ref[0])
    def _():
        x = x_ref[...]
        g = jnp.dot(x, wg_s[...], preferred_element_type=F32)
        u = jnp.dot(x, wu_s[...], preferred_element_type=F32)
        o_ref[...] = (g * jax.nn.sigmoid(g) * u).astype(o_ref.dtype)

    @pl.when(i >= used_ref[0])
    def _():
        o_ref[...] = jnp.zeros(o_ref.shape, o_ref.dtype)


def moe_up(xs, wg, wu, te, used, tile):
    p_rows, d = xs.shape
    f = wg.shape[2]
    tn = _pick(f, (512, 256, 128))
    wspec = pl.BlockSpec((None, d, tn), lambda j, i, te_r, used_r: (te_r[i], 0, j))
    vmem = 2 * tile * d * 2 + 4 * d * tn * 4 + 2 * d * tn * 2 + 6 * tile * tn * 4
    return pl.pallas_call(
        _moe_up_kernel,
        grid_spec=pltpu.PrefetchScalarGridSpec(
            num_scalar_prefetch=2, grid=(f // tn, p_rows // tile),
            in_specs=[pl.BlockSpec((tile, d), lambda j, i, te_r, used_r: (i, 0)), wspec, wspec],
            out_specs=pl.BlockSpec((tile, tn), lambda j, i, te_r, used_r: (i, j)),
            scratch_shapes=[pltpu.VMEM((d, tn), BF16), pltpu.VMEM((d, tn), BF16)]),
        out_shape=jax.ShapeDtypeStruct((p_rows, f), BF16),
        compiler_params=_cparams(("parallel", "arbitrary"), vmem),
        name="moe_up",
    )(te, used, xs, wg, wu)


def _moe_down_kernel(te_ref, used_ref, x_ref, w_ref, o_ref):
    i = pl.program_id(1)

    @pl.when(i < used_ref[0])
    def _():
        o_ref[...] = jnp.dot(x_ref[...], w_ref[...], preferred_element_type=F32)

    @pl.when(i >= used_ref[0])
    def _():
        o_ref[...] = jnp.zeros(o_ref.shape, o_ref.dtype)


def moe_down(act, wd, te, used, tile):
    p_rows, f = act.shape
    d = wd.shape[2]
    tn = _pick(d, (512, 256, 128))
    vmem = 2 * tile * f * 2 + 2 * f * tn * 2 + 4 * tile * tn * 4
    return pl.pallas_call(
        _moe_down_kernel,
        grid_spec=pltpu.PrefetchScalarGridSpec(
            num_scalar_prefetch=2, grid=(d // tn, p_rows // tile),
            in_specs=[pl.BlockSpec((tile, f), lambda j, i, te_r, used_r: (i, 0)),
                      pl.BlockSpec((None, f, tn), lambda j, i, te_r, used_r: (te_r[i], 0, j))],
            out_specs=pl.BlockSpec((tile, tn), lambda j, i, te_r, used_r: (i, j))),
        out_shape=jax.ShapeDtypeStruct((p_rows, d), F32),
        compiler_params=_cparams(("parallel", "arbitrary"), vmem),
        name="moe_down",
    )(te, used, act, wd)


def _moe_combine_kernel(slot_ref, h_ref, g_ref, ys_hbm, fn_ref, o_ref, buf, sem, *, tc):
    base = pl.program_id(0) * tc

    def issue(r, c):
        p = 2 * (base + r)
        _row_copy(ys_hbm, slot_ref[p], buf.at[0], r, sem).start()
        _row_copy(ys_hbm, slot_ref[p + 1], buf.at[1], r, sem).start()
        return c

    def wait(r, c):
        _row_copy(ys_hbm, 0, buf.at[0], r, sem).wait()
        _row_copy(ys_hbm, 0, buf.at[1], r, sem).wait()
        return c

    lax.fori_loop(0, tc, issue, 0)
    lax.fori_loop(0, tc, wait, 0)
    g = g_ref[...]
    x = h_ref[...] + (g[:, 0:1] * buf[0] + g[:, 1:2] * buf[1])
    y = x * lax.rsqrt(jnp.mean(x * x, axis=-1, keepdims=True) + NORM_EPS)
    o_ref[...] = y * fn_ref[...]


def moe_combine_norm(h, rout, ys, slot, final_norm):
    n, d = h.shape
    tc = _pick(n, (256, 128, 64, 32, 16, 8))
    return pl.pallas_call(
        functools.partial(_moe_combine_kernel, tc=tc),
        grid_spec=pltpu.PrefetchScalarGridSpec(
            num_scalar_prefetch=1, grid=(n // tc,),
            in_specs=[pl.BlockSpec((tc, d), lambda i, s: (i, 0)),
                      pl.BlockSpec((tc, LANES), lambda i, s: (i, 0)),
                      pl.BlockSpec(memory_space=pl.ANY),
                      pl.BlockSpec((1, d), lambda i, s: (0, 0))],
            out_specs=pl.BlockSpec((tc, d), lambda i, s: (i, 0)),
            scratch_shapes=[pltpu.VMEM((2, tc, d), F32), pltpu.SemaphoreType.DMA(())]),
        out_shape=jax.ShapeDtypeStruct((n, d), F32),
        compiler_params=_cparams(("arbitrary",), 8 * tc * d * 4),
        name="moe_combine_norm",
    )(slot, h, rout, ys, final_norm.astype(F32).reshape(1, d))


def _rope_tables(pos, rope, width, lead):
    inv = ROPE_THETA ** (-jnp.arange(0, rope, 2, dtype=F32) / rope)
    ang = pos.astype(F32)[:, None] * inv[None, :]
    cos, sin = jnp.cos(ang), jnp.sin(ang)
    n = pos.shape[0]
    ctab = jnp.concatenate([jnp.ones((n, lead), F32), cos, cos, jnp.zeros((n, width - lead - rope), F32)], axis=1)
    stab = jnp.concatenate([jnp.zeros((n, lead), F32), -sin, sin, jnp.zeros((n, width - lead - rope), F32)], axis=1)
    return ctab, stab


def _swap_halves(w):
    half = w.shape[-1] // 2
    return jnp.concatenate([w[..., half:], w[..., :half]], axis=-1)


def kernel(x_prompt, x_sample, cache_mla_ckv, cache_mla_kpe, state_s5_re, state_s5_im, state_rwkv_wkv, state_rwkv_shift, cache_fox_k, cache_fox_v, cache_fox_logf, ln0_mix, w_in0, s5_a_re, s5_a_im, s5_log_dt, s5_b_re, s5_b_im, s5_c_re, s5_c_im, s5_d, s5_w_glu, s5_b_glu, mla_q_norm, mla_w_q_up, mla_kv_norm, mla_w_uk, mla_w_uv, w_out0, ln0_ffn, ffn_w_gate, ffn_w_up, ffn_w_down, ln1_mix, w_in1, rwkv_mu, rwkv_w0, rwkv_w2, rwkv_a0, rwkv_a2, rwkv_g2, rwkv_k_k, rwkv_k_a, rwkv_r_k, rwkv_ln_w, rwkv_ln_b, fox_b_f, w_out1, ln1_ffn, moe_w_router, moe_w_gate, moe_w_up, moe_w_down, final_norm):
    bp, tp, d = x_prompt.shape
    bs, ts, _ = x_sample.shape
    past = cache_mla_ckv.shape[1]
    n_p, n_s = bp * tp, bs * ts
    ntok = n_p + n_s
    streams = ((0, bp, tp), (n_p, bs, ts))

    h = jnp.concatenate([x_prompt.reshape(n_p, d), x_sample.reshape(n_s, d)], axis=0)

    s5_w = s5_d.shape[0]
    q_rank = mla_q_norm.shape[0]
    kv_rank = mla_kv_norm.shape[0]
    n_mh, qk = mla_w_q_up.shape[1], mla_w_q_up.shape[2]
    nope = mla_w_uk.shape[2]
    rope = qk - nope
    vdim = mla_w_uv.shape[2]
    mla_scale = float(qk) ** -0.5
    qw = 2 * LANES
    kvw = kv_rank + LANES

    (xn,) = rmsnorm(h, ln0_mix, (BF16,))
    w_in0b = w_in0.astype(BF16)
    (z_uq,) = matmul(xn, w_in0b[:, :s5_w + q_rank])
    off_kv = s5_w + q_rank
    w_kpe = w_in0b[:, off_kv + kv_rank:]
    zpad = jnp.zeros((d, kvw - kv_rank - rope), BF16)
    w_kv1 = jnp.concatenate([w_in0b[:, off_kv:off_kv + kv_rank], w_kpe, zpad], axis=1)
    w_kv2 = jnp.concatenate([jnp.zeros((d, kv_rank), BF16), _swap_halves(w_kpe), zpad], axis=1)
    pos_p = jnp.arange(tp)
    pos_s = past + jnp.arange(ts)

    def token_tables(width, lead):
        cp, sp = _rope_tables(pos_p, rope, width, lead)
        cs, ss = _rope_tables(pos_s, rope, width, lead)
        return (jnp.concatenate([jnp.tile(cp, (bp, 1)), jnp.tile(cs, (bs, 1))], axis=0),
                jnp.concatenate([jnp.tile(sp, (bp, 1)), jnp.tile(ss, (bs, 1))], axis=0))

    ckv_c, ckv_s = token_tables(kvw, kv_rank)
    z_kv = rope_matmul(xn, w_kv1, w_kv2, ckv_c, ckv_s, F32, tn=kvw)
    ckv_f, ckv_b = rmsnorm(z_kv, mla_kv_norm, (F32, BF16), col_block=0, width=kv_rank)
    kpe_f = z_kv[:, kv_rank:kv_rank + rope]
    kpe_b = z_kv[:, kv_rank:].astype(BF16)

    (cqn,) = rmsnorm(z_uq, mla_q_norm, (BF16,), col_block=s5_w // q_rank, width=q_rank)
    wq = mla_w_q_up.astype(BF16)
    zq = jnp.zeros((q_rank, n_mh, qw - qk), BF16)
    wq1 = jnp.concatenate([wq, zq], axis=2).reshape(q_rank, n_mh * qw)
    wq2 = jnp.concatenate([jnp.zeros((q_rank, n_mh, nope), BF16), _swap_halves(wq[:, :, nope:]), zq],
                          axis=2).reshape(q_rank, n_mh * qw)
    q_c, q_s = token_tables(qw, nope)
    q_all = rope_matmul(cqn, wq1, wq2, q_c, q_s, BF16, tn=qw)

    s5p = s5_params(s5_a_re, s5_a_im, s5_log_dt, s5_b_re, s5_b_im, s5_c_re, s5_c_im, s5_d, s5_w_glu, s5_b_glu)
    g5, p5 = s5_a_re.shape
    wuk = jnp.transpose(mla_w_uk, (1, 2, 0)).astype(BF16)
    wuv = jnp.transpose(mla_w_uv, (1, 0, 2)).astype(BF16)

    y_s5, s5_re, s5_im, y_mla = [], [], [], []
    for si, (off, bsz, t) in enumerate(streams):
        if si == 0:
            h0r = jnp.zeros((bsz, g5 * p5), F32)
            h0i = h0r
            ckv_k = ckv_b[off:off + bsz * t].reshape(bsz, t, kv_rank)
            kpe_k = kpe_b[off:off + bsz * t].reshape(bsz, t, LANES)
            klen, causal = t, True
        else:
            h0r, h0i = state_s5_re.astype(F32), state_s5_im.astype(F32)
            klen, causal = past + t, False
            padk = -(-klen // LANES) * LANES - klen
            ckv_k = jnp.concatenate([cache_mla_ckv.astype(BF16), ckv_b[off:off + bsz * t].reshape(bsz, t, kv_rank),
                                     jnp.zeros((bsz, padk, kv_rank), BF16)], axis=1)
            kpe_cache = jnp.concatenate([cache_mla_kpe.astype(BF16),
                                         jnp.zeros((bsz, past, LANES - rope), BF16)], axis=2)
            kpe_k = jnp.concatenate([kpe_cache, kpe_b[off:off + bsz * t].reshape(bsz, t, LANES),
                                     jnp.zeros((bsz, padk, LANES), BF16)], axis=1)
        ys, hr, hi = s5_mixer(z_uq, off, bsz, t, h0r, h0i, s5p)
        y_s5.append(ys)
        s5_re.append(hr.reshape(bsz, g5, p5))
        s5_im.append(hi.reshape(bsz, g5, p5))
        y_mla.append(mla_attention(q_all, off, bsz, t, ckv_k, kpe_k, wuk, wuv, klen, causal, mla_scale))

    mix0 = jnp.concatenate([jnp.concatenate(y_s5, axis=0), jnp.concatenate(y_mla, axis=0)], axis=1)
    (h,) = matmul(mix0, w_out0.astype(BF16), res=h)
    (hn,) = rmsnorm(h, ln0_ffn, (BF16,))
    act = swiglu_up(hn, ffn_w_gate, ffn_w_up)
    (h,) = matmul(act, ffn_w_down.astype(BF16), res=h, tm=_pick(ntok, (640, 512, 256, 128, 64, 32, 16, 8)),
                  tn=256, tk=ffn_w_down.shape[0])

    rw = rwkv_w0.shape[0]
    nh_r, hd_r = rwkv_k_k.shape
    shift_w = rwkv_mu.shape[0]
    lora_w = shift_w - 3 * rw
    slab = -(-lora_w // LANES) * LANES
    wz = 3 * rw + slab
    nh_f = fox_b_f.shape[0]
    fw = (w_in1.shape[1] - shift_w - nh_f) // 3
    fox_scale = float(fw // nh_f) ** -0.5
    d_lw, d_la = rwkv_w2.shape[0], rwkv_a2.shape[0]

    (xn,) = rmsnorm(h, ln1_mix, (BF16,))
    w_in1b = w_in1.astype(BF16)
    (zr,) = matmul(xn, jnp.pad(w_in1b[:, :shift_w], ((0, 0), (0, wz - shift_w))))
    (fq,) = matmul(xn, w_in1b[:, shift_w:shift_w + fw], (BF16,))
    fk_f, fk_b = matmul(xn, w_in1b[:, shift_w + fw:shift_w + 2 * fw], (F32, BF16))
    fv_f, fv_b = matmul(xn, w_in1b[:, shift_w + 2 * fw:shift_w + 3 * fw], (F32, BF16))
    (zf,) = matmul(xn, jnp.pad(w_in1b[:, shift_w + 3 * fw:], ((0, 0), (0, LANES - nh_f))))

    padrow = lambda wgt, lo: jnp.zeros((slab, rw), BF16).at[lo:lo + wgt.shape[0]].set(wgt.astype(BF16))
    row2 = lambda x_, n_: x_.astype(F32).reshape(1, n_)
    head3 = lambda x_: x_.astype(F32).reshape(nh_r, 1, hd_r)
    rprm = dict(mu=jnp.pad(row2(rwkv_mu, shift_w), ((0, 0), (0, wz - shift_w))), w0=row2(rwkv_w0, rw),
                a0=row2(rwkv_a0, rw), w2=padrow(rwkv_w2, 0), a2=padrow(rwkv_a2, d_lw),
                g2=padrow(rwkv_g2, d_lw + d_la), k_k=head3(rwkv_k_k), k_a=head3(rwkv_k_a), r_k=head3(rwkv_r_k),
                ln_w=head3(rwkv_ln_w), ln_b=head3(rwkv_ln_b))

    y_r, wkv, shift_new, y_f, logf_out = [], [], [], [], []
    for si, (off, bsz, t) in enumerate(streams):
        if si == 0:
            shift_prev = jnp.zeros((bsz, wz), F32)
            s0 = jnp.zeros((bsz, nh_r, hd_r, hd_r), F32)
            pre = None
            assert off == 0
            k_all, v_all, tkeys = fk_b, fv_b, t
            qoff = 0
        else:
            shift_prev = jnp.pad(state_rwkv_shift.astype(F32), ((0, 0), (0, wz - shift_w)))
            s0 = state_rwkv_wkv.astype(F32)
            pre = jnp.pad(cache_fox_logf.astype(F32), ((0, 0), (0, 0), (0, LANES - nh_f)))
            qoff = cache_fox_k.shape[1]
            tkeys = -(-(qoff + t) // LANES) * LANES
            cat = lambda cache, new: jnp.concatenate(
                [cache.reshape(bsz, qoff, fw).astype(BF16), new[off:off + bsz * t].reshape(bsz, t, fw),
                 jnp.zeros((bsz, tkeys - qoff - t, fw), BF16)], axis=1).reshape(bsz * tkeys, fw)
            k_all, v_all = cat(cache_fox_k, fk_b), cat(cache_fox_v, fv_b)
        parts = rwkv_prep(zr, off, bsz, t, shift_prev, rprm, nh_r)
        yr, s_fin = rwkv_scan(*parts, s0, rprm)
        y_r.append(yr)
        wkv.append(s_fin)
        shift_new.append(zr[off:off + bsz * t].reshape(bsz, t, wz)[:, -1, :shift_w])
        logf, cum, cumt = fox_gate(zf, off, bsz, t, fox_b_f, pre)
        logf_out.append(logf[:, :, :nh_f])
        y_f.append(fox_attention(fq, off, bsz, t, k_all, v_all, tkeys, cum, cumt, qoff, fox_scale))

    mix1 = jnp.concatenate([jnp.concatenate(y_r, axis=0), jnp.concatenate(y_f, axis=0)], axis=1)
    (h,) = matmul(mix1, w_out1.astype(BF16), res=h)
    (hn,) = rmsnorm(h, ln1_ffn, (F32,))
    rout = moe_router(hn, moe_w_router)
    n_exp = moe_w_gate.shape[0]
    slot, tok, te, used, n_tiles = _moe_plan(rout, n_exp, MOE_TILE)
    xs = moe_gather(hn, tok, used, n_tiles, MOE_TILE)
    act = moe_up(xs, moe_w_gate, moe_w_up, te, used, MOE_TILE)
    ys = moe_down(act, moe_w_down.astype(BF16), te, used, MOE_TILE)
    y_all = moe_combine_norm(h, rout, ys, slot, final_norm)
    nfh = fw // nh_f
    outs = [y_all[:n_p].reshape(bp, tp, d), y_all[n_p:].reshape(bs, ts, d)]
    for si, (off, bsz, t) in enumerate(streams):
        rows = slice(off, off + bsz * t)
        outs += [ckv_f[rows].reshape(bsz, t, kv_rank), kpe_f[rows].reshape(bsz, t, rope), s5_re[si], s5_im[si],
                 wkv[si], shift_new[si], fk_f[rows].reshape(bsz, t, nh_f, nfh), fv_f[rows].reshape(bsz, t, nh_f, nfh),
                 logf_out[si]]
    return tuple(outs)
```

```python
import functools
import math

import jax
import jax.numpy as jnp
from jax import lax
from jax.experimental import pallas as pl
from jax.experimental.pallas import tpu as pltpu

F32 = jnp.float32
BF16 = jnp.bfloat16

V7X_VMEM_BYTES = 64 * 1024 * 1024
VMEM_CAP = V7X_VMEM_BYTES - 8 * 1024 * 1024
LANES = 128

NORM_EPS = 1e-6
NEG_INF = -1e30
CHUNK = 64
ROPE_THETA = 10000.0
RWKV_LN_EPS = 64e-5
S5_BLOCK_GROUPS = 8


def _pick(n, cands):
    for c in cands:
        if c <= n and n % c == 0:
            return c
    return n


def _cparams(sem, vmem_bytes):
    limit = int(min(max(vmem_bytes * 1.25 + (4 << 20), 24 << 20), VMEM_CAP))
    return pltpu.CompilerParams(dimension_semantics=sem, vmem_limit_bytes=limit)


def _rmsnorm_kernel(x_ref, g_ref, *o_refs):
    x = x_ref[...].astype(F32)
    y = x * lax.rsqrt(jnp.mean(x * x, axis=-1, keepdims=True) + NORM_EPS)
    y = y * g_ref[...]
    for o in o_refs:
        o[...] = y.astype(o.dtype)


def rmsnorm(x, g, out_dtypes, col_block=0, width=None):
    m = x.shape[0]
    width = x.shape[1] if width is None else width
    tm = _pick(m, (512, 320, 256, 128, 64, 32, 16, 8))
    outs = tuple(jax.ShapeDtypeStruct((m, width), d) for d in out_dtypes)
    res = pl.pallas_call(
        _rmsnorm_kernel,
        grid=(m // tm,),
        in_specs=[pl.BlockSpec((tm, width), lambda i: (i, col_block)),
                  pl.BlockSpec((1, width), lambda i: (0, 0))],
        out_specs=tuple(pl.BlockSpec((tm, width), lambda i: (i, 0)) for _ in out_dtypes),
        out_shape=outs,
        compiler_params=_cparams(("parallel",), tm * width * 4 * 2 * (1 + len(out_dtypes))),
        name="rmsnorm",
    )(x, g.reshape(1, width).astype(F32))
    return res


def _mm_kernel(*refs, nk, has_res, has_scale, n_out):
    x_ref, w_ref = refs[0], refs[1]
    pos = 2
    res_ref = scale_ref = None
    if has_res:
        res_ref = refs[pos]
        pos += 1
    if has_scale:
        scale_ref = refs[pos]
        pos += 1
    o_refs = refs[pos:pos + n_out]
    acc_ref = refs[pos + n_out] if nk > 1 else None

    part = jnp.dot(x_ref[...].astype(BF16), w_ref[...].astype(BF16), preferred_element_type=F32)

    def finish(acc):
        if has_scale:
            acc = acc * scale_ref[...]
        if has_res:
            acc = res_ref[...] + acc
        for o in o_refs:
            o[...] = acc.astype(o.dtype)

    if nk == 1:
        finish(part)
    else:
        k = pl.program_id(2)

        @pl.when(k == 0)
        def _():
            acc_ref[...] = part

        @pl.when(k > 0)
        def _():
            acc_ref[...] += part

        @pl.when(k == nk - 1)
        def _():
            finish(acc_ref[...])


def matmul(x, w, out_dtypes=(F32,), res=None, scale=None, w_index=None, tm=None, tn=None, tk=None, rows=None):
    row0, m = rows if rows is not None else (0, x.shape[0])
    kdim = x.shape[1]
    n = w.shape[-1]
    tm = tm or _pick(math.gcd(m, row0) if row0 else m, (1280, 1024, 640, 512, 320, 256, 128, 64, 32, 16, 8))
    tn = tn or _pick(n, (512, 384, 256, 128))
    tk = tk or (kdim if kdim <= 4096 else _pick(kdim, (2048, 1792, 1024, 512, 256, 128)))
    nk = kdim // tk
    grid = (m // tm, n // tn, nk)
    rb0 = row0 // tm
    in_specs = [pl.BlockSpec((tm, tk), lambda i, j, k: (rb0 + i, k))]
    if w.ndim == 3:
        in_specs.append(pl.BlockSpec((None, tk, tn), lambda i, j, k: (w_index, k, j)))
    else:
        in_specs.append(pl.BlockSpec((tk, tn), lambda i, j, k: (k, j)))
    args = [x, w]
    if res is not None:
        in_specs.append(pl.BlockSpec((tm, tn), lambda i, j, k: (i, j)))
        args.append(res)
    if scale is not None:
        in_specs.append(pl.BlockSpec((tm, 1), lambda i, j, k: (i, 0)))
        args.append(scale)
    out_specs = tuple(pl.BlockSpec((tm, tn), lambda i, j, k: (i, j)) for _ in out_dtypes)
    out_shape = tuple(jax.ShapeDtypeStruct((m, n), d) for d in out_dtypes)
    scratch = [pltpu.VMEM((tm, tn), F32)] if nk > 1 else []
    vmem = (2 * tm * tk * x.dtype.itemsize + 2 * tk * tn * w.dtype.itemsize
            + tm * tn * 4 * (2 * len(out_dtypes) + 1 + (2 if res is not None else 0)))
    outs = pl.pallas_call(
        functools.partial(_mm_kernel, nk=nk, has_res=res is not None, has_scale=scale is not None,
                          n_out=len(out_dtypes)),
        grid=grid, in_specs=in_specs, out_specs=out_specs, out_shape=out_shape,
        scratch_shapes=scratch,
        compiler_params=_cparams(("parallel", "parallel", "arbitrary"), vmem),
        name="matmul",
    )(*args)
    return outs


def _rope_mm_kernel(x_ref, w1_ref, w2_ref, c_ref, s_ref, o_ref):
    x = x_ref[...]
    a = jnp.dot(x, w1_ref[...], preferred_element_type=F32)
    b = jnp.dot(x, w2_ref[...], preferred_element_type=F32)
    o_ref[...] = (a * c_ref[...] + b * s_ref[...]).astype(o_ref.dtype)


def rope_matmul(x, w1, w2, ctab, stab, out_dtype, tn):
    m, kdim = x.shape
    n = w1.shape[1]
    est = lambda rows: 2 * rows * kdim * 2 + 4 * kdim * tn * 2 + 8 * rows * tn * 4
    tm = next((c for c in (1280, 1024, 640, 512, 320, 256, 128, 64, 32, 16, 8)
               if m % c == 0 and est(c) <= VMEM_CAP // 2), 8)
    vmem = est(tm)
    return pl.pallas_call(
        _rope_mm_kernel,
        grid=(m // tm, n // tn),
        in_specs=[pl.BlockSpec((tm, kdim), lambda i, j: (i, 0)),
                  pl.BlockSpec((kdim, tn), lambda i, j: (0, j)),
                  pl.BlockSpec((kdim, tn), lambda i, j: (0, j)),
                  pl.BlockSpec((tm, tn), lambda i, j: (i, 0)),
                  pl.BlockSpec((tm, tn), lambda i, j: (i, 0))],
        out_specs=pl.BlockSpec((tm, tn), lambda i, j: (i, j)),
        out_shape=jax.ShapeDtypeStruct((m, n), out_dtype),
        compiler_params=_cparams(("parallel", "parallel"), vmem),
        name="rope_matmul",
    )(x, w1, w2, ctab, stab)


def _swiglu_up_kernel(x_ref, wg_ref, wu_ref, o_ref):
    x = x_ref[...]
    g = jnp.dot(x, wg_ref[...].astype(BF16), preferred_element_type=F32)
    u = jnp.dot(x, wu_ref[...].astype(BF16), preferred_element_type=F32)
    o_ref[...] = (g * jax.nn.sigmoid(g) * u).astype(o_ref.dtype)


def swiglu_up(x, wg, wu, w_index=None):
    m, kdim = x.shape
    n = wg.shape[-1]
    tm = _pick(m, (1280, 1024, 640, 512, 320, 256, 128, 64, 32, 16, 8))
    tn = _pick(n, (256, 128))
    if wg.ndim == 3:
        wspec = pl.BlockSpec((None, kdim, tn), lambda i, j: (w_index, 0, j))
    else:
        wspec = pl.BlockSpec((kdim, tn), lambda i, j: (0, j))
    vmem = 2 * tm * kdim * 2 + 4 * kdim * tn * wg.dtype.itemsize + 6 * tm * tn * 4
    return pl.pallas_call(
        _swiglu_up_kernel,
        grid=(m // tm, n // tn),
        in_specs=[pl.BlockSpec((tm, kdim), lambda i, j: (i, 0)), wspec, wspec],
        out_specs=pl.BlockSpec((tm, tn), lambda i, j: (i, j)),
        out_shape=jax.ShapeDtypeStruct((m, n), BF16),
        compiler_params=_cparams(("parallel", "parallel"), vmem),
        name="swiglu_up",
    )(x, wg, wu)


def _s5_kernel(u_ref, h0r_ref, h0i_ref, lr_ref, li_ref, bbr_ref, bbi_ref, ccr_ref, cci_ref,
               d_ref, wglu_ref, bglu_ref, y_ref, hr_ref, hi_ref, xr_s, xi_s, st_r, st_i, *, tc, nblk):
    c = pl.program_id(1)

    @pl.when(c == 0)
    def _():
        st_r[...] = h0r_ref[0]
        st_i[...] = h0i_ref[0]

    u = u_ref[...]
    ub = u.astype(BF16)
    sw = xr_s.shape[1] // nblk
    for k in range(nblk):
        uk = ub[:, k * LANES:(k + 1) * LANES]
        xr_s[:, k * sw:(k + 1) * sw] = jnp.dot(uk, bbr_ref[k], preferred_element_type=F32)
        xi_s[:, k * sw:(k + 1) * sw] = jnp.dot(uk, bbi_ref[k], preferred_element_type=F32)

    scan_w = 1024
    for q in range(xr_s.shape[1] // scan_w):
        cols = slice(q * scan_w, (q + 1) * scan_w)
        lr = lr_ref[:, cols]
        li = li_ref[:, cols]

        def body(t, carry, cols=cols, lr=lr, li=li):
            hr, hi = carry
            nr = lr * hr - li * hi + xr_s[pl.ds(t, 1), cols]
            ni = lr * hi + li * hr + xi_s[pl.ds(t, 1), cols]
            xr_s[pl.ds(t, 1), cols] = nr
            xi_s[pl.ds(t, 1), cols] = ni
            return nr, ni

        hr, hi = lax.fori_loop(0, tc, body, (st_r[:, cols], st_i[:, cols]))
        st_r[:, cols] = hr
        st_i[:, cols] = hi

    hr_ref[0] = st_r[...]
    hi_ref[0] = st_i[...]

    ys = []
    for k in range(nblk):
        xr = xr_s[:, k * sw:(k + 1) * sw].astype(BF16)
        xi = xi_s[:, k * sw:(k + 1) * sw].astype(BF16)
        ys.append(jnp.dot(xr, ccr_ref[k], preferred_element_type=F32)
                  - jnp.dot(xi, cci_ref[k], preferred_element_type=F32))
    y = jnp.concatenate(ys, axis=1) + d_ref[...] * u
    y = jax.nn.gelu(y)
    gate = jax.nn.sigmoid(jnp.dot(y.astype(BF16), wglu_ref[...], preferred_element_type=F32) + bglu_ref[...])
    y_ref[...] = (y * gate).astype(y_ref.dtype)


def s5_mixer(z, row_off, bsz, t, h0_re, h0_im, prm):
    width = prm["d"].shape[1]
    nblk = width // LANES
    nstate = prm["lr"].shape[1]
    tc = _pick(t, (256, 128, 64, 32, 16, 8))
    nt = t // tc
    rb0 = row_off // tc
    full = lambda shape: pl.BlockSpec(shape, lambda b, c: (0,) * len(shape))
    vmem = (4 * tc * width * 4 + 2 * tc * nstate * 4 + 4 * nblk * LANES * (nstate // nblk) * 2 * 2
            + 2 * width * width * 2 + 8 * tc * width * 4)
    y, hr, hi = pl.pallas_call(
        functools.partial(_s5_kernel, tc=tc, nblk=nblk),
        grid=(bsz, nt),
        in_specs=[pl.BlockSpec((tc, width), lambda b, c: (rb0 + b * nt + c, 0)),
                  pl.BlockSpec((1, 1, nstate), lambda b, c: (b, 0, 0)),
                  pl.BlockSpec((1, 1, nstate), lambda b, c: (b, 0, 0)),
                  full((1, nstate)), full((1, nstate)),
                  full(prm["bbr"].shape), full(prm["bbi"].shape),
                  full(prm["ccr"].shape), full(prm["cci"].shape),
                  full((1, width)), full((width, width)), full((1, width))],
        out_specs=(pl.BlockSpec((tc, width), lambda b, c: (b * nt + c, 0)),
                   pl.BlockSpec((1, 1, nstate), lambda b, c: (b, 0, 0)),
                   pl.BlockSpec((1, 1, nstate), lambda b, c: (b, 0, 0))),
        out_shape=(jax.ShapeDtypeStruct((bsz * t, width), BF16),
                   jax.ShapeDtypeStruct((bsz, 1, nstate), F32),
                   jax.ShapeDtypeStruct((bsz, 1, nstate), F32)),
        scratch_shapes=[pltpu.VMEM((tc, nstate), F32), pltpu.VMEM((tc, nstate), F32),
                        pltpu.VMEM((1, nstate), F32), pltpu.VMEM((1, nstate), F32)],
        compiler_params=_cparams(("parallel", "arbitrary"), vmem),
        name="s5_mixer",
    )(z, h0_re.reshape(bsz, 1, nstate), h0_im.reshape(bsz, 1, nstate), prm["lr"], prm["li"],
      prm["bbr"], prm["bbi"], prm["ccr"], prm["cci"], prm["d"], prm["wglu"], prm["bglu"])
    return y, hr, hi


def s5_params(a_re, a_im, log_dt, b_re, b_im, c_re, c_im, d_skip, w_glu, b_glu):
    g, p = a_re.shape
    nch = b_re.shape[2]
    dt = jnp.exp(log_dt.astype(F32))[:, None]
    ar, ai = a_re.astype(F32), a_im.astype(F32)
    mag = jnp.exp(ar * dt)
    lr = mag * jnp.cos(ai * dt)
    li = mag * jnp.sin(ai * dt)
    den = ar * ar + ai * ai
    fr = ((lr - 1.0) * ar + li * ai) / den
    fi = (li * ar - (lr - 1.0) * ai) / den
    br, bi = b_re.astype(F32), b_im.astype(F32)
    bbr = fr[..., None] * br - fi[..., None] * bi
    bbi = fr[..., None] * bi + fi[..., None] * br
    gb = S5_BLOCK_GROUPS
    nblk = g // gb
    eye = jnp.eye(gb, dtype=F32)

    def blk_in(m):
        m = m.reshape(nblk, gb, p, nch)
        return jnp.einsum("kgpn,gh->kgnhp", m, eye).reshape(nblk, gb * nch, gb * p).astype(BF16)

    def blk_out(m):
        m = m.astype(F32).reshape(nblk, gb, nch, p)
        return jnp.einsum("kgnp,gh->kgphn", m, eye).reshape(nblk, gb * p, gb * nch).astype(BF16)

    width = g * nch
    return dict(lr=lr.reshape(1, g * p), li=li.reshape(1, g * p), bbr=blk_in(bbr), bbi=blk_in(bbi),
                ccr=blk_out(c_re), cci=blk_out(c_im), d=d_skip.astype(F32).reshape(1, width),
                wglu=w_glu.astype(BF16), bglu=b_glu.astype(F32).reshape(1, width))


MLA_ROWS = 1024
MLA_TQ = 512
MLA_CHAIN_ROWS = 256


def _mla_kernel(q_ref, ckv_ref, kpe_ref, wuk_ref, wuv_ref, o_ref, qa_s, qpe_s, m_s, l_s, acc_s,
                *, tq, tk, nk_total, klen, causal, scale, hps):
    i = pl.program_id(1)
    nope = wuk_ref.shape[1]
    qw = q_ref.shape[1] // hps
    vdim = wuv_ref.shape[2]
    rows_all = hps * tq
    rc = MLA_CHAIN_ROWS if rows_all % MLA_CHAIN_ROWS == 0 else rows_all
    for hh in range(hps):
        rows = slice(hh * tq, (hh + 1) * tq)
        qa = jnp.dot(q_ref[:, hh * qw:hh * qw + nope], wuk_ref[hh], preferred_element_type=F32)
        qa_s[rows, :] = qa.astype(qa_s.dtype)
        qpe_s[rows, :] = q_ref[:, hh * qw + nope:(hh + 1) * qw]
    m_s[...] = jnp.full(m_s.shape, NEG_INF, F32)
    l_s[...] = jnp.zeros(l_s.shape, F32)
    acc_s[...] = jnp.zeros(acc_s.shape, F32)

    def tile(j, masked):
        ks = pl.multiple_of(j * tk, tk)
        ckv = ckv_ref[0, pl.ds(ks, tk), :]
        kpe = kpe_ref[0, pl.ds(ks, tk), :]
        scores = []
        for c0 in range(0, rows_all, rc):
            rows = slice(c0, c0 + rc)
            s = lax.dot_general(qa_s[rows, :].astype(BF16), ckv, (((1,), (1,)), ((), ())), preferred_element_type=F32)
            s = s + lax.dot_general(qpe_s[rows, :], kpe, (((1,), (1,)), ((), ())), preferred_element_type=F32)
            scores.append(s * scale)
        for c0, s in zip(range(0, rows_all, rc), scores):
            rows = slice(c0, c0 + rc)
            if masked:
                kpos = ks + lax.broadcasted_iota(jnp.int32, (rc, tk), 1)
                if causal:
                    qpos = i * tq + (c0 + lax.broadcasted_iota(jnp.int32, (rc, tk), 0)) % tq
                    s = jnp.where(kpos // CHUNK <= qpos // CHUNK, s, NEG_INF)
                if klen < nk_total * tk:
                    s = jnp.where(kpos < klen, s, NEG_INF)
            m_prev = m_s[rows, :]
            m_new = jnp.maximum(m_prev, jnp.max(s, axis=1, keepdims=True))
            alpha = jnp.exp(m_prev - m_new)
            p = jnp.exp(s - m_new)
            l_s[rows, :] = alpha * l_s[rows, :] + jnp.sum(p, axis=1, keepdims=True)
            acc_s[rows, :] = alpha * acc_s[rows, :] + jnp.dot(p.astype(BF16), ckv, preferred_element_type=F32)
            m_s[rows, :] = m_new

    def full_tile(j, c):
        tile(j, False)
        return c

    n_full = i if causal else nk_total - 1
    lax.fori_loop(0, n_full, full_tile, 0)
    tile(n_full, True)
    o_lat = (acc_s[...] / l_s[...]).astype(BF16)
    for hh in range(hps):
        o_ref[:, hh * vdim:(hh + 1) * vdim] = jnp.dot(
            o_lat[hh * tq:(hh + 1) * tq], wuv_ref[hh], preferred_element_type=F32).astype(o_ref.dtype)


def mla_attention(q_all, row_off, bsz, t, ckv, kpe, wuk, wuv, klen, causal, scale):
    nheads, nope, lat = wuk.shape
    vdim = wuv.shape[2]
    qw = q_all.shape[1] // nheads
    tkeys = ckv.shape[1]
    tq = _pick(t, (MLA_TQ, 256, 128, 64, 32, 16, 8))
    tk = tq if causal else _pick(tkeys, (256, 128))
    if causal:
        assert tq % CHUNK == 0 and tkeys == t
    hps = _pick(nheads, tuple(c for c in (24, 16, 12, 8, 6, 4, 3, 2, 1) if c * tq <= MLA_ROWS))
    nq = t // tq
    rb0 = row_off // tq
    rows_all = hps * tq
    vmem = (2 * tkeys * (lat + LANES) * 2 + 4 * tq * hps * qw * 2 + 3 * rows_all * lat * 4 + 8 * rows_all * tk * 4
            + 4 * hps * (nope + vdim) * lat * 2 + (2 << 20))
    return pl.pallas_call(
        functools.partial(_mla_kernel, tq=tq, tk=tk, nk_total=tkeys // tk, klen=klen, causal=causal, scale=scale,
                          hps=hps),
        grid=(bsz, nq, nheads // hps),
        in_specs=[pl.BlockSpec((tq, hps * qw), lambda b, i, h: (rb0 + b * nq + i, h)),
                  pl.BlockSpec((1, tkeys, lat), lambda b, i, h: (b, 0, 0)),
                  pl.BlockSpec((1, tkeys, LANES), lambda b, i, h: (b, 0, 0)),
                  pl.BlockSpec((hps, nope, lat), lambda b, i, h: (h, 0, 0)),
                  pl.BlockSpec((hps, lat, vdim), lambda b, i, h: (h, 0, 0))],
        out_specs=pl.BlockSpec((tq, hps * vdim), lambda b, i, h: (b * nq + i, h)),
        out_shape=jax.ShapeDtypeStruct((bsz * t, nheads * vdim), BF16),
        scratch_shapes=[pltpu.VMEM((rows_all, lat), F32), pltpu.VMEM((rows_all, LANES), BF16),
                        pltpu.VMEM((rows_all, 1), F32), pltpu.VMEM((rows_all, 1), F32),
                        pltpu.VMEM((rows_all, lat), F32)],
        compiler_params=_cparams(("parallel", "parallel", "arbitrary"), vmem),
        name="mla_attention",
    )(q_all, ckv, kpe, wuk, wuv)


def _fox_gate_kernel(pre_ref, zf_ref, bf_ref, logf_ref, cum_ref, cumt_ref, lf_s, *, npre, t, blk):
    total = lf_s.shape[0]
    z = zf_ref[0] + bf_ref[...]
    logf = jnp.minimum(z, 0.0) - jnp.log1p(jnp.exp(-jnp.abs(z)))
    logf_ref[0] = logf
    if npre + t < total:
        lf_s[...] = jnp.zeros(lf_s.shape, F32)
    if npre:
        lf_s[0:npre, :] = pre_ref[0]
    lf_s[npre:npre + t, :] = logf
    tri = (lax.broadcasted_iota(jnp.int32, (blk, blk), 1)
           <= lax.broadcasted_iota(jnp.int32, (blk, blk), 0)).astype(F32)
    carry = jnp.zeros((1, LANES), F32)
    for c in range(total // blk):
        rows = slice(c * blk, (c + 1) * blk)
        cum = jnp.dot(tri, lf_s[rows, :], preferred_element_type=F32, precision=lax.Precision.HIGHEST) + carry
        cum_ref[0, rows, :] = cum
        cumt_ref[0, c] = cum.T[:cumt_ref.shape[2], :]
        carry = cum[blk - 1:blk, :]


def fox_gate(zf, row_off, bsz, t, b_f, pre):
    nheads = b_f.shape[0]
    npre = 0 if pre is None else pre.shape[1]
    total = -(-(npre + t) // LANES) * LANES
    if pre is None:
        pre = jnp.zeros((bsz, 8, LANES), F32)
    pp = pre.shape[1]
    hrows = -(-nheads // 8) * 8
    zf3 = zf[row_off:row_off + bsz * t].reshape(bsz, t, LANES)
    bfp = jnp.zeros((1, LANES), F32).at[0, :nheads].set(b_f.astype(F32))
    return pl.pallas_call(
        functools.partial(_fox_gate_kernel, npre=npre, t=t, blk=LANES),
        grid=(bsz,),
        in_specs=[pl.BlockSpec((1, pp, LANES), lambda b: (b, 0, 0)),
                  pl.BlockSpec((1, t, LANES), lambda b: (b, 0, 0)),
                  pl.BlockSpec((1, LANES), lambda b: (0, 0))],
        out_specs=(pl.BlockSpec((1, t, LANES), lambda b: (b, 0, 0)),
                   pl.BlockSpec((1, total, LANES), lambda b: (b, 0, 0)),
                   pl.BlockSpec((1, total // LANES, hrows, LANES), lambda b: (b, 0, 0, 0))),
        out_shape=(jax.ShapeDtypeStruct((bsz, t, LANES), F32),
                   jax.ShapeDtypeStruct((bsz, total, LANES), F32),
                   jax.ShapeDtypeStruct((bsz, total // LANES, hrows, LANES), F32)),
        scratch_shapes=[pltpu.VMEM((total, LANES), F32)],
        compiler_params=_cparams(("parallel",), 12 * total * LANES * 4),
        name="fox_gate",
    )(pre, zf3, bfp)


FOX_HEADS_PER_STEP = 4
FOX_TQ = 512


def _fox_attn_kernel(q_ref, k_ref, v_ref, cq_ref, ck_ref, o_ref, m_s, l_s, acc_s,
                     *, tq, tk, qoff, scale, hps):
    hg = pl.program_id(1)
    i = pl.program_id(2)
    hd = acc_s.shape[2]
    lane = lax.broadcasted_iota(jnp.int32, cq_ref.shape[1:], 1)
    cq = cq_ref[0]
    fq = [jnp.sum(jnp.where(lane == hg * hps + hh, cq, 0.0), axis=1, keepdims=True) for hh in range(hps)]
    m_s[...] = jnp.full(m_s.shape, NEG_INF, F32)
    l_s[...] = jnp.zeros(l_s.shape, F32)
    acc_s[...] = jnp.zeros(acc_s.shape, F32)

    def tile(j, masked):
        ks = pl.multiple_of(j * tk, tk)
        scores = []
        for hh in range(hps):
            cols = slice(hh * hd, (hh + 1) * hd)
            k = k_ref[pl.ds(ks, tk), cols]
            fk = jnp.concatenate([ck_ref[0, j * (tk // LANES) + c, pl.ds(hg * hps + hh, 1), :]
                                  for c in range(tk // LANES)], axis=1)
            s = lax.dot_general(q_ref[:, cols], k, (((1,), (1,)), ((), ())), preferred_element_type=F32) * scale
            scores.append(s + fq[hh] - fk)
        for hh, s in enumerate(scores):
            cols = slice(hh * hd, (hh + 1) * hd)
            if masked:
                qpos = qoff + i * tq + lax.broadcasted_iota(jnp.int32, (tq, tk), 0)
                kpos = ks + lax.broadcasted_iota(jnp.int32, (tq, tk), 1)
                s = jnp.where(kpos <= qpos, s, NEG_INF)
            m_prev = m_s[hh]
            m_new = jnp.maximum(m_prev, jnp.max(s, axis=1, keepdims=True))
            alpha = jnp.exp(m_prev - m_new)
            p = jnp.exp(s - m_new)
            l_s[hh] = alpha * l_s[hh] + jnp.sum(p, axis=1, keepdims=True)
            v = v_ref[pl.ds(ks, tk), cols]
            acc_s[hh] = alpha * acc_s[hh] + jnp.dot(p.astype(BF16), v, preferred_element_type=F32)
            m_s[hh] = m_new

    def full_tile(j, c):
        tile(j, False)
        return c

    n_full = (qoff + i * tq) // tk
    lax.fori_loop(0, n_full, full_tile, 0)
    tile(n_full, True)
    for hh in range(hps):
        o_ref[:, hh * hd:(hh + 1) * hd] = (acc_s[hh] / l_s[hh]).astype(o_ref.dtype)


def fox_attention(q_all, row_off, bsz, t, k, v, tkeys, cum, cumt, qoff, scale):
    hd = LANES
    nheads = q_all.shape[1] // hd
    hps = _pick(nheads, (FOX_HEADS_PER_STEP, 2, 1))
    tq = _pick(t, (FOX_TQ, 256, 128, 64, 32, 16, 8))
    tk = tkeys if tq * tkeys * 4 <= (256 << 10) else _pick(tkeys, (tq, 256, 128))
    assert tk % tq == 0 and qoff % tq == 0 and qoff + t <= tkeys
    nq = t // tq
    rb0 = row_off // tq
    cq0 = qoff // tq
    vmem = (8 * tkeys * hps * hd * 2 + 2 * cumt.shape[2] * tkeys * 4 + 4 * hps * tq * tk * 4
            + 8 * hps * tq * hd * 4 + (2 << 20))
    return pl.pallas_call(
        functools.partial(_fox_attn_kernel, tq=tq, tk=tk, qoff=qoff, scale=scale, hps=hps),
        grid=(bsz, nheads // hps, nq),
        in_specs=[pl.BlockSpec((tq, hps * hd), lambda b, h, i: (rb0 + b * nq + i, h)),
                  pl.BlockSpec((tkeys, hps * hd), lambda b, h, i: (b, h)),
                  pl.BlockSpec((tkeys, hps * hd), lambda b, h, i: (b, h)),
                  pl.BlockSpec((1, tq, LANES), lambda b, h, i: (b, cq0 + i, 0)),
                  pl.BlockSpec((1,) + cumt.shape[1:], lambda b, h, i: (b, 0, 0, 0))],
        out_specs=pl.BlockSpec((tq, hps * hd), lambda b, h, i: (b * nq + i, h)),
        out_shape=jax.ShapeDtypeStruct((bsz * t, nheads * hd), BF16),
        scratch_shapes=[pltpu.VMEM((hps, tq, 1), F32), pltpu.VMEM((hps, tq, 1), F32), pltpu.VMEM((hps, tq, hd), F32)],
        compiler_params=_cparams(("parallel", "parallel", "arbitrary"), vmem),
        name="fox_attention",
    )(q_all, k, v, cum, cumt)


def _rwkv_prep_kernel(z_ref, zp_ref, sh_ref, mu_ref, w0_ref, a0_ref, w2_ref, a2_ref, g2_ref,
                      r_ref, k_ref, v_ref, lw_ref, a_ref, g_ref, *, w):
    i = pl.program_id(1)
    z = z_ref[...]
    prev_row = jnp.where(i == 0, sh_ref[0], zp_ref[7:8, :])
    row = lax.broadcasted_iota(jnp.int32, z.shape, 0)
    z_prev = jnp.where(row == 0, prev_row, pltpu.roll(z, 1, 0))
    zm = z + (z_prev - z) * mu_ref[...]
    slab = zm[:, 3 * w:]
    lora_w = jnp.dot(jnp.tanh(slab).astype(BF16), w2_ref[...], preferred_element_type=F32)
    x = -(w0_ref[...] + lora_w)
    softplus = jnp.maximum(x, 0.0) + jnp.log1p(jnp.exp(-jnp.abs(x)))
    lw = -jnp.exp(-softplus - 0.5)
    a = jax.nn.sigmoid(a0_ref[...] + jnp.dot(slab.astype(BF16), a2_ref[...], preferred_element_type=F32))
    g = jnp.dot(jax.nn.sigmoid(slab).astype(BF16), g2_ref[...], preferred_element_type=F32)
    nh, hd = r_ref.shape[1], r_ref.shape[3]
    for ref, val in ((r_ref, zm[:, 0:w]), (k_ref, zm[:, w:2 * w]), (v_ref, zm[:, 2 * w:3 * w]),
                     (lw_ref, lw), (a_ref, a), (g_ref, g)):
        for hh in range(nh):
            ref[0, hh] = val[:, hh * hd:(hh + 1) * hd]


def rwkv_prep(zr, row_off, bsz, t, shift_prev, prm, nh):
    wz = zr.shape[1]
    w = prm["w0"].shape[1]
    hd = w // nh
    tt = _pick(t, (128, 64, 32, 16, 8))
    nt = t // tt
    rb0 = row_off // tt
    full = lambda shape: pl.BlockSpec(shape, lambda b, i: (0,) * len(shape))
    ospec = pl.BlockSpec((1, nh, tt, hd), lambda b, i: (b, 0, i, 0))
    oshape = jax.ShapeDtypeStruct((bsz, nh, t, hd), F32)
    ls = wz - 3 * w
    vmem = 6 * tt * wz * 4 + 12 * tt * nh * LANES * 4 + 6 * ls * w * 2 + 8 * tt * w * 4
    return pl.pallas_call(
        functools.partial(_rwkv_prep_kernel, w=w),
        grid=(bsz, nt),
        in_specs=[pl.BlockSpec((tt, wz), lambda b, i: (rb0 + b * nt + i, 0)),
                  pl.BlockSpec((8, wz), lambda b, i: (jnp.maximum((row_off + (b * nt + i) * tt) // 8 - 1, 0), 0)),
                  pl.BlockSpec((1, 1, wz), lambda b, i: (b, 0, 0)),
                  full((1, wz)), full((1, w)), full((1, w)), full((ls, w)), full((ls, w)), full((ls, w))],
        out_specs=(ospec,) * 6,
        out_shape=(oshape,) * 6,
        compiler_params=_cparams(("parallel", "arbitrary"), vmem),
        name="rwkv_prep",
    )(zr, zr, shift_prev.reshape(bsz, 1, wz), prm["mu"], prm["w0"], prm["a0"], prm["w2"], prm["a2"], prm["g2"])


def _bdot(a, b, dims):
    return lax.dot_general(a.astype(BF16), b.astype(BF16), dims, preferred_element_type=F32)


def _cumsum_rows(tri, x):
    hi = x.astype(BF16)
    r1 = x - hi.astype(F32)
    mid = r1.astype(BF16)
    lo = (r1 - mid.astype(F32)).astype(BF16)
    t = tri.astype(BF16)
    dot = lambda p: lax.dot_general(t, p, _NN, preferred_element_type=F32)
    return dot(hi) + dot(mid) + dot(lo)


_NT = (((2,), (2,)), ((0,), (0,)))
_NN = (((2,), (1,)), ((0,), (0,)))
_TN = (((1,), (1,)), ((0,), (0,)))


def _rwkv_scan_kernel(r_ref, k_ref, v_ref, lw_ref, a_ref, g_ref, s0_ref, kk_ref, ka_ref, rk_ref,
                      lnw_ref, lnb_ref, y_ref, sout_ref, st_s, *, chunk):
    c = pl.program_id(2)

    @pl.when(c == 0)
    def _():
        st_s[...] = s0_ref[0]

    r = r_ref[0]
    k = k_ref[0]
    v = v_ref[0]
    lw = lw_ref[0]
    a = a_ref[0]
    hb = r.shape[0]
    s0 = st_s[...]

    kk = k * kk_ref[...]
    kk = kk / jnp.maximum(jnp.sqrt(jnp.sum(kk * kk, axis=-1, keepdims=True)), 1e-12)
    kmod = k * (1.0 + (a - 1.0) * ka_ref[...])

    li = lax.broadcasted_iota(jnp.int32, (chunk, chunk), 0)
    mi = lax.broadcasted_iota(jnp.int32, (chunk, chunk), 1)
    incl = (mi <= li).astype(F32)
    strict = (mi < li).astype(F32)
    cs = _cumsum_rows(jnp.broadcast_to(incl, (hb, chunk, chunk)), lw)
    dec_in = jnp.exp(cs)
    dec_ex = jnp.exp(cs - lw)
    inv = jnp.exp(-cs)
    p_rows = jnp.concatenate([-kk * dec_ex, r * dec_in], axis=1)
    q_rows = jnp.concatenate([kk * a * inv, kmod * inv], axis=1)
    mm = _bdot(p_rows, q_rows, _NT)
    a_ab = mm[:, :chunk, :chunk] * strict
    a_ak = mm[:, :chunk, chunk:] * strict
    r_b = mm[:, chunk:, :chunk] * incl
    r_k = mm[:, chunk:, chunk:] * incl
    ps = _bdot(p_rows, s0, _NT)
    x = ps[:, :chunk] + _bdot(a_ak, v, _NN)
    pw = a_ab
    n_iter = chunk.bit_length() - 1
    for it in range(n_iter):
        x = x + _bdot(pw, x, _NN)
        if it + 1 < n_iter:
            pw = _bdot(pw, pw, _NN)
    y = ps[:, chunk:] + _bdot(r_b, x, _NN) + _bdot(r_k, v, _NN)
    uv = jnp.concatenate([x, v], axis=1)
    s_new = (s0 + _bdot(uv, q_rows, _TN)) * dec_in[:, chunk - 1:chunk, :]
    st_s[...] = s_new
    sout_ref[0] = s_new

    mean = jnp.mean(y, axis=-1, keepdims=True)
    var = jnp.mean(jnp.square(y - mean), axis=-1, keepdims=True)
    yn = (y - mean) * lax.rsqrt(var + RWKV_LN_EPS) * lnw_ref[...] + lnb_ref[...]
    bonus = jnp.sum(r * kmod * rk_ref[...], axis=-1, keepdims=True) * v
    out = (yn + bonus) * g_ref[0]
    hd = out.shape[2]
    for hh in range(hb):
        y_ref[:, hh * hd:(hh + 1) * hd] = out[hh].astype(y_ref.dtype)


def rwkv_scan(r, k, v, lw, a, g, s0, prm):
    bsz, nh, t, hd = r.shape
    chunk = _pick(t, (64, 32, 16, 8))
    hb = _pick(nh, (32, 16, 8, 4, 2, 1))
    nc = t // chunk
    xspec = pl.BlockSpec((1, hb, chunk, hd), lambda b, h, c: (b, h, c, 0))
    pspec = pl.BlockSpec((hb, 1, hd), lambda b, h, c: (h, 0, 0))
    sspec = pl.BlockSpec((1, hb, hd, hd), lambda b, h, c: (b, h, 0, 0))
    vmem = 16 * hb * chunk * LANES * 4 + 40 * hb * 2 * chunk * LANES * 4 + 6 * hb * hd * LANES * 4
    y, s = pl.pallas_call(
        functools.partial(_rwkv_scan_kernel, chunk=chunk),
        grid=(bsz, nh // hb, nc),
        in_specs=[xspec] * 6 + [sspec] + [pspec] * 5,
        out_specs=(pl.BlockSpec((chunk, hb * hd), lambda b, h, c: (b * nc + c, h)), sspec),
        out_shape=(jax.ShapeDtypeStruct((bsz * t, nh * hd), BF16), jax.ShapeDtypeStruct((bsz, nh, hd, hd), F32)),
        scratch_shapes=[pltpu.VMEM((hb, hd, hd), F32)],
        compiler_params=_cparams(("parallel", "parallel", "arbitrary"), vmem),
        name="rwkv_scan",
    )(r, k, v, lw, a, g, s0, prm["k_k"], prm["k_a"], prm["r_k"], prm["ln_w"], prm["ln_b"])
    return y, s


def _router_kernel(x_ref, w_ref, comb_ref, *, n_experts):
    logits = jnp.dot(x_ref[...].astype(BF16), w_ref[...], preferred_element_type=F32)
    lane = lax.broadcasted_iota(jnp.int32, logits.shape, 1)
    big = jnp.int32(LANES)
    logits = jnp.where(lane < n_experts, logits, -jnp.inf)
    t1 = jnp.max(logits, axis=1, keepdims=True)
    i1 = jnp.min(jnp.where(logits == t1, lane, big), axis=1, keepdims=True)
    rest = jnp.where(lane == i1, -jnp.inf, logits)
    t2 = jnp.max(rest, axis=1, keepdims=True)
    i2 = jnp.min(jnp.where(rest == t2, lane, big), axis=1, keepdims=True)
    e2 = jnp.exp(t2 - t1)
    den = 1.0 + e2
    comb_ref[...] = jnp.where(lane == 0, 1.0 / den, jnp.where(lane == 1, e2 / den, jnp.where(
        lane == 2, i1.astype(F32), jnp.where(lane == 3, i2.astype(F32), 0.0))))


def moe_router(x, w_router):
    m, d = x.shape
    ne = w_router.shape[1]
    wp = jnp.zeros((d, LANES), BF16).at[:, :ne].set(w_router.astype(BF16))
    tm = _pick(m, (640, 512, 320, 256, 128, 64, 32, 16, 8))
    return pl.pallas_call(
        functools.partial(_router_kernel, n_experts=ne),
        grid=(m // tm,),
        in_specs=[pl.BlockSpec((tm, d), lambda i: (i, 0)), pl.BlockSpec((d, LANES), lambda i: (0, 0))],
        out_specs=pl.BlockSpec((tm, LANES), lambda i: (i, 0)),
        out_shape=jax.ShapeDtypeStruct((m, LANES), F32),
        compiler_params=_cparams(("parallel",), 3 * tm * d * 4 + 2 * d * LANES * 2 + 8 * tm * LANES * 4),
        name="moe_router",
    )(x, wp)


MOE_TILE = 512


def _moe_plan(rout, n_exp, tile):
    n = rout.shape[0]
    pair_e = rout[:, 2:4].astype(jnp.int32).reshape(-1)
    onehot = (pair_e[:, None] == jnp.arange(n_exp, dtype=jnp.int32)[None, :]).astype(jnp.int32)
    csum = jnp.cumsum(onehot, axis=0)
    rank = jnp.take_along_axis(csum - onehot, pair_e[:, None], axis=1)[:, 0]
    gsz = (csum[-1] + tile - 1) // tile * tile
    gend = jnp.cumsum(gsz)
    slot = ((gend - gsz)[pair_e] + rank).astype(jnp.int32)
    n_tiles = -(-2 * n // tile) + n_exp
    tok = jnp.zeros((n_tiles * tile,), jnp.int32).at[slot].set(jnp.arange(2 * n, dtype=jnp.int32) // 2)
    tile_start = jnp.arange(n_tiles, dtype=jnp.int32) * tile
    te = jnp.minimum(jnp.searchsorted(gend, tile_start, side="right"), n_exp - 1).astype(jnp.int32)
    used = (gend[-1] // tile).astype(jnp.int32).reshape(1)
    return slot, tok, te, used, n_tiles


def _row_copy(src_hbm, row, dst, r, sem):
    return pltpu.make_async_copy(src_hbm.at[pl.ds(row, 1)], dst.at[pl.ds(r, 1)], sem)


def _moe_gather_kernel(tok_ref, used_ref, x_hbm, o_ref, buf, sem, *, tile):
    i = pl.program_id(0)

    @pl.when(i < used_ref[0])
    def _():
        base = i * tile

        def issue(r, c):
            _row_copy(x_hbm, tok_ref[base + r], buf, r, sem).start()
            return c

        def wait(r, c):
            _row_copy(x_hbm, 0, buf, r, sem).wait()
            return c

        lax.fori_loop(0, tile, issue, 0)
        lax.fori_loop(0, tile, wait, 0)
        o_ref[...] = buf[...].astype(o_ref.dtype)

    @pl.when(i >= used_ref[0])
    def _():
        o_ref[...] = jnp.zeros(o_ref.shape, o_ref.dtype)


def moe_gather(x, tok, used, n_tiles, tile):
    d = x.shape[1]
    return pl.pallas_call(
        functools.partial(_moe_gather_kernel, tile=tile),
        grid_spec=pltpu.PrefetchScalarGridSpec(
            num_scalar_prefetch=2, grid=(n_tiles,),
            in_specs=[pl.BlockSpec(memory_space=pl.ANY)],
            out_specs=pl.BlockSpec((tile, d), lambda i, tok_r, used_r: (i, 0)),
            scratch_shapes=[pltpu.VMEM((tile, d), x.dtype), pltpu.SemaphoreType.DMA(())]),
        out_shape=jax.ShapeDtypeStruct((n_tiles * tile, d), BF16),
        compiler_params=_cparams(("arbitrary",), 4 * tile * d * 4),
        name="moe_gather",
    )(tok, used, x)


def _moe_up_kernel(te_ref, used_ref, x_ref, wg_ref, wu_ref, o_ref, wg_s, wu_s):
    i = pl.program_id(1)

    @pl.when((i == 0) | (te_ref[i] != te_ref[jnp.maximum(i - 1, 0)]))
    def _():
        wg_s[...] = wg_ref[...].astype(BF16)
        wu_s[...] = wu_ref[...].astype(BF16)

    @pl.when(i < used_ref[0])
    def _():
        x = x_ref[...]
        g = jnp.dot(x, wg_s[...], preferred_element_type=F32)
        u = jnp.dot(x, wu_s[...], preferred_element_type=F32)
        o_ref[...] = (g * jax.nn.sigmoid(g) * u).astype(o_ref.dtype)

    @pl.when(i >= used_ref[0])
    def _():
        o_ref[...] = jnp.zeros(o_ref.shape, o_ref.dtype)


def moe_up(xs, wg, wu, te, used, tile):
    p_rows, d = xs.shape
    f = wg.shape[2]
    tn = _pick(f, (512, 256, 128))
    wspec = pl.BlockSpec((None, d, tn), lambda j, i, te_r, used_r: (te_r[i], 0, j))
    vmem = 2 * tile * d * 2 + 4 * d * tn * 4 + 2 * d * tn * 2 + 6 * tile * tn * 4
    return pl.pallas_call(
        _moe_up_kernel,
        grid_spec=pltpu.PrefetchScalarGridSpec(
            num_scalar_prefetch=2, grid=(f // tn, p_rows // tile),
            in_specs=[pl.BlockSpec((tile, d), lambda j, i, te_r, used_r: (i, 0)), wspec, wspec],
            out_specs=pl.BlockSpec((tile, tn), lambda j, i, te_r, used_r: (i, j)),
            scratch_shapes=[pltpu.VMEM((d, tn), BF16), pltpu.VMEM((d, tn), BF16)]),
        out_shape=jax.ShapeDtypeStruct((p_rows, f), BF16),
        compiler_params=_cparams(("parallel", "arbitrary"), vmem),
        name="moe_up",
    )(te, used, xs, wg, wu)


def _moe_down_kernel(te_ref, used_ref, x_ref, w_ref, o_ref, *, per_tile):
    i = pl.program_id(1)

    @pl.when(i < used_ref[0] * per_tile)
    def _():
        o_ref[...] = jnp.dot(x_ref[...], w_ref[...], preferred_element_type=F32)

    @pl.when(i >= used_ref[0] * per_tile)
    def _():
        o_ref[...] = jnp.zeros(o_ref.shape, o_ref.dtype)


def moe_down(act, wd, te, used, tile):
    p_rows, f = act.shape
    d = wd.shape[2]
    tn = _pick(d, (512, 256, 128))
    rows = _pick(tile, (256, 128, 64, 32, 16, 8))
    per_tile = tile // rows
    vmem = 2 * rows * f * 2 + 2 * f * tn * 2 + 4 * rows * tn * 4
    return pl.pallas_call(
        functools.partial(_moe_down_kernel, per_tile=per_tile),
        grid_spec=pltpu.PrefetchScalarGridSpec(
            num_scalar_prefetch=2, grid=(d // tn, p_rows // rows),
            in_specs=[pl.BlockSpec((rows, f), lambda j, i, te_r, used_r: (i, 0)),
                      pl.BlockSpec((None, f, tn), lambda j, i, te_r, used_r: (te_r[i // per_tile], 0, j))],
            out_specs=pl.BlockSpec((rows, tn), lambda j, i, te_r, used_r: (i, j))),
        out_shape=jax.ShapeDtypeStruct((p_rows, d), F32),
        compiler_params=_cparams(("parallel", "arbitrary"), vmem),
        name="moe_down",
    )(te, used, act, wd)


def _moe_combine_kernel(slot_ref, h_ref, g_ref, ys_hbm, fn_ref, o_ref, buf, sem, *, tc, rb0):
    base = (rb0 + pl.program_id(0)) * tc

    def issue(r, c):
        p = 2 * (base + r)
        _row_copy(ys_hbm, slot_ref[p], buf.at[0], r, sem).start()
        _row_copy(ys_hbm, slot_ref[p + 1], buf.at[1], r, sem).start()
        return c

    def wait(r, c):
        _row_copy(ys_hbm, 0, buf.at[0], r, sem).wait()
        _row_copy(ys_hbm, 0, buf.at[1], r, sem).wait()
        return c

    lax.fori_loop(0, tc, issue, 0)
    lax.fori_loop(0, tc, wait, 0)
    g = g_ref[...]
    x = h_ref[...] + (g[:, 0:1] * buf[0] + g[:, 1:2] * buf[1])
    y = x * lax.rsqrt(jnp.mean(x * x, axis=-1, keepdims=True) + NORM_EPS)
    o_ref[...] = y * fn_ref[...]


def moe_combine_norm(h, rout, ys, slot, final_norm, row0, n):
    d = h.shape[1]
    tc = _pick(math.gcd(n, row0) if row0 else n, (256, 128, 64, 32, 16, 8))
    rb0 = row0 // tc
    return pl.pallas_call(
        functools.partial(_moe_combine_kernel, tc=tc, rb0=rb0),
        grid_spec=pltpu.PrefetchScalarGridSpec(
            num_scalar_prefetch=1, grid=(n // tc,),
            in_specs=[pl.BlockSpec((tc, d), lambda i, s: (rb0 + i, 0)),
                      pl.BlockSpec((tc, LANES), lambda i, s: (rb0 + i, 0)),
                      pl.BlockSpec(memory_space=pl.ANY),
                      pl.BlockSpec((1, d), lambda i, s: (0, 0))],
            out_specs=pl.BlockSpec((tc, d), lambda i, s: (i, 0)),
            scratch_shapes=[pltpu.VMEM((2, tc, d), F32), pltpu.SemaphoreType.DMA(())]),
        out_shape=jax.ShapeDtypeStruct((n, d), F32),
        compiler_params=_cparams(("arbitrary",), 8 * tc * d * 4),
        name="moe_combine_norm",
    )(slot, h, rout, ys, final_norm.astype(F32).reshape(1, d))


def _rope_tables(pos, rope, width, lead):
    inv = ROPE_THETA ** (-jnp.arange(0, rope, 2, dtype=F32) / rope)
    ang = pos.astype(F32)[:, None] * inv[None, :]
    cos, sin = jnp.cos(ang), jnp.sin(ang)
    n = pos.shape[0]
    ctab = jnp.concatenate([jnp.ones((n, lead), F32), cos, cos, jnp.zeros((n, width - lead - rope), F32)], axis=1)
    stab = jnp.concatenate([jnp.zeros((n, lead), F32), -sin, sin, jnp.zeros((n, width - lead - rope), F32)], axis=1)
    return ctab, stab


def _swap_halves(w):
    half = w.shape[-1] // 2
    return jnp.concatenate([w[..., half:], w[..., :half]], axis=-1)


def kernel(x_prompt, x_sample, cache_mla_ckv, cache_mla_kpe, state_s5_re, state_s5_im, state_rwkv_wkv, state_rwkv_shift, cache_fox_k, cache_fox_v, cache_fox_logf, ln0_mix, w_in0, s5_a_re, s5_a_im, s5_log_dt, s5_b_re, s5_b_im, s5_c_re, s5_c_im, s5_d, s5_w_glu, s5_b_glu, mla_q_norm, mla_w_q_up, mla_kv_norm, mla_w_uk, mla_w_uv, w_out0, ln0_ffn, ffn_w_gate, ffn_w_up, ffn_w_down, ln1_mix, w_in1, rwkv_mu, rwkv_w0, rwkv_w2, rwkv_a0, rwkv_a2, rwkv_g2, rwkv_k_k, rwkv_k_a, rwkv_r_k, rwkv_ln_w, rwkv_ln_b, fox_b_f, w_out1, ln1_ffn, moe_w_router, moe_w_gate, moe_w_up, moe_w_down, final_norm):
    bp, tp, d = x_prompt.shape
    bs, ts, _ = x_sample.shape
    past = cache_mla_ckv.shape[1]
    n_p, n_s = bp * tp, bs * ts
    ntok = n_p + n_s
    streams = ((0, bp, tp), (n_p, bs, ts))

    h = jnp.concatenate([x_prompt.reshape(n_p, d), x_sample.reshape(n_s, d)], axis=0)

    s5_w = s5_d.shape[0]
    q_rank = mla_q_norm.shape[0]
    kv_rank = mla_kv_norm.shape[0]
    n_mh, qk = mla_w_q_up.shape[1], mla_w_q_up.shape[2]
    nope = mla_w_uk.shape[2]
    rope = qk - nope
    vdim = mla_w_uv.shape[2]
    mla_scale = float(qk) ** -0.5
    qw = 2 * LANES
    kvw = kv_rank + LANES

    (xn,) = rmsnorm(h, ln0_mix, (BF16,))
    w_in0b = w_in0.astype(BF16)
    (z_uq,) = matmul(xn, w_in0b[:, :s5_w + q_rank])
    off_kv = s5_w + q_rank
    w_kpe = w_in0b[:, off_kv + kv_rank:]
    zpad = jnp.zeros((d, kvw - kv_rank - rope), BF16)
    w_kv1 = jnp.concatenate([w_in0b[:, off_kv:off_kv + kv_rank], w_kpe, zpad], axis=1)
    w_kv2 = jnp.concatenate([jnp.zeros((d, kv_rank), BF16), _swap_halves(w_kpe), zpad], axis=1)
    pos_p = jnp.arange(tp)
    pos_s = past + jnp.arange(ts)

    def token_tables(width, lead):
        cp, sp = _rope_tables(pos_p, rope, width, lead)
        cs, ss = _rope_tables(pos_s, rope, width, lead)
        return (jnp.concatenate([jnp.tile(cp, (bp, 1)), jnp.tile(cs, (bs, 1))], axis=0),
                jnp.concatenate([jnp.tile(sp, (bp, 1)), jnp.tile(ss, (bs, 1))], axis=0))

    ckv_c, ckv_s = token_tables(kvw, kv_rank)
    z_kv = rope_matmul(xn, w_kv1, w_kv2, ckv_c, ckv_s, F32, tn=kvw)
    ckv_f, ckv_b = rmsnorm(z_kv, mla_kv_norm, (F32, BF16), col_block=0, width=kv_rank)
    kpe_f = z_kv[:, kv_rank:kv_rank + rope]
    kpe_b = z_kv[:, kv_rank:].astype(BF16)

    (cqn,) = rmsnorm(z_uq, mla_q_norm, (BF16,), col_block=s5_w // q_rank, width=q_rank)
    wq = mla_w_q_up.astype(BF16)
    zq = jnp.zeros((q_rank, n_mh, qw - qk), BF16)
    wq1 = jnp.concatenate([wq, zq], axis=2).reshape(q_rank, n_mh * qw)
    wq2 = jnp.concatenate([jnp.zeros((q_rank, n_mh, nope), BF16), _swap_halves(wq[:, :, nope:]), zq],
                          axis=2).reshape(q_rank, n_mh * qw)
    q_c, q_s = token_tables(qw, nope)
    q_all = rope_matmul(cqn, wq1, wq2, q_c, q_s, BF16, tn=qw)

    s5p = s5_params(s5_a_re, s5_a_im, s5_log_dt, s5_b_re, s5_b_im, s5_c_re, s5_c_im, s5_d, s5_w_glu, s5_b_glu)
    g5, p5 = s5_a_re.shape
    wuk = jnp.transpose(mla_w_uk, (1, 2, 0)).astype(BF16)
    wuv = jnp.transpose(mla_w_uv, (1, 0, 2)).astype(BF16)

    y_s5, s5_re, s5_im, y_mla = [], [], [], []
    for si, (off, bsz, t) in enumerate(streams):
        if si == 0:
            h0r = jnp.zeros((bsz, g5 * p5), F32)
            h0i = h0r
            ckv_k = ckv_b[off:off + bsz * t].reshape(bsz, t, kv_rank)
            kpe_k = kpe_b[off:off + bsz * t].reshape(bsz, t, LANES)
            klen, causal = t, True
        else:
            h0r, h0i = state_s5_re.astype(F32), state_s5_im.astype(F32)
            klen, causal = past + t, False
            padk = -(-klen // LANES) * LANES - klen
            ckv_k = jnp.concatenate([cache_mla_ckv.astype(BF16), ckv_b[off:off + bsz * t].reshape(bsz, t, kv_rank),
                                     jnp.zeros((bsz, padk, kv_rank), BF16)], axis=1)
            kpe_cache = jnp.concatenate([cache_mla_kpe.astype(BF16),
                                         jnp.zeros((bsz, past, LANES - rope), BF16)], axis=2)
            kpe_k = jnp.concatenate([kpe_cache, kpe_b[off:off + bsz * t].reshape(bsz, t, LANES),
                                     jnp.zeros((bsz, padk, LANES), BF16)], axis=1)
        ys, hr, hi = s5_mixer(z_uq, off, bsz, t, h0r, h0i, s5p)
        y_s5.append(ys)
        s5_re.append(hr.reshape(bsz, g5, p5))
        s5_im.append(hi.reshape(bsz, g5, p5))
        y_mla.append(mla_attention(q_all, off, bsz, t, ckv_k, kpe_k, wuk, wuv, klen, causal, mla_scale))

    mix0 = jnp.concatenate([jnp.concatenate(y_s5, axis=0), jnp.concatenate(y_mla, axis=0)], axis=1)
    (h,) = matmul(mix0, w_out0.astype(BF16), res=h)
    (hn,) = rmsnorm(h, ln0_ffn, (BF16,))
    act = swiglu_up(hn, ffn_w_gate, ffn_w_up)
    (h,) = matmul(act, ffn_w_down.astype(BF16), res=h, tm=_pick(ntok, (640, 512, 256, 128, 64, 32, 16, 8)),
                  tn=256, tk=ffn_w_down.shape[0])

    rw = rwkv_w0.shape[0]
    nh_r, hd_r = rwkv_k_k.shape
    shift_w = rwkv_mu.shape[0]
    lora_w = shift_w - 3 * rw
    slab = -(-lora_w // LANES) * LANES
    wz = 3 * rw + slab
    nh_f = fox_b_f.shape[0]
    fw = (w_in1.shape[1] - shift_w - nh_f) // 3
    fox_scale = float(fw // nh_f) ** -0.5
    d_lw, d_la = rwkv_w2.shape[0], rwkv_a2.shape[0]

    (xn,) = rmsnorm(h, ln1_mix, (BF16,))
    w_in1b = w_in1.astype(BF16)
    (zr,) = matmul(xn, jnp.pad(w_in1b[:, :shift_w], ((0, 0), (0, wz - shift_w))))
    (fq,) = matmul(xn, w_in1b[:, shift_w:shift_w + fw], (BF16,))
    fk = [matmul(xn, w_in1b[:, shift_w + fw:shift_w + 2 * fw], (F32, BF16), rows=(o_, b_ * t_)) for o_, b_, t_ in streams]
    fv = [matmul(xn, w_in1b[:, shift_w + 2 * fw:shift_w + 3 * fw], (F32, BF16), rows=(o_, b_ * t_))
          for o_, b_, t_ in streams]
    (zf,) = matmul(xn, jnp.pad(w_in1b[:, shift_w + 3 * fw:], ((0, 0), (0, LANES - nh_f))))

    padrow = lambda wgt, lo: jnp.zeros((slab, rw), BF16).at[lo:lo + wgt.shape[0]].set(wgt.astype(BF16))
    row2 = lambda x_, n_: x_.astype(F32).reshape(1, n_)
    head3 = lambda x_: x_.astype(F32).reshape(nh_r, 1, hd_r)
    rprm = dict(mu=jnp.pad(row2(rwkv_mu, shift_w), ((0, 0), (0, wz - shift_w))), w0=row2(rwkv_w0, rw),
                a0=row2(rwkv_a0, rw), w2=padrow(rwkv_w2, 0), a2=padrow(rwkv_a2, d_lw),
                g2=padrow(rwkv_g2, d_lw + d_la), k_k=head3(rwkv_k_k), k_a=head3(rwkv_k_a), r_k=head3(rwkv_r_k),
                ln_w=head3(rwkv_ln_w), ln_b=head3(rwkv_ln_b))

    y_r, wkv, shift_new, y_f, logf_out = [], [], [], [], []
    for si, (off, bsz, t) in enumerate(streams):
        if si == 0:
            shift_prev = jnp.zeros((bsz, wz), F32)
            s0 = jnp.zeros((bsz, nh_r, hd_r, hd_r), F32)
            pre = None
            k_all, v_all, tkeys = fk[si][1], fv[si][1], t
            qoff = 0
        else:
            shift_prev = jnp.pad(state_rwkv_shift.astype(F32), ((0, 0), (0, wz - shift_w)))
            s0 = state_rwkv_wkv.astype(F32)
            pre = jnp.pad(cache_fox_logf.astype(F32), ((0, 0), (0, 0), (0, LANES - nh_f)))
            qoff = cache_fox_k.shape[1]
            tkeys = -(-(qoff + t) // LANES) * LANES
            cat = lambda cache, new: jnp.concatenate(
                [cache.reshape(bsz, qoff, fw).astype(BF16), new.reshape(bsz, t, fw),
                 jnp.zeros((bsz, tkeys - qoff - t, fw), BF16)], axis=1).reshape(bsz * tkeys, fw)
            k_all, v_all = cat(cache_fox_k, fk[si][1]), cat(cache_fox_v, fv[si][1])
        parts = rwkv_prep(zr, off, bsz, t, shift_prev, rprm, nh_r)
        yr, s_fin = rwkv_scan(*parts, s0, rprm)
        y_r.append(yr)
        wkv.append(s_fin)
        shift_new.append(zr[off:off + bsz * t].reshape(bsz, t, wz)[:, -1, :shift_w])
        logf, cum, cumt = fox_gate(zf, off, bsz, t, fox_b_f, pre)
        logf_out.append(logf[:, :, :nh_f])
        y_f.append(fox_attention(fq, off, bsz, t, k_all, v_all, tkeys, cum, cumt, qoff, fox_scale))

    mix1 = jnp.concatenate([jnp.concatenate(y_r, axis=0), jnp.concatenate(y_f, axis=0)], axis=1)
    (h,) = matmul(mix1, w_out1.astype(BF16), res=h)
    (hn,) = rmsnorm(h, ln1_ffn, (F32,))
    rout = moe_router(hn, moe_w_router)
    n_exp = moe_w_gate.shape[0]
    slot, tok, te, used, n_tiles = _moe_plan(rout, n_exp, MOE_TILE)
    xs = moe_gather(hn, tok, used, n_tiles, MOE_TILE)
    act = moe_up(xs, moe_w_gate, moe_w_up, te, used, MOE_TILE)
    ys = moe_down(act, moe_w_down.astype(BF16), te, used, MOE_TILE)
    nfh = fw // nh_f
    outs = [moe_combine_norm(h, rout, ys, slot, final_norm, off, bsz * t).reshape(bsz, t, d) for off, bsz, t in streams]
    for si, (off, bsz, t) in enumerate(streams):
        rows = slice(off, off + bsz * t)
        outs += [ckv_f[rows].reshape(bsz, t, kv_rank), kpe_f[rows].reshape(bsz, t, rope), s5_re[si], s5_im[si],
                 wkv[si], shift_new[si], fk[si][0].reshape(bsz, t, nh_f, nfh), fv[si][0].reshape(bsz, t, nh_f, nfh),
                 logf_out[si]]
    return tuple(outs)
```

```python
import functools
import math

import jax
import jax.numpy as jnp
from jax import lax
from jax.experimental import pallas as pl
from jax.experimental.pallas import tpu as pltpu

F32 = jnp.float32
BF16 = jnp.bfloat16

V7X_VMEM_BYTES = 64 * 1024 * 1024
VMEM_CAP = V7X_VMEM_BYTES - 4 * 1024 * 1024
LANES = 128

NORM_EPS = 1e-6
NEG_INF = -1e30
CHUNK = 64
ROPE_THETA = 10000.0
RWKV_LN_EPS = 64e-5
S5_BLOCK_GROUPS = 8


def _pick(n, cands):
    for c in cands:
        if c <= n and n % c == 0:
            return c
    return n


def _cparams(sem, vmem_bytes):
    limit = int(min(max(vmem_bytes * 1.25 + (4 << 20), 24 << 20), VMEM_CAP))
    return pltpu.CompilerParams(dimension_semantics=sem, vmem_limit_bytes=limit)


def _rmsnorm_kernel(x_ref, g_ref, *o_refs):
    x = x_ref[...].astype(F32)
    y = x * lax.rsqrt(jnp.mean(x * x, axis=-1, keepdims=True) + NORM_EPS)
    y = y * g_ref[...]
    for o in o_refs:
        o[...] = y.astype(o.dtype)


def rmsnorm(x, g, out_dtypes, col_block=0, width=None):
    m = x.shape[0]
    width = x.shape[1] if width is None else width
    tm = _pick(m, (512, 320, 256, 128, 64, 32, 16, 8))
    outs = tuple(jax.ShapeDtypeStruct((m, width), d) for d in out_dtypes)
    res = pl.pallas_call(
        _rmsnorm_kernel,
        grid=(m // tm,),
        in_specs=[pl.BlockSpec((tm, width), lambda i: (i, col_block)),
                  pl.BlockSpec((1, width), lambda i: (0, 0))],
        out_specs=tuple(pl.BlockSpec((tm, width), lambda i: (i, 0)) for _ in out_dtypes),
        out_shape=outs,
        compiler_params=_cparams(("parallel",), tm * width * 4 * 2 * (1 + len(out_dtypes))),
        name="rmsnorm",
    )(x, g.reshape(1, width).astype(F32))
    return res


def _mm_kernel(*refs, nk, has_res, has_scale, n_out):
    x_ref, w_ref = refs[0], refs[1]
    pos = 2
    res_ref = scale_ref = None
    if has_res:
        res_ref = refs[pos]
        pos += 1
    if has_scale:
        scale_ref = refs[pos]
        pos += 1
    o_refs = refs[pos:pos + n_out]
    acc_ref = refs[pos + n_out] if nk > 1 else None

    part = jnp.dot(x_ref[...].astype(BF16), w_ref[...].astype(BF16), preferred_element_type=F32)

    def finish(acc):
        if has_scale:
            acc = acc * scale_ref[...]
        if has_res:
            acc = res_ref[...] + acc
        for o in o_refs:
            o[...] = acc.astype(o.dtype)

    if nk == 1:
        finish(part)
    else:
        k = pl.program_id(2)

        @pl.when(k == 0)
        def _():
            acc_ref[...] = part

        @pl.when(k > 0)
        def _():
            acc_ref[...] += part

        @pl.when(k == nk - 1)
        def _():
            finish(acc_ref[...])


def matmul(x, w, out_dtypes=(F32,), res=None, scale=None, w_index=None, tm=None, tn=None, tk=None, rows=None):
    row0, m = rows if rows is not None else (0, x.shape[0])
    kdim = x.shape[1]
    n = w.shape[-1]
    tm = tm or _pick(math.gcd(m, row0) if row0 else m, (1280, 1024, 640, 512, 320, 256, 128, 64, 32, 16, 8))
    tn = tn or _pick(n, (512, 384, 256, 128))
    tk = tk or (kdim if kdim <= 4096 else _pick(kdim, (2048, 1792, 1024, 512, 256, 128)))
    nk = kdim // tk
    grid = (m // tm, n // tn, nk)
    rb0 = row0 // tm
    in_specs = [pl.BlockSpec((tm, tk), lambda i, j, k: (rb0 + i, k))]
    if w.ndim == 3:
        in_specs.append(pl.BlockSpec((None, tk, tn), lambda i, j, k: (w_index, k, j)))
    else:
        in_specs.append(pl.BlockSpec((tk, tn), lambda i, j, k: (k, j)))
    args = [x, w]
    if res is not None:
        in_specs.append(pl.BlockSpec((tm, tn), lambda i, j, k: (i, j)))
        args.append(res)
    if scale is not None:
        in_specs.append(pl.BlockSpec((tm, 1), lambda i, j, k: (i, 0)))
        args.append(scale)
    out_specs = tuple(pl.BlockSpec((tm, tn), lambda i, j, k: (i, j)) for _ in out_dtypes)
    out_shape = tuple(jax.ShapeDtypeStruct((m, n), d) for d in out_dtypes)
    scratch = [pltpu.VMEM((tm, tn), F32)] if nk > 1 else []
    vmem = (2 * tm * tk * x.dtype.itemsize + 2 * tk * tn * w.dtype.itemsize
            + tm * tn * 4 * (2 * len(out_dtypes) + 1 + (2 if res is not None else 0)))
    outs = pl.pallas_call(
        functools.partial(_mm_kernel, nk=nk, has_res=res is not None, has_scale=scale is not None,
                          n_out=len(out_dtypes)),
        grid=grid, in_specs=in_specs, out_specs=out_specs, out_shape=out_shape,
        scratch_shapes=scratch,
        compiler_params=_cparams(("parallel", "parallel", "arbitrary"), vmem),
        name="matmul",
    )(*args)
    return outs


def _mm_heads_kernel(x_ref, w_ref, o3_ref, ob_ref):
    acc = jnp.dot(x_ref[...], w_ref[...], preferred_element_type=F32)
    ob_ref[...] = acc.astype(ob_ref.dtype)
    hd = o3_ref.shape[2]
    for hh in range(o3_ref.shape[1]):
        o3_ref[:, hh, :] = acc[:, hh * hd:(hh + 1) * hd]


def matmul_heads(x, w, nheads, rows):
    row0, m = rows
    kdim, n = w.shape
    hd = n // nheads
    tm = _pick(math.gcd(m, row0) if row0 else m, (256, 128, 64, 32, 16, 8))
    rb0 = row0 // tm
    vmem = 2 * tm * kdim * 2 + 2 * kdim * n * 2 + 8 * tm * n * 4
    return pl.pallas_call(
        _mm_heads_kernel,
        grid=(m // tm,),
        in_specs=[pl.BlockSpec((tm, kdim), lambda i: (rb0 + i, 0)), pl.BlockSpec((kdim, n), lambda i: (0, 0))],
        out_specs=(pl.BlockSpec((tm, nheads, hd), lambda i: (i, 0, 0)), pl.BlockSpec((tm, n), lambda i: (i, 0))),
        out_shape=(jax.ShapeDtypeStruct((m, nheads, hd), F32), jax.ShapeDtypeStruct((m, n), BF16)),
        compiler_params=_cparams(("parallel",), vmem),
        name="matmul_heads",
    )(x, w)


def _rope_mm_kernel(x_ref, w1_ref, w2_ref, c_ref, s_ref, o_ref):
    x = x_ref[...]
    a = jnp.dot(x, w1_ref[...], preferred_element_type=F32)
    b = jnp.dot(x, w2_ref[...], preferred_element_type=F32)
    o_ref[...] = (a * c_ref[...] + b * s_ref[...]).astype(o_ref.dtype)


def rope_matmul(x, w1, w2, ctab, stab, out_dtype, tn):
    m, kdim = x.shape
    n = w1.shape[1]
    est = lambda rows: 2 * rows * kdim * 2 + 4 * kdim * tn * 2 + 8 * rows * tn * 4
    tm = next((c for c in (1280, 1024, 640, 512, 320, 256, 128, 64, 32, 16, 8)
               if m % c == 0 and est(c) <= VMEM_CAP // 2), 8)
    vmem = est(tm)
    return pl.pallas_call(
        _rope_mm_kernel,
        grid=(m // tm, n // tn),
        in_specs=[pl.BlockSpec((tm, kdim), lambda i, j: (i, 0)),
                  pl.BlockSpec((kdim, tn), lambda i, j: (0, j)),
                  pl.BlockSpec((kdim, tn), lambda i, j: (0, j)),
                  pl.BlockSpec((tm, tn), lambda i, j: (i, 0)),
                  pl.BlockSpec((tm, tn), lambda i, j: (i, 0))],
        out_specs=pl.BlockSpec((tm, tn), lambda i, j: (i, j)),
        out_shape=jax.ShapeDtypeStruct((m, n), out_dtype),
        compiler_params=_cparams(("parallel", "parallel"), vmem),
        name="rope_matmul",
    )(x, w1, w2, ctab, stab)


def _swiglu_up_kernel(x_ref, wg_ref, wu_ref, o_ref):
    x = x_ref[...]
    g = jnp.dot(x, wg_ref[...].astype(BF16), preferred_element_type=F32)
    u = jnp.dot(x, wu_ref[...].astype(BF16), preferred_element_type=F32)
    o_ref[...] = (g * jax.nn.sigmoid(g) * u).astype(o_ref.dtype)


def swiglu_up(x, wg, wu, w_index=None):
    m, kdim = x.shape
    n = wg.shape[-1]
    tm = _pick(m, (1280, 1024, 640, 512, 320, 256, 128, 64, 32, 16, 8))
    tn = _pick(n, (256, 128))
    if wg.ndim == 3:
        wspec = pl.BlockSpec((None, kdim, tn), lambda i, j: (w_index, 0, j))
    else:
        wspec = pl.BlockSpec((kdim, tn), lambda i, j: (0, j))
    vmem = 2 * tm * kdim * 2 + 4 * kdim * tn * wg.dtype.itemsize + 6 * tm * tn * 4
    return pl.pallas_call(
        _swiglu_up_kernel,
        grid=(m // tm, n // tn),
        in_specs=[pl.BlockSpec((tm, kdim), lambda i, j: (i, 0)), wspec, wspec],
        out_specs=pl.BlockSpec((tm, tn), lambda i, j: (i, j)),
        out_shape=jax.ShapeDtypeStruct((m, n), BF16),
        compiler_params=_cparams(("parallel", "parallel"), vmem),
        name="swiglu_up",
    )(x, wg, wu)


def _s5_kernel(u_ref, h0r_ref, h0i_ref, lr_ref, li_ref, bbr_ref, bbi_ref, ccr_ref, cci_ref,
               d_ref, wglu_ref, bglu_ref, y_ref, hr_ref, hi_ref, xr_s, xi_s, st_r, st_i, *, tc, nblk):
    c = pl.program_id(1)

    @pl.when(c == 0)
    def _():
        st_r[...] = h0r_ref[0]
        st_i[...] = h0i_ref[0]

    u = u_ref[...]
    ub = u.astype(BF16)
    sw = xr_s.shape[1] // nblk
    for k in range(nblk):
        uk = ub[:, k * LANES:(k + 1) * LANES]
        xr_s[:, k * sw:(k + 1) * sw] = jnp.dot(uk, bbr_ref[k], preferred_element_type=F32)
        xi_s[:, k * sw:(k + 1) * sw] = jnp.dot(uk, bbi_ref[k], preferred_element_type=F32)

    scan_w = 1024
    for q in range(xr_s.shape[1] // scan_w):
        cols = slice(q * scan_w, (q + 1) * scan_w)
        lr = lr_ref[:, cols]
        li = li_ref[:, cols]

        def body(t, carry, cols=cols, lr=lr, li=li):
            hr, hi = carry
            nr = lr * hr - li * hi + xr_s[pl.ds(t, 1), cols]
            ni = lr * hi + li * hr + xi_s[pl.ds(t, 1), cols]
            xr_s[pl.ds(t, 1), cols] = nr
            xi_s[pl.ds(t, 1), cols] = ni
            return nr, ni

        hr, hi = lax.fori_loop(0, tc, body, (st_r[:, cols], st_i[:, cols]))
        st_r[:, cols] = hr
        st_i[:, cols] = hi

    hr_ref[0] = st_r[...]
    hi_ref[0] = st_i[...]

    ys = []
    for k in range(nblk):
        xr = xr_s[:, k * sw:(k + 1) * sw].astype(BF16)
        xi = xi_s[:, k * sw:(k + 1) * sw].astype(BF16)
        ys.append(jnp.dot(xr, ccr_ref[k], preferred_element_type=F32)
                  - jnp.dot(xi, cci_ref[k], preferred_element_type=F32))
    y = jnp.concatenate(ys, axis=1) + d_ref[...] * u
    y = jax.nn.gelu(y)
    gate = jax.nn.sigmoid(jnp.dot(y.astype(BF16), wglu_ref[...], preferred_element_type=F32) + bglu_ref[...])
    y_ref[...] = (y * gate).astype(y_ref.dtype)


def s5_mixer(z, row_off, bsz, t, h0_re, h0_im, prm):
    width = prm["d"].shape[1]
    nblk = width // LANES
    nstate = prm["lr"].shape[1]
    tc = _pick(t, (256, 128, 64, 32, 16, 8))
    nt = t // tc
    rb0 = row_off // tc
    full = lambda shape: pl.BlockSpec(shape, lambda b, c: (0,) * len(shape))
    vmem = (4 * tc * width * 4 + 2 * tc * nstate * 4 + 4 * nblk * LANES * (nstate // nblk) * 2 * 2
            + 2 * width * width * 2 + 8 * tc * width * 4)
    y, hr, hi = pl.pallas_call(
        functools.partial(_s5_kernel, tc=tc, nblk=nblk),
        grid=(bsz, nt),
        in_specs=[pl.BlockSpec((tc, width), lambda b, c: (rb0 + b * nt + c, 0)),
                  pl.BlockSpec((1, 1, nstate), lambda b, c: (b, 0, 0)),
                  pl.BlockSpec((1, 1, nstate), lambda b, c: (b, 0, 0)),
                  full((1, nstate)), full((1, nstate)),
                  full(prm["bbr"].shape), full(prm["bbi"].shape),
                  full(prm["ccr"].shape), full(prm["cci"].shape),
                  full((1, width)), full((width, width)), full((1, width))],
        out_specs=(pl.BlockSpec((tc, width), lambda b, c: (b * nt + c, 0)),
                   pl.BlockSpec((1, 1, nstate), lambda b, c: (b, 0, 0)),
                   pl.BlockSpec((1, 1, nstate), lambda b, c: (b, 0, 0))),
        out_shape=(jax.ShapeDtypeStruct((bsz * t, width), BF16),
                   jax.ShapeDtypeStruct((bsz, 1, nstate), F32),
                   jax.ShapeDtypeStruct((bsz, 1, nstate), F32)),
        scratch_shapes=[pltpu.VMEM((tc, nstate), F32), pltpu.VMEM((tc, nstate), F32),
                        pltpu.VMEM((1, nstate), F32), pltpu.VMEM((1, nstate), F32)],
        compiler_params=_cparams(("parallel", "arbitrary"), vmem),
        name="s5_mixer",
    )(z, h0_re.reshape(bsz, 1, nstate), h0_im.reshape(bsz, 1, nstate), prm["lr"], prm["li"],
      prm["bbr"], prm["bbi"], prm["ccr"], prm["cci"], prm["d"], prm["wglu"], prm["bglu"])
    return y, hr, hi


def s5_params(a_re, a_im, log_dt, b_re, b_im, c_re, c_im, d_skip, w_glu, b_glu):
    g, p = a_re.shape
    nch = b_re.shape[2]
    dt = jnp.exp(log_dt.astype(F32))[:, None]
    ar, ai = a_re.astype(F32), a_im.astype(F32)
    mag = jnp.exp(ar * dt)
    lr = mag * jnp.cos(ai * dt)
    li = mag * jnp.sin(ai * dt)
    den = ar * ar + ai * ai
    fr = ((lr - 1.0) * ar + li * ai) / den
    fi = (li * ar - (lr - 1.0) * ai) / den
    br, bi = b_re.astype(F32), b_im.astype(F32)
    bbr = fr[..., None] * br - fi[..., None] * bi
    bbi = fr[..., None] * bi + fi[..., None] * br
    gb = S5_BLOCK_GROUPS
    nblk = g // gb
    eye = jnp.eye(gb, dtype=F32)

    def blk_in(m):
        m = m.reshape(nblk, gb, p, nch)
        return jnp.einsum("kgpn,gh->kgnhp", m, eye).reshape(nblk, gb * nch, gb * p).astype(BF16)

    def blk_out(m):
        m = m.astype(F32).reshape(nblk, gb, nch, p)
        return jnp.einsum("kgnp,gh->kgphn", m, eye).reshape(nblk, gb * p, gb * nch).astype(BF16)

    width = g * nch
    return dict(lr=lr.reshape(1, g * p), li=li.reshape(1, g * p), bbr=blk_in(bbr), bbi=blk_in(bbi),
                ccr=blk_out(c_re), cci=blk_out(c_im), d=d_skip.astype(F32).reshape(1, width),
                wglu=w_glu.astype(BF16), bglu=b_glu.astype(F32).reshape(1, width))


MLA_ROWS = 1024
MLA_TQ = 512
MLA_CHAIN_ROWS = 256


def _mla_kernel(q_ref, ckv_ref, kpe_ref, wuk_ref, wuv_ref, o_ref, qa_s, qpe_s, m_s, l_s, acc_s,
                *, tq, tk, nk_total, klen, causal, scale, hps):
    i = pl.program_id(1)
    nope = wuk_ref.shape[1]
    qw = q_ref.shape[1] // hps
    vdim = wuv_ref.shape[2]
    rows_all = hps * tq
    rc = MLA_CHAIN_ROWS if rows_all % MLA_CHAIN_ROWS == 0 else rows_all
    for hh in range(hps):
        rows = slice(hh * tq, (hh + 1) * tq)
        qa = jnp.dot(q_ref[:, hh * qw:hh * qw + nope], wuk_ref[hh], preferred_element_type=F32)
        qa_s[rows, :] = qa.astype(qa_s.dtype)
        qpe_s[rows, :] = q_ref[:, hh * qw + nope:(hh + 1) * qw]
    m_s[...] = jnp.full(m_s.shape, NEG_INF, F32)
    l_s[...] = jnp.zeros(l_s.shape, F32)
    acc_s[...] = jnp.zeros(acc_s.shape, F32)

    def tile(j, masked):
        ks = pl.multiple_of(j * tk, tk)
        ckv = ckv_ref[0, pl.ds(ks, tk), :]
        kpe = kpe_ref[0, pl.ds(ks, tk), :]
        scores = []
        for c0 in range(0, rows_all, rc):
            rows = slice(c0, c0 + rc)
            s = lax.dot_general(qa_s[rows, :].astype(BF16), ckv, (((1,), (1,)), ((), ())), preferred_element_type=F32)
            s = s + lax.dot_general(qpe_s[rows, :], kpe, (((1,), (1,)), ((), ())), preferred_element_type=F32)
            scores.append(s * scale)
        for c0, s in zip(range(0, rows_all, rc), scores):
            rows = slice(c0, c0 + rc)
            if masked:
                kpos = ks + lax.broadcasted_iota(jnp.int32, (rc, tk), 1)
                if causal:
                    qpos = i * tq + (c0 + lax.broadcasted_iota(jnp.int32, (rc, tk), 0)) % tq
                    s = jnp.where(kpos // CHUNK <= qpos // CHUNK, s, NEG_INF)
                if klen < nk_total * tk:
                    s = jnp.where(kpos < klen, s, NEG_INF)
            m_prev = m_s[rows, :]
            m_new = jnp.maximum(m_prev, jnp.max(s, axis=1, keepdims=True))
            alpha = jnp.exp(m_prev - m_new)
            p = jnp.exp(s - m_new)
            l_s[rows, :] = alpha * l_s[rows, :] + jnp.sum(p, axis=1, keepdims=True)
            acc_s[rows, :] = alpha * acc_s[rows, :] + jnp.dot(p.astype(BF16), ckv, preferred_element_type=F32)
            m_s[rows, :] = m_new

    def full_tile(j, c):
        tile(j, False)
        return c

    n_full = i if causal else nk_total - 1
    lax.fori_loop(0, n_full, full_tile, 0)
    tile(n_full, True)
    o_lat = (acc_s[...] / l_s[...]).astype(BF16)
    for hh in range(hps):
        o_ref[:, hh * vdim:(hh + 1) * vdim] = jnp.dot(
            o_lat[hh * tq:(hh + 1) * tq], wuv_ref[hh], preferred_element_type=F32).astype(o_ref.dtype)


def mla_attention(q_all, row_off, bsz, t, ckv, kpe, wuk, wuv, klen, causal, scale):
    nheads, nope, lat = wuk.shape
    vdim = wuv.shape[2]
    qw = q_all.shape[1] // nheads
    tkeys = ckv.shape[1]
    tq = _pick(t, (MLA_TQ, 256, 128, 64, 32, 16, 8))
    tk = tq if causal else _pick(tkeys, (256, 128))
    if causal:
        assert tq % CHUNK == 0 and tkeys == t
    hps = _pick(nheads, tuple(c for c in (24, 16, 12, 8, 6, 4, 3, 2, 1) if c * tq <= MLA_ROWS))
    nq = t // tq
    rb0 = row_off // tq
    rows_all = hps * tq
    vmem = (2 * tkeys * (lat + LANES) * 2 + 4 * tq * hps * qw * 2 + 3 * rows_all * lat * 4 + 8 * rows_all * tk * 4
            + 4 * hps * (nope + vdim) * lat * 2 + (2 << 20))
    return pl.pallas_call(
        functools.partial(_mla_kernel, tq=tq, tk=tk, nk_total=tkeys // tk, klen=klen, causal=causal, scale=scale,
                          hps=hps),
        grid=(bsz, nq, nheads // hps),
        in_specs=[pl.BlockSpec((tq, hps * qw), lambda b, i, h: (rb0 + b * nq + i, h)),
                  pl.BlockSpec((1, tkeys, lat), lambda b, i, h: (b, 0, 0)),
                  pl.BlockSpec((1, tkeys, LANES), lambda b, i, h: (b, 0, 0)),
                  pl.BlockSpec((hps, nope, lat), lambda b, i, h: (h, 0, 0)),
                  pl.BlockSpec((hps, lat, vdim), lambda b, i, h: (h, 0, 0))],
        out_specs=pl.BlockSpec((tq, hps * vdim), lambda b, i, h: (b * nq + i, h)),
        out_shape=jax.ShapeDtypeStruct((bsz * t, nheads * vdim), BF16),
        scratch_shapes=[pltpu.VMEM((rows_all, lat), F32), pltpu.VMEM((rows_all, LANES), BF16),
                        pltpu.VMEM((rows_all, 1), F32), pltpu.VMEM((rows_all, 1), F32),
                        pltpu.VMEM((rows_all, lat), F32)],
        compiler_params=_cparams(("parallel", "parallel", "arbitrary"), vmem),
        name="mla_attention",
    )(q_all, ckv, kpe, wuk, wuv)


def _fox_gate_kernel(pre_ref, zf_ref, bf_ref, logf_ref, cum_ref, cumt_ref, lf_s, *, npre, t, blk):
    total = lf_s.shape[0]
    z = zf_ref[0] + bf_ref[...]
    logf = jnp.minimum(z, 0.0) - jnp.log1p(jnp.exp(-jnp.abs(z)))
    logf_ref[0] = logf
    if npre + t < total:
        lf_s[...] = jnp.zeros(lf_s.shape, F32)
    if npre:
        lf_s[0:npre, :] = pre_ref[0]
    lf_s[npre:npre + t, :] = logf
    tri = (lax.broadcasted_iota(jnp.int32, (blk, blk), 1)
           <= lax.broadcasted_iota(jnp.int32, (blk, blk), 0)).astype(F32)
    carry = jnp.zeros((1, LANES), F32)
    for c in range(total // blk):
        rows = slice(c * blk, (c + 1) * blk)
        cum = jnp.dot(tri, lf_s[rows, :], preferred_element_type=F32, precision=lax.Precision.HIGHEST) + carry
        cum_ref[0, rows, :] = cum
        cumt_ref[0, c] = cum.T[:cumt_ref.shape[2], :]
        carry = cum[blk - 1:blk, :]


def fox_gate(zf, row_off, bsz, t, b_f, pre):
    nheads = b_f.shape[0]
    npre = 0 if pre is None else pre.shape[1]
    total = -(-(npre + t) // LANES) * LANES
    if pre is None:
        pre = jnp.zeros((bsz, 8, LANES), F32)
    pp = pre.shape[1]
    hrows = -(-nheads // 8) * 8
    zf3 = zf[row_off:row_off + bsz * t].reshape(bsz, t, LANES)
    bfp = jnp.zeros((1, LANES), F32).at[0, :nheads].set(b_f.astype(F32))
    return pl.pallas_call(
        functools.partial(_fox_gate_kernel, npre=npre, t=t, blk=LANES),
        grid=(bsz,),
        in_specs=[pl.BlockSpec((1, pp, LANES), lambda b: (b, 0, 0)),
                  pl.BlockSpec((1, t, LANES), lambda b: (b, 0, 0)),
                  pl.BlockSpec((1, LANES), lambda b: (0, 0))],
        out_specs=(pl.BlockSpec((1, t, LANES), lambda b: (b, 0, 0)),
                   pl.BlockSpec((1, total, LANES), lambda b: (b, 0, 0)),
                   pl.BlockSpec((1, total // LANES, hrows, LANES), lambda b: (b, 0, 0, 0))),
        out_shape=(jax.ShapeDtypeStruct((bsz, t, LANES), F32),
                   jax.ShapeDtypeStruct((bsz, total, LANES), F32),
                   jax.ShapeDtypeStruct((bsz, total // LANES, hrows, LANES), F32)),
        scratch_shapes=[pltpu.VMEM((total, LANES), F32)],
        compiler_params=_cparams(("parallel",), 12 * total * LANES * 4),
        name="fox_gate",
    )(pre, zf3, bfp)


FOX_HEADS_PER_STEP = 4
FOX_TQ = 512


def _fox_attn_kernel(q_ref, k_ref, v_ref, cq_ref, ck_ref, o_ref, m_s, l_s, acc_s,
                     *, tq, tk, qoff, scale, hps):
    hg = pl.program_id(1)
    i = pl.program_id(2)
    hd = acc_s.shape[2]
    lane = lax.broadcasted_iota(jnp.int32, cq_ref.shape[1:], 1)
    cq = cq_ref[0]
    fq = [jnp.sum(jnp.where(lane == hg * hps + hh, cq, 0.0), axis=1, keepdims=True) for hh in range(hps)]
    m_s[...] = jnp.full(m_s.shape, NEG_INF, F32)
    l_s[...] = jnp.zeros(l_s.shape, F32)
    acc_s[...] = jnp.zeros(acc_s.shape, F32)

    def tile(j, masked):
        ks = pl.multiple_of(j * tk, tk)
        scores = []
        for hh in range(hps):
            cols = slice(hh * hd, (hh + 1) * hd)
            k = k_ref[pl.ds(ks, tk), cols]
            fk = jnp.concatenate([ck_ref[0, j * (tk // LANES) + c, pl.ds(hg * hps + hh, 1), :]
                                  for c in range(tk // LANES)], axis=1)
            s = lax.dot_general(q_ref[:, cols], k, (((1,), (1,)), ((), ())), preferred_element_type=F32) * scale
            scores.append(s + fq[hh] - fk)
        for hh, s in enumerate(scores):
            cols = slice(hh * hd, (hh + 1) * hd)
            if masked:
                qpos = qoff + i * tq + lax.broadcasted_iota(jnp.int32, (tq, tk), 0)
                kpos = ks + lax.broadcasted_iota(jnp.int32, (tq, tk), 1)
                s = jnp.where(kpos <= qpos, s, NEG_INF)
            m_prev = m_s[hh]
            m_new = jnp.maximum(m_prev, jnp.max(s, axis=1, keepdims=True))
            alpha = jnp.exp(m_prev - m_new)
            p = jnp.exp(s - m_new)
            l_s[hh] = alpha * l_s[hh] + jnp.sum(p, axis=1, keepdims=True)
            v = v_ref[pl.ds(ks, tk), cols]
            acc_s[hh] = alpha * acc_s[hh] + jnp.dot(p.astype(BF16), v, preferred_element_type=F32)
            m_s[hh] = m_new

    def full_tile(j, c):
        tile(j, False)
        return c

    n_full = (qoff + i * tq) // tk
    lax.fori_loop(0, n_full, full_tile, 0)
    tile(n_full, True)
    for hh in range(hps):
        o_ref[:, hh * hd:(hh + 1) * hd] = (acc_s[hh] / l_s[hh]).astype(o_ref.dtype)


def fox_attention(q_all, row_off, bsz, t, k, v, tkeys, cum, cumt, qoff, scale):
    hd = LANES
    nheads = q_all.shape[1] // hd
    hps = _pick(nheads, (FOX_HEADS_PER_STEP, 2, 1))
    tq = _pick(t, (FOX_TQ, 256, 128, 64, 32, 16, 8))
    tk = tkeys if tq * tkeys * 4 <= (256 << 10) else _pick(tkeys, (tq, 256, 128))
    assert tk % tq == 0 and qoff % tq == 0 and qoff + t <= tkeys
    nq = t // tq
    rb0 = row_off // tq
    cq0 = qoff // tq
    vmem = (8 * tkeys * hps * hd * 2 + 2 * cumt.shape[2] * tkeys * 4 + 4 * hps * tq * tk * 4
            + 8 * hps * tq * hd * 4 + (2 << 20))
    return pl.pallas_call(
        functools.partial(_fox_attn_kernel, tq=tq, tk=tk, qoff=qoff, scale=scale, hps=hps),
        grid=(bsz, nheads // hps, nq),
        in_specs=[pl.BlockSpec((tq, hps * hd), lambda b, h, i: (rb0 + b * nq + i, h)),
                  pl.BlockSpec((tkeys, hps * hd), lambda b, h, i: (b, h)),
                  pl.BlockSpec((tkeys, hps * hd), lambda b, h, i: (b, h)),
                  pl.BlockSpec((1, tq, LANES), lambda b, h, i: (b, cq0 + i, 0)),
                  pl.BlockSpec((1,) + cumt.shape[1:], lambda b, h, i: (b, 0, 0, 0))],
        out_specs=pl.BlockSpec((tq, hps * hd), lambda b, h, i: (b * nq + i, h)),
        out_shape=jax.ShapeDtypeStruct((bsz * t, nheads * hd), BF16),
        scratch_shapes=[pltpu.VMEM((hps, tq, 1), F32), pltpu.VMEM((hps, tq, 1), F32), pltpu.VMEM((hps, tq, hd), F32)],
        compiler_params=_cparams(("parallel", "parallel", "arbitrary"), vmem),
        name="fox_attention",
    )(q_all, k, v, cum, cumt)


def _rwkv_prep_kernel(z_ref, zp_ref, sh_ref, mu_ref, w0_ref, a0_ref, w2_ref, a2_ref, g2_ref,
                      r_ref, k_ref, v_ref, lw_ref, a_ref, g_ref, *, w):
    i = pl.program_id(1)
    z = z_ref[...]
    prev_row = jnp.where(i == 0, sh_ref[0], zp_ref[7:8, :])
    row = lax.broadcasted_iota(jnp.int32, z.shape, 0)
    z_prev = jnp.where(row == 0, prev_row, pltpu.roll(z, 1, 0))
    zm = z + (z_prev - z) * mu_ref[...]
    slab = zm[:, 3 * w:]
    lora_w = jnp.dot(jnp.tanh(slab).astype(BF16), w2_ref[...], preferred_element_type=F32)
    x = -(w0_ref[...] + lora_w)
    softplus = jnp.maximum(x, 0.0) + jnp.log1p(jnp.exp(-jnp.abs(x)))
    lw = -jnp.exp(-softplus - 0.5)
    a = jax.nn.sigmoid(a0_ref[...] + jnp.dot(slab.astype(BF16), a2_ref[...], preferred_element_type=F32))
    g = jnp.dot(jax.nn.sigmoid(slab).astype(BF16), g2_ref[...], preferred_element_type=F32)
    nh, hd = r_ref.shape[1], r_ref.shape[3]
    for ref, val in ((r_ref, zm[:, 0:w]), (k_ref, zm[:, w:2 * w]), (v_ref, zm[:, 2 * w:3 * w]),
                     (lw_ref, lw), (a_ref, a), (g_ref, g)):
        for hh in range(nh):
            ref[0, hh] = val[:, hh * hd:(hh + 1) * hd]


def rwkv_prep(zr, row_off, bsz, t, shift_prev, prm, nh):
    wz = zr.shape[1]
    w = prm["w0"].shape[1]
    hd = w // nh
    tt = _pick(t, (128, 64, 32, 16, 8))
    nt = t // tt
    rb0 = row_off // tt
    full = lambda shape: pl.BlockSpec(shape, lambda b, i: (0,) * len(shape))
    ospec = pl.BlockSpec((1, nh, tt, hd), lambda b, i: (b, 0, i, 0))
    oshape = jax.ShapeDtypeStruct((bsz, nh, t, hd), F32)
    ls = wz - 3 * w
    vmem = 6 * tt * wz * 4 + 12 * tt * nh * LANES * 4 + 6 * ls * w * 2 + 8 * tt * w * 4
    return pl.pallas_call(
        functools.partial(_rwkv_prep_kernel, w=w),
        grid=(bsz, nt),
        in_specs=[pl.BlockSpec((tt, wz), lambda b, i: (rb0 + b * nt + i, 0)),
                  pl.BlockSpec((8, wz), lambda b, i: (jnp.maximum((row_off + (b * nt + i) * tt) // 8 - 1, 0), 0)),
                  pl.BlockSpec((1, 1, wz), lambda b, i: (b, 0, 0)),
                  full((1, wz)), full((1, w)), full((1, w)), full((ls, w)), full((ls, w)), full((ls, w))],
        out_specs=(ospec,) * 6,
        out_shape=(oshape,) * 6,
        compiler_params=_cparams(("parallel", "arbitrary"), vmem),
        name="rwkv_prep",
    )(zr, zr, shift_prev.reshape(bsz, 1, wz), prm["mu"], prm["w0"], prm["a0"], prm["w2"], prm["a2"], prm["g2"])


def _bdot(a, b, dims):
    return lax.dot_general(a.astype(BF16), b.astype(BF16), dims, preferred_element_type=F32)


def _cumsum_rows(tri, x):
    hi = x.astype(BF16)
    r1 = x - hi.astype(F32)
    mid = r1.astype(BF16)
    lo = (r1 - mid.astype(F32)).astype(BF16)
    t = tri.astype(BF16)
    dot = lambda p: lax.dot_general(t, p, _NN, preferred_element_type=F32)
    return dot(hi) + dot(mid) + dot(lo)


_NT = (((2,), (2,)), ((0,), (0,)))
_NN = (((2,), (1,)), ((0,), (0,)))
_TN = (((1,), (1,)), ((0,), (0,)))


def _rwkv_scan_kernel(r_ref, k_ref, v_ref, lw_ref, a_ref, g_ref, s0_ref, kk_ref, ka_ref, rk_ref,
                      lnw_ref, lnb_ref, y_ref, sout_ref, st_s, *, chunk):
    c = pl.program_id(2)

    @pl.when(c == 0)
    def _():
        st_s[...] = s0_ref[0]

    r = r_ref[0]
    k = k_ref[0]
    v = v_ref[0]
    lw = lw_ref[0]
    a = a_ref[0]
    hb = r.shape[0]
    s0 = st_s[...]

    kk = k * kk_ref[...]
    kk = kk / jnp.maximum(jnp.sqrt(jnp.sum(kk * kk, axis=-1, keepdims=True)), 1e-12)
    kmod = k * (1.0 + (a - 1.0) * ka_ref[...])

    li = lax.broadcasted_iota(jnp.int32, (chunk, chunk), 0)
    mi = lax.broadcasted_iota(jnp.int32, (chunk, chunk), 1)
    incl = (mi <= li).astype(F32)
    strict = (mi < li).astype(F32)
    cs = _cumsum_rows(jnp.broadcast_to(incl, (hb, chunk, chunk)), lw)
    dec_in = jnp.exp(cs)
    dec_ex = jnp.exp(cs - lw)
    inv = jnp.exp(-cs)
    p_rows = jnp.concatenate([-kk * dec_ex, r * dec_in], axis=1)
    q_rows = jnp.concatenate([kk * a * inv, kmod * inv], axis=1)
    mm = _bdot(p_rows, q_rows, _NT)
    a_ab = mm[:, :chunk, :chunk] * strict
    a_ak = mm[:, :chunk, chunk:] * strict
    r_b = mm[:, chunk:, :chunk] * incl
    r_k = mm[:, chunk:, chunk:] * incl
    ps = _bdot(p_rows, s0, _NT)
    x = ps[:, :chunk] + _bdot(a_ak, v, _NN)
    pw = a_ab
    n_iter = chunk.bit_length() - 1
    for it in range(n_iter):
        x = x + _bdot(pw, x, _NN)
        if it + 1 < n_iter:
            pw = _bdot(pw, pw, _NN)
    y = ps[:, chunk:] + _bdot(r_b, x, _NN) + _bdot(r_k, v, _NN)
    uv = jnp.concatenate([x, v], axis=1)
    s_new = (s0 + _bdot(uv, q_rows, _TN)) * dec_in[:, chunk - 1:chunk, :]
    st_s[...] = s_new
    sout_ref[0] = s_new

    mean = jnp.mean(y, axis=-1, keepdims=True)
    var = jnp.mean(jnp.square(y - mean), axis=-1, keepdims=True)
    yn = (y - mean) * lax.rsqrt(var + RWKV_LN_EPS) * lnw_ref[...] + lnb_ref[...]
    bonus = jnp.sum(r * kmod * rk_ref[...], axis=-1, keepdims=True) * v
    out = (yn + bonus) * g_ref[0]
    hd = out.shape[2]
    for hh in range(hb):
        y_ref[:, hh * hd:(hh + 1) * hd] = out[hh].astype(y_ref.dtype)


def rwkv_scan(r, k, v, lw, a, g, s0, prm):
    bsz, nh, t, hd = r.shape
    chunk = _pick(t, (64, 32, 16, 8))
    hb = _pick(nh, (32, 16, 8, 4, 2, 1))
    nc = t // chunk
    xspec = pl.BlockSpec((1, hb, chunk, hd), lambda b, h, c: (b, h, c, 0))
    pspec = pl.BlockSpec((hb, 1, hd), lambda b, h, c: (h, 0, 0))
    sspec = pl.BlockSpec((1, hb, hd, hd), lambda b, h, c: (b, h, 0, 0))
    vmem = 16 * hb * chunk * LANES * 4 + 40 * hb * 2 * chunk * LANES * 4 + 6 * hb * hd * LANES * 4
    y, s = pl.pallas_call(
        functools.partial(_rwkv_scan_kernel, chunk=chunk),
        grid=(bsz, nh // hb, nc),
        in_specs=[xspec] * 6 + [sspec] + [pspec] * 5,
        out_specs=(pl.BlockSpec((chunk, hb * hd), lambda b, h, c: (b * nc + c, h)), sspec),
        out_shape=(jax.ShapeDtypeStruct((bsz * t, nh * hd), BF16), jax.ShapeDtypeStruct((bsz, nh, hd, hd), F32)),
        scratch_shapes=[pltpu.VMEM((hb, hd, hd), F32)],
        compiler_params=_cparams(("parallel", "parallel", "arbitrary"), vmem),
        name="rwkv_scan",
    )(r, k, v, lw, a, g, s0, prm["k_k"], prm["k_a"], prm["r_k"], prm["ln_w"], prm["ln_b"])
    return y, s


def _router_kernel(x_ref, w_ref, comb_ref, *, n_experts):
    logits = jnp.dot(x_ref[...].astype(BF16), w_ref[...], preferred_element_type=F32)
    lane = lax.broadcasted_iota(jnp.int32, logits.shape, 1)
    big = jnp.int32(LANES)
    logits = jnp.where(lane < n_experts, logits, -jnp.inf)
    t1 = jnp.max(logits, axis=1, keepdims=True)
    i1 = jnp.min(jnp.where(logits == t1, lane, big), axis=1, keepdims=True)
    rest = jnp.where(lane == i1, -jnp.inf, logits)
    t2 = jnp.max(rest, axis=1, keepdims=True)
    i2 = jnp.min(jnp.where(rest == t2, lane, big), axis=1, keepdims=True)
    e2 = jnp.exp(t2 - t1)
    den = 1.0 + e2
    comb_ref[...] = jnp.where(lane == 0, 1.0 / den, jnp.where(lane == 1, e2 / den, jnp.where(
        lane == 2, i1.astype(F32), jnp.where(lane == 3, i2.astype(F32), 0.0))))


def moe_router(x, w_router):
    m, d = x.shape
    ne = w_router.shape[1]
    wp = jnp.zeros((d, LANES), BF16).at[:, :ne].set(w_router.astype(BF16))
    tm = _pick(m, (640, 512, 320, 256, 128, 64, 32, 16, 8))
    return pl.pallas_call(
        functools.partial(_router_kernel, n_experts=ne),
        grid=(m // tm,),
        in_specs=[pl.BlockSpec((tm, d), lambda i: (i, 0)), pl.BlockSpec((d, LANES), lambda i: (0, 0))],
        out_specs=pl.BlockSpec((tm, LANES), lambda i: (i, 0)),
        out_shape=jax.ShapeDtypeStruct((m, LANES), F32),
        compiler_params=_cparams(("parallel",), 3 * tm * d * 4 + 2 * d * LANES * 2 + 8 * tm * LANES * 4),
        name="moe_router",
    )(x, wp)


MOE_TILE = 512


def _moe_plan(rout, n_exp, tile):
    n = rout.shape[0]
    pair_e = rout[:, 2:4].astype(jnp.int32).reshape(-1)
    onehot = (pair_e[:, None] == jnp.arange(n_exp, dtype=jnp.int32)[None, :]).astype(jnp.int32)
    csum = jnp.cumsum(onehot, axis=0)
    rank = jnp.take_along_axis(csum - onehot, pair_e[:, None], axis=1)[:, 0]
    gsz = (csum[-1] + tile - 1) // tile * tile
    gend = jnp.cumsum(gsz)
    slot = ((gend - gsz)[pair_e] + rank).astype(jnp.int32)
    n_tiles = -(-2 * n // tile) + n_exp
    tok = jnp.zeros((n_tiles * tile,), jnp.int32).at[slot].set(jnp.arange(2 * n, dtype=jnp.int32) // 2)
    tile_start = jnp.arange(n_tiles, dtype=jnp.int32) * tile
    te = jnp.minimum(jnp.searchsorted(gend, tile_start, side="right"), n_exp - 1).astype(jnp.int32)
    used = (gend[-1] // tile).astype(jnp.int32).reshape(1)
    return slot, tok, te, used, n_tiles


def _row_copy(src_hbm, row, dst, r, sem):
    return pltpu.make_async_copy(src_hbm.at[pl.ds(row, 1)], dst.at[pl.ds(r, 1)], sem)


def _moe_gather_kernel(tok_ref, used_ref, x_hbm, o_ref, buf, sem, *, tile):
    i = pl.program_id(0)
    used = used_ref[0]

    def issue_tile(t_idx):
        slot = t_idx % 2
        base = t_idx * tile

        def issue(r, c):
            _row_copy(x_hbm, tok_ref[base + r], buf.at[slot], r, sem.at[slot]).start()
            return c

        lax.fori_loop(0, tile, issue, 0)

    @pl.when((i == 0) & (used > 0))
    def _():
        issue_tile(i)

    @pl.when(i + 1 < used)
    def _():
        issue_tile(i + 1)

    @pl.when(i < used)
    def _():
        slot = i % 2

        def wait(r, c):
            _row_copy(x_hbm, 0, buf.at[slot], r, sem.at[slot]).wait()
            return c

        lax.fori_loop(0, tile, wait, 0)
        o_ref[...] = buf[slot].astype(o_ref.dtype)

    @pl.when(i >= used)
    def _():
        o_ref[...] = jnp.zeros(o_ref.shape, o_ref.dtype)


def moe_gather(x, tok, used, n_tiles, tile):
    d = x.shape[1]
    return pl.pallas_call(
        functools.partial(_moe_gather_kernel, tile=tile),
        grid_spec=pltpu.PrefetchScalarGridSpec(
            num_scalar_prefetch=2, grid=(n_tiles,),
            in_specs=[pl.BlockSpec(memory_space=pl.ANY)],
            out_specs=pl.BlockSpec((tile, d), lambda i, tok_r, used_r: (i, 0)),
            scratch_shapes=[pltpu.VMEM((2, tile, d), x.dtype), pltpu.SemaphoreType.DMA((2,))]),
        out_shape=jax.ShapeDtypeStruct((n_tiles * tile, d), BF16),
        compiler_params=_cparams(("arbitrary",), 5 * tile * d * 4),
        name="moe_gather",
    )(tok, used, x)


def _moe_up_kernel(te_ref, used_ref, x_ref, wg_ref, wu_ref, wd_ref, o_ref, wdo_ref, wg_s, wu_s):
    i = pl.program_id(1)
    wdo_ref[...] = wd_ref[...].astype(BF16)

    @pl.when((i == 0) | (te_ref[i] != te_ref[jnp.maximum(i - 1, 0)]))
    def _():
        wg_s[...] = wg_ref[...].astype(BF16)
        wu_s[...] = wu_ref[...].astype(BF16)

    @pl.when(i < used_ref[0])
    def _():
        x = x_ref[...]
        g = jnp.dot(x, wg_s[...], preferred_element_type=F32)
        u = jnp.dot(x, wu_s[...], preferred_element_type=F32)
        o_ref[...] = (g * jax.nn.sigmoid(g) * u).astype(o_ref.dtype)

    @pl.when(i >= used_ref[0])
    def _():
        o_ref[...] = jnp.zeros(o_ref.shape, o_ref.dtype)


def moe_up(xs, wg, wu, wd, te, used, tile):
    p_rows, d = xs.shape
    n_exp, f = wg.shape[0], wg.shape[2]
    tn = _pick(f, (512, 256, 128))
    nj, ni = f // tn, p_rows // tile
    wd_rows = n_exp * f
    rows_c = next(c for c in (16, 32, 64, 128, 256, 512, 1024, 2048, 4096, wd_rows)
                  if wd_rows % c == 0 and wd_rows // c <= nj * ni)
    last_c = wd_rows // rows_c - 1
    cspec = pl.BlockSpec((rows_c, wd.shape[2]), lambda j, i, te_r, used_r: (jnp.minimum(j * ni + i, last_c), 0))
    wspec = pl.BlockSpec((None, d, tn), lambda j, i, te_r, used_r: (te_r[i], 0, j))
    vmem = 2 * tile * d * 2 + 4 * d * tn * 4 + 2 * d * tn * 2 + 6 * tile * tn * 4 + 12 * rows_c * wd.shape[2]
    act, wd_b = pl.pallas_call(
        _moe_up_kernel,
        grid_spec=pltpu.PrefetchScalarGridSpec(
            num_scalar_prefetch=2, grid=(nj, ni),
            in_specs=[pl.BlockSpec((tile, d), lambda j, i, te_r, used_r: (i, 0)), wspec, wspec, cspec],
            out_specs=(pl.BlockSpec((tile, tn), lambda j, i, te_r, used_r: (i, j)), cspec),
            scratch_shapes=[pltpu.VMEM((d, tn), BF16), pltpu.VMEM((d, tn), BF16)]),
        out_shape=(jax.ShapeDtypeStruct((p_rows, f), BF16), jax.ShapeDtypeStruct((wd_rows, wd.shape[2]), BF16)),
        compiler_params=_cparams(("arbitrary", "arbitrary"), vmem),
        name="moe_up",
    )(te, used, xs, wg, wu, wd.reshape(wd_rows, wd.shape[2]))
    return act, wd_b.reshape(wd.shape)


def _moe_down_kernel(te_ref, used_ref, x_ref, w_ref, o_ref, *, per_tile):
    i = pl.program_id(1)

    @pl.when(i < used_ref[0] * per_tile)
    def _():
        o_ref[...] = jnp.dot(x_ref[...], w_ref[...], preferred_element_type=F32)

    @pl.when(i >= used_ref[0] * per_tile)
    def _():
        o_ref[...] = jnp.zeros(o_ref.shape, o_ref.dtype)


def moe_down(act, wd, te, used, tile):
    p_rows, f = act.shape
    d = wd.shape[2]
    tn = _pick(d, (512, 256, 128))
    rows = _pick(tile, (256, 128, 64, 32, 16, 8))
    per_tile = tile // rows
    vmem = 2 * rows * f * 2 + 2 * f * tn * 2 + 4 * rows * tn * 4
    return pl.pallas_call(
        functools.partial(_moe_down_kernel, per_tile=per_tile),
        grid_spec=pltpu.PrefetchScalarGridSpec(
            num_scalar_prefetch=2, grid=(d // tn, p_rows // rows),
            in_specs=[pl.BlockSpec((rows, f), lambda j, i, te_r, used_r: (i, 0)),
                      pl.BlockSpec((None, f, tn), lambda j, i, te_r, used_r: (te_r[i // per_tile], 0, j))],
            out_specs=pl.BlockSpec((rows, tn), lambda j, i, te_r, used_r: (i, j))),
        out_shape=jax.ShapeDtypeStruct((p_rows, d), F32),
        compiler_params=_cparams(("parallel", "arbitrary"), vmem),
        name="moe_down",
    )(te, used, act, wd)


def _moe_combine_kernel(slot_ref, h_ref, g_ref, ys_hbm, fn_ref, o_ref, buf, sem, *, tc, rb0, nsteps):
    i = pl.program_id(0)

    def issue_tile(t_idx):
        bs = t_idx % 2
        base = (rb0 + t_idx) * tc

        def issue(r, c):
            p = 2 * (base + r)
            _row_copy(ys_hbm, slot_ref[p], buf.at[bs, 0], r, sem.at[bs]).start()
            _row_copy(ys_hbm, slot_ref[p + 1], buf.at[bs, 1], r, sem.at[bs]).start()
            return c

        lax.fori_loop(0, tc, issue, 0)

    @pl.when(i == 0)
    def _():
        issue_tile(i)

    @pl.when(i + 1 < nsteps)
    def _():
        issue_tile(i + 1)

    bs = i % 2

    def wait(r, c):
        _row_copy(ys_hbm, 0, buf.at[bs, 0], r, sem.at[bs]).wait()
        _row_copy(ys_hbm, 0, buf.at[bs, 1], r, sem.at[bs]).wait()
        return c

    lax.fori_loop(0, tc, wait, 0)
    g = g_ref[...]
    x = h_ref[...] + (g[:, 0:1] * buf[bs, 0] + g[:, 1:2] * buf[bs, 1])
    y = x * lax.rsqrt(jnp.mean(x * x, axis=-1, keepdims=True) + NORM_EPS)
    o_ref[...] = y * fn_ref[...]


def moe_combine_norm(h, rout, ys, slot, final_norm, row0, n):
    d = h.shape[1]
    tc = _pick(math.gcd(n, row0) if row0 else n, (256, 128, 64, 32, 16, 8))
    rb0 = row0 // tc
    return pl.pallas_call(
        functools.partial(_moe_combine_kernel, tc=tc, rb0=rb0, nsteps=n // tc),
        grid_spec=pltpu.PrefetchScalarGridSpec(
            num_scalar_prefetch=1, grid=(n // tc,),
            in_specs=[pl.BlockSpec((tc, d), lambda i, s: (rb0 + i, 0)),
                      pl.BlockSpec((tc, LANES), lambda i, s: (rb0 + i, 0)),
                      pl.BlockSpec(memory_space=pl.ANY),
                      pl.BlockSpec((1, d), lambda i, s: (0, 0))],
            out_specs=pl.BlockSpec((tc, d), lambda i, s: (i, 0)),
            scratch_shapes=[pltpu.VMEM((2, 2, tc, d), F32), pltpu.SemaphoreType.DMA((2,))]),
        out_shape=jax.ShapeDtypeStruct((n, d), F32),
        compiler_params=_cparams(("arbitrary",), 10 * tc * d * 4),
        name="moe_combine_norm",
    )(slot, h, rout, ys, final_norm.astype(F32).reshape(1, d))


def _rope_tables(pos, rope, width, lead):
    inv = ROPE_THETA ** (-jnp.arange(0, rope, 2, dtype=F32) / rope)
    ang = pos.astype(F32)[:, None] * inv[None, :]
    cos, sin = jnp.cos(ang), jnp.sin(ang)
    n = pos.shape[0]
    ctab = jnp.concatenate([jnp.ones((n, lead), F32), cos, cos, jnp.zeros((n, width - lead - rope), F32)], axis=1)
    stab = jnp.concatenate([jnp.zeros((n, lead), F32), -sin, sin, jnp.zeros((n, width - lead - rope), F32)], axis=1)
    return ctab, stab


def _swap_halves(w):
    half = w.shape[-1] // 2
    return jnp.concatenate([w[..., half:], w[..., :half]], axis=-1)


def kernel(x_prompt, x_sample, cache_mla_ckv, cache_mla_kpe, state_s5_re, state_s5_im, state_rwkv_wkv, state_rwkv_shift, cache_fox_k, cache_fox_v, cache_fox_logf, ln0_mix, w_in0, s5_a_re, s5_a_im, s5_log_dt, s5_b_re, s5_b_im, s5_c_re, s5_c_im, s5_d, s5_w_glu, s5_b_glu, mla_q_norm, mla_w_q_up, mla_kv_norm, mla_w_uk, mla_w_uv, w_out0, ln0_ffn, ffn_w_gate, ffn_w_up, ffn_w_down, ln1_mix, w_in1, rwkv_mu, rwkv_w0, rwkv_w2, rwkv_a0, rwkv_a2, rwkv_g2, rwkv_k_k, rwkv_k_a, rwkv_r_k, rwkv_ln_w, rwkv_ln_b, fox_b_f, w_out1, ln1_ffn, moe_w_router, moe_w_gate, moe_w_up, moe_w_down, final_norm):
    bp, tp, d = x_prompt.shape
    bs, ts, _ = x_sample.shape
    past = cache_mla_ckv.shape[1]
    n_p, n_s = bp * tp, bs * ts
    ntok = n_p + n_s
    streams = ((0, bp, tp), (n_p, bs, ts))

    h = jnp.concatenate([x_prompt.reshape(n_p, d), x_sample.reshape(n_s, d)], axis=0)

    s5_w = s5_d.shape[0]
    q_rank = mla_q_norm.shape[0]
    kv_rank = mla_kv_norm.shape[0]
    n_mh, qk = mla_w_q_up.shape[1], mla_w_q_up.shape[2]
    nope = mla_w_uk.shape[2]
    rope = qk - nope
    vdim = mla_w_uv.shape[2]
    mla_scale = float(qk) ** -0.5
    qw = 2 * LANES
    kvw = kv_rank + LANES

    (xn,) = rmsnorm(h, ln0_mix, (BF16,))
    w_in0b = w_in0.astype(BF16)
    (z_uq,) = matmul(xn, w_in0b[:, :s5_w + q_rank])
    off_kv = s5_w + q_rank
    w_kpe = w_in0b[:, off_kv + kv_rank:]
    zpad = jnp.zeros((d, kvw - kv_rank - rope), BF16)
    w_kv1 = jnp.concatenate([w_in0b[:, off_kv:off_kv + kv_rank], w_kpe, zpad], axis=1)
    w_kv2 = jnp.concatenate([jnp.zeros((d, kv_rank), BF16), _swap_halves(w_kpe), zpad], axis=1)
    pos_p = jnp.arange(tp)
    pos_s = past + jnp.arange(ts)

    def token_tables(width, lead):
        cp, sp = _rope_tables(pos_p, rope, width, lead)
        cs, ss = _rope_tables(pos_s, rope, width, lead)
        return (jnp.concatenate([jnp.tile(cp, (bp, 1)), jnp.tile(cs, (bs, 1))], axis=0),
                jnp.concatenate([jnp.tile(sp, (bp, 1)), jnp.tile(ss, (bs, 1))], axis=0))

    ckv_c, ckv_s = token_tables(kvw, kv_rank)
    z_kv = rope_matmul(xn, w_kv1, w_kv2, ckv_c, ckv_s, F32, tn=kvw)
    ckv_f, ckv_b = rmsnorm(z_kv, mla_kv_norm, (F32, BF16), col_block=0, width=kv_rank)
    kpe_f = z_kv[:, kv_rank:kv_rank + rope]
    kpe_b = z_kv[:, kv_rank:].astype(BF16)

    (cqn,) = rmsnorm(z_uq, mla_q_norm, (BF16,), col_block=s5_w // q_rank, width=q_rank)
    wq = mla_w_q_up.astype(BF16)
    zq = jnp.zeros((q_rank, n_mh, qw - qk), BF16)
    wq1 = jnp.concatenate([wq, zq], axis=2).reshape(q_rank, n_mh * qw)
    wq2 = jnp.concatenate([jnp.zeros((q_rank, n_mh, nope), BF16), _swap_halves(wq[:, :, nope:]), zq],
                          axis=2).reshape(q_rank, n_mh * qw)
    q_c, q_s = token_tables(qw, nope)
    q_all = rope_matmul(cqn, wq1, wq2, q_c, q_s, BF16, tn=qw)

    s5p = s5_params(s5_a_re, s5_a_im, s5_log_dt, s5_b_re, s5_b_im, s5_c_re, s5_c_im, s5_d, s5_w_glu, s5_b_glu)
    g5, p5 = s5_a_re.shape
    wuk = jnp.transpose(mla_w_uk, (1, 2, 0)).astype(BF16)
    wuv = jnp.transpose(mla_w_uv, (1, 0, 2)).astype(BF16)

    y_s5, s5_re, s5_im, y_mla = [], [], [], []
    for si, (off, bsz, t) in enumerate(streams):
        if si == 0:
            h0r = jnp.zeros((bsz, g5 * p5), F32)
            h0i = h0r
            ckv_k = ckv_b[off:off + bsz * t].reshape(bsz, t, kv_rank)
            kpe_k = kpe_b[off:off + bsz * t].reshape(bsz, t, LANES)
            klen, causal = t, True
        else:
            h0r, h0i = state_s5_re.astype(F32), state_s5_im.astype(F32)
            klen, causal = past + t, False
            padk = -(-klen // LANES) * LANES - klen
            ckv_k = jnp.concatenate([cache_mla_ckv.astype(BF16), ckv_b[off:off + bsz * t].reshape(bsz, t, kv_rank),
                                     jnp.zeros((bsz, padk, kv_rank), BF16)], axis=1)
            kpe_cache = jnp.concatenate([cache_mla_kpe.astype(BF16),
                                         jnp.zeros((bsz, past, LANES - rope), BF16)], axis=2)
            kpe_k = jnp.concatenate([kpe_cache, kpe_b[off:off + bsz * t].reshape(bsz, t, LANES),
                                     jnp.zeros((bsz, padk, LANES), BF16)], axis=1)
        ys, hr, hi = s5_mixer(z_uq, off, bsz, t, h0r, h0i, s5p)
        y_s5.append(ys)
        s5_re.append(hr.reshape(bsz, g5, p5))
        s5_im.append(hi.reshape(bsz, g5, p5))
        y_mla.append(mla_attention(q_all, off, bsz, t, ckv_k, kpe_k, wuk, wuv, klen, causal, mla_scale))

    mix0 = jnp.concatenate([jnp.concatenate(y_s5, axis=0), jnp.concatenate(y_mla, axis=0)], axis=1)
    (h,) = matmul(mix0, w_out0.astype(BF16), res=h)
    (hn,) = rmsnorm(h, ln0_ffn, (BF16,))
    act = swiglu_up(hn, ffn_w_gate, ffn_w_up)
    (h,) = matmul(act, ffn_w_down.astype(BF16), res=h, tm=_pick(ntok, (640, 512, 256, 128, 64, 32, 16, 8)),
                  tn=256, tk=ffn_w_down.shape[0])

    rw = rwkv_w0.shape[0]
    nh_r, hd_r = rwkv_k_k.shape
    shift_w = rwkv_mu.shape[0]
    lora_w = shift_w - 3 * rw
    slab = -(-lora_w // LANES) * LANES
    wz = 3 * rw + slab
    nh_f = fox_b_f.shape[0]
    fw = (w_in1.shape[1] - shift_w - nh_f) // 3
    fox_scale = float(fw // nh_f) ** -0.5
    d_lw, d_la = rwkv_w2.shape[0], rwkv_a2.shape[0]

    (xn,) = rmsnorm(h, ln1_mix, (BF16,))
    w_in1b = w_in1.astype(BF16)
    (zr,) = matmul(xn, jnp.pad(w_in1b[:, :shift_w], ((0, 0), (0, wz - shift_w))))
    (fq,) = matmul(xn, w_in1b[:, shift_w:shift_w + fw], (BF16,))
    fk = [matmul_heads(xn, w_in1b[:, shift_w + fw:shift_w + 2 * fw], nh_f, (o_, b_ * t_)) for o_, b_, t_ in streams]
    fv = [matmul_heads(xn, w_in1b[:, shift_w + 2 * fw:shift_w + 3 * fw], nh_f, (o_, b_ * t_)) for o_, b_, t_ in streams]
    (zf,) = matmul(xn, jnp.pad(w_in1b[:, shift_w + 3 * fw:], ((0, 0), (0, LANES - nh_f))))

    padrow = lambda wgt, lo: jnp.zeros((slab, rw), BF16).at[lo:lo + wgt.shape[0]].set(wgt.astype(BF16))
    row2 = lambda x_, n_: x_.astype(F32).reshape(1, n_)
    head3 = lambda x_: x_.astype(F32).reshape(nh_r, 1, hd_r)
    rprm = dict(mu=jnp.pad(row2(rwkv_mu, shift_w), ((0, 0), (0, wz - shift_w))), w0=row2(rwkv_w0, rw),
                a0=row2(rwkv_a0, rw), w2=padrow(rwkv_w2, 0), a2=padrow(rwkv_a2, d_lw),
                g2=padrow(rwkv_g2, d_lw + d_la), k_k=head3(rwkv_k_k), k_a=head3(rwkv_k_a), r_k=head3(rwkv_r_k),
                ln_w=head3(rwkv_ln_w), ln_b=head3(rwkv_ln_b))

    y_r, wkv, shift_new, y_f, logf_out = [], [], [], [], []
    for si, (off, bsz, t) in enumerate(streams):
        if si == 0:
            shift_prev = jnp.zeros((bsz, wz), F32)
            s0 = jnp.zeros((bsz, nh_r, hd_r, hd_r), F32)
            pre = None
            k_all, v_all, tkeys = fk[si][1], fv[si][1], t
            qoff = 0
        else:
            shift_prev = jnp.pad(state_rwkv_shift.astype(F32), ((0, 0), (0, wz - shift_w)))
            s0 = state_rwkv_wkv.astype(F32)
            pre = jnp.pad(cache_fox_logf.astype(F32), ((0, 0), (0, 0), (0, LANES - nh_f)))
            qoff = cache_fox_k.shape[1]
            tkeys = -(-(qoff + t) // LANES) * LANES
            cat = lambda cache, new: jnp.concatenate(
                [cache.reshape(bsz, qoff, fw).astype(BF16), new.reshape(bsz, t, fw),
                 jnp.zeros((bsz, tkeys - qoff - t, fw), BF16)], axis=1).reshape(bsz * tkeys, fw)
            k_all, v_all = cat(cache_fox_k, fk[si][1]), cat(cache_fox_v, fv[si][1])
        parts = rwkv_prep(zr, off, bsz, t, shift_prev, rprm, nh_r)
        yr, s_fin = rwkv_scan(*parts, s0, rprm)
        y_r.append(yr)
        wkv.append(s_fin)
        shift_new.append(zr[off + t - 1:off + bsz * t:t, :shift_w])
        logf, cum, cumt = fox_gate(zf, off, bsz, t, fox_b_f, pre)
        logf_out.append(logf[:, :, :nh_f])
        y_f.append(fox_attention(fq, off, bsz, t, k_all, v_all, tkeys, cum, cumt, qoff, fox_scale))

    mix1 = jnp.concatenate([jnp.concatenate(y_r, axis=0), jnp.concatenate(y_f, axis=0)], axis=1)
    (h,) = matmul(mix1, w_out1.astype(BF16), res=h)
    (hn,) = rmsnorm(h, ln1_ffn, (F32,))
    rout = moe_router(hn, moe_w_router)
    n_exp = moe_w_gate.shape[0]
    slot, tok, te, used, n_tiles = _moe_plan(rout, n_exp, MOE_TILE)
    xs = moe_gather(hn, tok, used, n_tiles, MOE_TILE)
    act, wd_b = moe_up(xs, moe_w_gate, moe_w_up, moe_w_down, te, used, MOE_TILE)
    ys = moe_down(act, wd_b, te, used, MOE_TILE)
    nfh = fw // nh_f
    outs = [moe_combine_norm(h, rout, ys, slot, final_norm, off, bsz * t).reshape(bsz, t, d) for off, bsz, t in streams]
    for si, (off, bsz, t) in enumerate(streams):
        rows = slice(off, off + bsz * t)
        outs += [ckv_f[rows].reshape(bsz, t, kv_rank), kpe_f[rows].reshape(bsz, t, rope), s5_re[si], s5_im[si],
                 wkv[si], shift_new[si], fk[si][0].reshape(bsz, t, nh_f, nfh), fv[si][0].reshape(bsz, t, nh_f, nfh),
                 logf_out[si]]
    return tuple(outs)
```

```python
import functools
import math

import jax
import jax.numpy as jnp
from jax import lax
from jax.experimental import pallas as pl
from jax.experimental.pallas import tpu as pltpu

F32 = jnp.float32
BF16 = jnp.bfloat16

V7X_VMEM_BYTES = 64 * 1024 * 1024
VMEM_CAP = V7X_VMEM_BYTES - 4 * 1024 * 1024
LANES = 128

NORM_EPS = 1e-6
NEG_INF = -1e30
CHUNK = 64
ROPE_THETA = 10000.0
RWKV_LN_EPS = 64e-5
S5_BLOCK_GROUPS = 8


def _pick(n, cands):
    for c in cands:
        if c <= n and n % c == 0:
            return c
    return n


def _cparams(sem, vmem_bytes):
    limit = int(min(max(vmem_bytes * 1.25 + (4 << 20), 24 << 20), VMEM_CAP))
    return pltpu.CompilerParams(dimension_semantics=sem, vmem_limit_bytes=limit)


def _rmsnorm_kernel(x_ref, g_ref, *o_refs):
    x = x_ref[...].astype(F32)
    y = x * lax.rsqrt(jnp.mean(x * x, axis=-1, keepdims=True) + NORM_EPS)
    y = y * g_ref[...]
    for o in o_refs:
        o[...] = y.astype(o.dtype)


def rmsnorm(x, g, out_dtypes, col_block=0, width=None):
    m = x.shape[0]
    width = x.shape[1] if width is None else width
    tm = _pick(m, (512, 320, 256, 128, 64, 32, 16, 8))
    outs = tuple(jax.ShapeDtypeStruct((m, width), d) for d in out_dtypes)
    res = pl.pallas_call(
        _rmsnorm_kernel,
        grid=(m // tm,),
        in_specs=[pl.BlockSpec((tm, width), lambda i: (i, col_block)),
                  pl.BlockSpec((1, width), lambda i: (0, 0))],
        out_specs=tuple(pl.BlockSpec((tm, width), lambda i: (i, 0)) for _ in out_dtypes),
        out_shape=outs,
        compiler_params=_cparams(("parallel",), tm * width * 4 * 2 * (1 + len(out_dtypes))),
        name="rmsnorm",
    )(x, g.reshape(1, width).astype(F32))
    return res


def _mm_kernel(*refs, nk, has_res, has_scale, n_out):
    x_ref, w_ref = refs[0], refs[1]
    pos = 2
    res_ref = scale_ref = None
    if has_res:
        res_ref = refs[pos]
        pos += 1
    if has_scale:
        scale_ref = refs[pos]
        pos += 1
    o_refs = refs[pos:pos + n_out]
    acc_ref = refs[pos + n_out] if nk > 1 else None

    part = jnp.dot(x_ref[...].astype(BF16), w_ref[...].astype(BF16), preferred_element_type=F32)

    def finish(acc):
        if has_scale:
            acc = acc * scale_ref[...]
        if has_res:
            acc = res_ref[...] + acc
        for o in o_refs:
            o[...] = acc.astype(o.dtype)

    if nk == 1:
        finish(part)
    else:
        k = pl.program_id(2)

        @pl.when(k == 0)
        def _():
            acc_ref[...] = part

        @pl.when(k > 0)
        def _():
            acc_ref[...] += part

        @pl.when(k == nk - 1)
        def _():
            finish(acc_ref[...])


def matmul(x, w, out_dtypes=(F32,), res=None, scale=None, w_index=None, tm=None, tn=None, tk=None, rows=None):
    row0, m = rows if rows is not None else (0, x.shape[0])
    kdim = x.shape[1]
    n = w.shape[-1]
    tm = tm or _pick(math.gcd(m, row0) if row0 else m, (1280, 1024, 640, 512, 320, 256, 128, 64, 32, 16, 8))
    tn = tn or _pick(n, (512, 384, 256, 128))
    tk = tk or (kdim if kdim <= 4096 else _pick(kdim, (2048, 1792, 1024, 512, 256, 128)))
    nk = kdim // tk
    grid = (m // tm, n // tn, nk)
    rb0 = row0 // tm
    in_specs = [pl.BlockSpec((tm, tk), lambda i, j, k: (rb0 + i, k))]
    if w.ndim == 3:
        in_specs.append(pl.BlockSpec((None, tk, tn), lambda i, j, k: (w_index, k, j)))
    else:
        in_specs.append(pl.BlockSpec((tk, tn), lambda i, j, k: (k, j)))
    args = [x, w]
    if res is not None:
        in_specs.append(pl.BlockSpec((tm, tn), lambda i, j, k: (i, j)))
        args.append(res)
    if scale is not None:
        in_specs.append(pl.BlockSpec((tm, 1), lambda i, j, k: (i, 0)))
        args.append(scale)
    out_specs = tuple(pl.BlockSpec((tm, tn), lambda i, j, k: (i, j)) for _ in out_dtypes)
    out_shape = tuple(jax.ShapeDtypeStruct((m, n), d) for d in out_dtypes)
    scratch = [pltpu.VMEM((tm, tn), F32)] if nk > 1 else []
    vmem = (2 * tm * tk * x.dtype.itemsize + 2 * tk * tn * w.dtype.itemsize
            + tm * tn * 4 * (2 * len(out_dtypes) + 1 + (2 if res is not None else 0)))
    outs = pl.pallas_call(
        functools.partial(_mm_kernel, nk=nk, has_res=res is not None, has_scale=scale is not None,
                          n_out=len(out_dtypes)),
        grid=grid, in_specs=in_specs, out_specs=out_specs, out_shape=out_shape,
        scratch_shapes=scratch,
        compiler_params=_cparams(("parallel", "parallel", "arbitrary"), vmem),
        name="matmul",
    )(*args)
    return outs


def _mm_heads_kernel(x_ref, w_ref, o3_ref, ob_ref):
    acc = jnp.dot(x_ref[...], w_ref[...], preferred_element_type=F32)
    ob_ref[...] = acc.astype(ob_ref.dtype)
    hd = o3_ref.shape[2]
    for hh in range(o3_ref.shape[1]):
        o3_ref[:, hh, :] = acc[:, hh * hd:(hh + 1) * hd]


def matmul_heads(x, w, nheads, rows):
    row0, m = rows
    kdim, n = w.shape
    hd = n // nheads
    tm = _pick(math.gcd(m, row0) if row0 else m, (256, 128, 64, 32, 16, 8))
    rb0 = row0 // tm
    vmem = 2 * tm * kdim * 2 + 2 * kdim * n * 2 + 8 * tm * n * 4
    return pl.pallas_call(
        _mm_heads_kernel,
        grid=(m // tm,),
        in_specs=[pl.BlockSpec((tm, kdim), lambda i: (rb0 + i, 0)), pl.BlockSpec((kdim, n), lambda i: (0, 0))],
        out_specs=(pl.BlockSpec((tm, nheads, hd), lambda i: (i, 0, 0)), pl.BlockSpec((tm, n), lambda i: (i, 0))),
        out_shape=(jax.ShapeDtypeStruct((m, nheads, hd), F32), jax.ShapeDtypeStruct((m, n), BF16)),
        compiler_params=_cparams(("parallel",), vmem),
        name="matmul_heads",
    )(x, w)


def _rope_mm_kernel(x_ref, w1_ref, w2_ref, c_ref, s_ref, o_ref):
    x = x_ref[...]
    a = jnp.dot(x, w1_ref[...], preferred_element_type=F32)
    b = jnp.dot(x, w2_ref[...], preferred_element_type=F32)
    o_ref[...] = (a * c_ref[...] + b * s_ref[...]).astype(o_ref.dtype)


def rope_matmul(x, w1, w2, ctab, stab, out_dtype, tn):
    m, kdim = x.shape
    n = w1.shape[1]
    est = lambda rows: 2 * rows * kdim * 2 + 4 * kdim * tn * 2 + 8 * rows * tn * 4
    tm = next((c for c in (1280, 1024, 640, 512, 320, 256, 128, 64, 32, 16, 8)
               if m % c == 0 and est(c) <= VMEM_CAP // 2), 8)
    vmem = est(tm)
    return pl.pallas_call(
        _rope_mm_kernel,
        grid=(m // tm, n // tn),
        in_specs=[pl.BlockSpec((tm, kdim), lambda i, j: (i, 0)),
                  pl.BlockSpec((kdim, tn), lambda i, j: (0, j)),
                  pl.BlockSpec((kdim, tn), lambda i, j: (0, j)),
                  pl.BlockSpec((tm, tn), lambda i, j: (i, 0)),
                  pl.BlockSpec((tm, tn), lambda i, j: (i, 0))],
        out_specs=pl.BlockSpec((tm, tn), lambda i, j: (i, j)),
        out_shape=jax.ShapeDtypeStruct((m, n), out_dtype),
        compiler_params=_cparams(("parallel", "parallel"), vmem),
        name="rope_matmul",
    )(x, w1, w2, ctab, stab)


def _swiglu_up_kernel(x_ref, wg_ref, wu_ref, o_ref):
    x = x_ref[...]
    g = jnp.dot(x, wg_ref[...].astype(BF16), preferred_element_type=F32)
    u = jnp.dot(x, wu_ref[...].astype(BF16), preferred_element_type=F32)
    o_ref[...] = (g * jax.nn.sigmoid(g) * u).astype(o_ref.dtype)


def swiglu_up(x, wg, wu, w_index=None):
    m, kdim = x.shape
    n = wg.shape[-1]
    tm = _pick(m, (1280, 1024, 640, 512, 320, 256, 128, 64, 32, 16, 8))
    tn = _pick(n, (256, 128))
    if wg.ndim == 3:
        wspec = pl.BlockSpec((None, kdim, tn), lambda i, j: (w_index, 0, j))
    else:
        wspec = pl.BlockSpec((kdim, tn), lambda i, j: (0, j))
    vmem = 2 * tm * kdim * 2 + 4 * kdim * tn * wg.dtype.itemsize + 6 * tm * tn * 4
    return pl.pallas_call(
        _swiglu_up_kernel,
        grid=(m // tm, n // tn),
        in_specs=[pl.BlockSpec((tm, kdim), lambda i, j: (i, 0)), wspec, wspec],
        out_specs=pl.BlockSpec((tm, tn), lambda i, j: (i, j)),
        out_shape=jax.ShapeDtypeStruct((m, n), BF16),
        compiler_params=_cparams(("parallel", "parallel"), vmem),
        name="swiglu_up",
    )(x, wg, wu)


def _s5_kernel(u_ref, h0r_ref, h0i_ref, lr_ref, li_ref, bbr_ref, bbi_ref, ccr_ref, cci_ref,
               d_ref, wglu_ref, bglu_ref, y_ref, hr_ref, hi_ref, xr_s, xi_s, st_r, st_i, *, tc, nblk):
    c = pl.program_id(1)

    @pl.when(c == 0)
    def _():
        st_r[...] = h0r_ref[0]
        st_i[...] = h0i_ref[0]

    u = u_ref[...]
    ub = u.astype(BF16)
    sw = xr_s.shape[1] // nblk
    for k in range(nblk):
        uk = ub[:, k * LANES:(k + 1) * LANES]
        xr_s[:, k * sw:(k + 1) * sw] = jnp.dot(uk, bbr_ref[k], preferred_element_type=F32)
        xi_s[:, k * sw:(k + 1) * sw] = jnp.dot(uk, bbi_ref[k], preferred_element_type=F32)

    scan_w = 1024
    for q in range(xr_s.shape[1] // scan_w):
        cols = slice(q * scan_w, (q + 1) * scan_w)
        lr = lr_ref[:, cols]
        li = li_ref[:, cols]

        def body(t, carry, cols=cols, lr=lr, li=li):
            hr, hi = carry
            nr = lr * hr - li * hi + xr_s[pl.ds(t, 1), cols]
            ni = lr * hi + li * hr + xi_s[pl.ds(t, 1), cols]
            xr_s[pl.ds(t, 1), cols] = nr
            xi_s[pl.ds(t, 1), cols] = ni
            return nr, ni

        hr, hi = lax.fori_loop(0, tc, body, (st_r[:, cols], st_i[:, cols]))
        st_r[:, cols] = hr
        st_i[:, cols] = hi

    hr_ref[0] = st_r[...]
    hi_ref[0] = st_i[...]

    ys = []
    for k in range(nblk):
        xr = xr_s[:, k * sw:(k + 1) * sw].astype(BF16)
        xi = xi_s[:, k * sw:(k + 1) * sw].astype(BF16)
        ys.append(jnp.dot(xr, ccr_ref[k], preferred_element_type=F32)
                  - jnp.dot(xi, cci_ref[k], preferred_element_type=F32))
    y = jnp.concatenate(ys, axis=1) + d_ref[...] * u
    y = jax.nn.gelu(y)
    gate = jax.nn.sigmoid(jnp.dot(y.astype(BF16), wglu_ref[...], preferred_element_type=F32) + bglu_ref[...])
    y_ref[...] = (y * gate).astype(y_ref.dtype)


def s5_mixer(z, row_off, bsz, t, h0_re, h0_im, prm):
    width = prm["d"].shape[1]
    nblk = width // LANES
    nstate = prm["lr"].shape[1]
    tc = _pick(t, (256, 128, 64, 32, 16, 8))
    nt = t // tc
    rb0 = row_off // tc
    full = lambda shape: pl.BlockSpec(shape, lambda b, c: (0,) * len(shape))
    vmem = (4 * tc * width * 4 + 2 * tc * nstate * 4 + 4 * nblk * LANES * (nstate // nblk) * 2 * 2
            + 2 * width * width * 2 + 8 * tc * width * 4)
    y, hr, hi = pl.pallas_call(
        functools.partial(_s5_kernel, tc=tc, nblk=nblk),
        grid=(bsz, nt),
        in_specs=[pl.BlockSpec((tc, width), lambda b, c: (rb0 + b * nt + c, 0)),
                  pl.BlockSpec((1, 1, nstate), lambda b, c: (b, 0, 0)),
                  pl.BlockSpec((1, 1, nstate), lambda b, c: (b, 0, 0)),
                  full((1, nstate)), full((1, nstate)),
                  full(prm["bbr"].shape), full(prm["bbi"].shape),
                  full(prm["ccr"].shape), full(prm["cci"].shape),
                  full((1, width)), full((width, width)), full((1, width))],
        out_specs=(pl.BlockSpec((tc, width), lambda b, c: (b * nt + c, 0)),
                   pl.BlockSpec((1, 1, nstate), lambda b, c: (b, 0, 0)),
                   pl.BlockSpec((1, 1, nstate), lambda b, c: (b, 0, 0))),
        out_shape=(jax.ShapeDtypeStruct((bsz * t, width), BF16),
                   jax.ShapeDtypeStruct((bsz, 1, nstate), F32),
                   jax.ShapeDtypeStruct((bsz, 1, nstate), F32)),
        scratch_shapes=[pltpu.VMEM((tc, nstate), F32), pltpu.VMEM((tc, nstate), F32),
                        pltpu.VMEM((1, nstate), F32), pltpu.VMEM((1, nstate), F32)],
        compiler_params=_cparams(("parallel", "arbitrary"), vmem),
        name="s5_mixer",
    )(z, h0_re.reshape(bsz, 1, nstate), h0_im.reshape(bsz, 1, nstate), prm["lr"], prm["li"],
      prm["bbr"], prm["bbi"], prm["ccr"], prm["cci"], prm["d"], prm["wglu"], prm["bglu"])
    return y, hr, hi


def s5_params(a_re, a_im, log_dt, b_re, b_im, c_re, c_im, d_skip, w_glu, b_glu):
    g, p = a_re.shape
    nch = b_re.shape[2]
    dt = jnp.exp(log_dt.astype(F32))[:, None]
    ar, ai = a_re.astype(F32), a_im.astype(F32)
    mag = jnp.exp(ar * dt)
    lr = mag * jnp.cos(ai * dt)
    li = mag * jnp.sin(ai * dt)
    den = ar * ar + ai * ai
    fr = ((lr - 1.0) * ar + li * ai) / den
    fi = (li * ar - (lr - 1.0) * ai) / den
    br, bi = b_re.astype(F32), b_im.astype(F32)
    bbr = fr[..., None] * br - fi[..., None] * bi
    bbi = fr[..., None] * bi + fi[..., None] * br
    gb = S5_BLOCK_GROUPS
    nblk = g // gb
    eye = jnp.eye(gb, dtype=F32)

    def blk_in(m):
        m = m.reshape(nblk, gb, p, nch)
        return jnp.einsum("kgpn,gh->kgnhp", m, eye).reshape(nblk, gb * nch, gb * p).astype(BF16)

    def blk_out(m):
        m = m.astype(F32).reshape(nblk, gb, nch, p)
        return jnp.einsum("kgnp,gh->kgphn", m, eye).reshape(nblk, gb * p, gb * nch).astype(BF16)

    width = g * nch
    return dict(lr=lr.reshape(1, g * p), li=li.reshape(1, g * p), bbr=blk_in(bbr), bbi=blk_in(bbi),
                ccr=blk_out(c_re), cci=blk_out(c_im), d=d_skip.astype(F32).reshape(1, width),
                wglu=w_glu.astype(BF16), bglu=b_glu.astype(F32).reshape(1, width))


MLA_ROWS = 1024
MLA_TQ = 512
MLA_CHAIN_ROWS = 256


def _mla_kernel(q_ref, ckv_ref, kpe_ref, wuk_ref, wuv_ref, o_ref, qa_s, qpe_s, m_s, l_s, acc_s,
                *, tq, tk, nk_total, klen, causal, scale, hps):
    i = pl.program_id(1)
    nope = wuk_ref.shape[1]
    qw = q_ref.shape[1] // hps
    vdim = wuv_ref.shape[2]
    rows_all = hps * tq
    rc = MLA_CHAIN_ROWS if rows_all % MLA_CHAIN_ROWS == 0 else rows_all
    for hh in range(hps):
        rows = slice(hh * tq, (hh + 1) * tq)
        qa = jnp.dot(q_ref[:, hh * qw:hh * qw + nope], wuk_ref[hh], preferred_element_type=F32)
        qa_s[rows, :] = qa.astype(qa_s.dtype)
        qpe_s[rows, :] = q_ref[:, hh * qw + nope:(hh + 1) * qw]
    m_s[...] = jnp.full(m_s.shape, NEG_INF, F32)
    l_s[...] = jnp.zeros(l_s.shape, F32)
    acc_s[...] = jnp.zeros(acc_s.shape, F32)

    def tile(j, masked):
        ks = pl.multiple_of(j * tk, tk)
        ckv = ckv_ref[0, pl.ds(ks, tk), :]
        kpe = kpe_ref[0, pl.ds(ks, tk), :]
        scores = []
        for c0 in range(0, rows_all, rc):
            rows = slice(c0, c0 + rc)
            s = lax.dot_general(qa_s[rows, :].astype(BF16), ckv, (((1,), (1,)), ((), ())), preferred_element_type=F32)
            s = s + lax.dot_general(qpe_s[rows, :], kpe, (((1,), (1,)), ((), ())), preferred_element_type=F32)
            scores.append(s * scale)
        for c0, s in zip(range(0, rows_all, rc), scores):
            rows = slice(c0, c0 + rc)
            if masked:
                kpos = ks + lax.broadcasted_iota(jnp.int32, (rc, tk), 1)
                if causal:
                    qpos = i * tq + (c0 + lax.broadcasted_iota(jnp.int32, (rc, tk), 0)) % tq
                    s = jnp.where(kpos // CHUNK <= qpos // CHUNK, s, NEG_INF)
                if klen < nk_total * tk:
                    s = jnp.where(kpos < klen, s, NEG_INF)
            m_prev = m_s[rows, :]
            m_new = jnp.maximum(m_prev, jnp.max(s, axis=1, keepdims=True))
            alpha = jnp.exp(m_prev - m_new)
            p = jnp.exp(s - m_new)
            l_s[rows, :] = alpha * l_s[rows, :] + jnp.sum(p, axis=1, keepdims=True)
            acc_s[rows, :] = alpha * acc_s[rows, :] + jnp.dot(p.astype(BF16), ckv, preferred_element_type=F32)
            m_s[rows, :] = m_new

    def full_tile(j, c):
        tile(j, False)
        return c

    n_full = i if causal else nk_total - 1
    lax.fori_loop(0, n_full, full_tile, 0)
    tile(n_full, True)
    o_lat = (acc_s[...] / l_s[...]).astype(BF16)
    for hh in range(hps):
        o_ref[:, hh * vdim:(hh + 1) * vdim] = jnp.dot(
            o_lat[hh * tq:(hh + 1) * tq], wuv_ref[hh], preferred_element_type=F32).astype(o_ref.dtype)


def mla_attention(q_all, row_off, bsz, t, ckv, kpe, wuk, wuv, klen, causal, scale):
    nheads, nope, lat = wuk.shape
    vdim = wuv.shape[2]
    qw = q_all.shape[1] // nheads
    tkeys = ckv.shape[1]
    tq = _pick(t, (MLA_TQ, 256, 128, 64, 32, 16, 8))
    tk = tq if causal else _pick(tkeys, (256, 128))
    if causal:
        assert tq % CHUNK == 0 and tkeys == t
    hps = _pick(nheads, tuple(c for c in (24, 16, 12, 8, 6, 4, 3, 2, 1) if c * tq <= MLA_ROWS))
    nq = t // tq
    rb0 = row_off // tq
    rows_all = hps * tq
    vmem = (2 * tkeys * (lat + LANES) * 2 + 4 * tq * hps * qw * 2 + 3 * rows_all * lat * 4 + 8 * rows_all * tk * 4
            + 4 * hps * (nope + vdim) * lat * 2 + (2 << 20))
    return pl.pallas_call(
        functools.partial(_mla_kernel, tq=tq, tk=tk, nk_total=tkeys // tk, klen=klen, causal=causal, scale=scale,
                          hps=hps),
        grid=(bsz, nq, nheads // hps),
        in_specs=[pl.BlockSpec((tq, hps * qw), lambda b, i, h: (rb0 + b * nq + i, h)),
                  pl.BlockSpec((1, tkeys, lat), lambda b, i, h: (b, 0, 0)),
                  pl.BlockSpec((1, tkeys, LANES), lambda b, i, h: (b, 0, 0)),
                  pl.BlockSpec((hps, nope, lat), lambda b, i, h: (h, 0, 0)),
                  pl.BlockSpec((hps, lat, vdim), lambda b, i, h: (h, 0, 0))],
        out_specs=pl.BlockSpec((tq, hps * vdim), lambda b, i, h: (b * nq + i, h)),
        out_shape=jax.ShapeDtypeStruct((bsz * t, nheads * vdim), BF16),
        scratch_shapes=[pltpu.VMEM((rows_all, lat), F32), pltpu.VMEM((rows_all, LANES), BF16),
                        pltpu.VMEM((rows_all, 1), F32), pltpu.VMEM((rows_all, 1), F32),
                        pltpu.VMEM((rows_all, lat), F32)],
        compiler_params=_cparams(("parallel", "parallel", "arbitrary"), vmem),
        name="mla_attention",
    )(q_all, ckv, kpe, wuk, wuv)


def _fox_gate_kernel(pre_ref, zf_ref, bf_ref, logf_ref, cum_ref, cumt_ref, lf_s, *, npre, t, blk):
    total = lf_s.shape[0]
    z = zf_ref[0] + bf_ref[...]
    logf = jnp.minimum(z, 0.0) - jnp.log1p(jnp.exp(-jnp.abs(z)))
    logf_ref[0] = logf
    if npre + t < total:
        lf_s[...] = jnp.zeros(lf_s.shape, F32)
    if npre:
        lf_s[0:npre, :] = pre_ref[0]
    lf_s[npre:npre + t, :] = logf
    tri = (lax.broadcasted_iota(jnp.int32, (blk, blk), 1)
           <= lax.broadcasted_iota(jnp.int32, (blk, blk), 0)).astype(F32)
    carry = jnp.zeros((1, LANES), F32)
    for c in range(total // blk):
        rows = slice(c * blk, (c + 1) * blk)
        cum = jnp.dot(tri, lf_s[rows, :], preferred_element_type=F32, precision=lax.Precision.HIGHEST) + carry
        cum_ref[0, rows, :] = cum
        cumt_ref[0, c] = cum.T[:cumt_ref.shape[2], :]
        carry = cum[blk - 1:blk, :]


def fox_gate(zf, row_off, bsz, t, b_f, pre):
    nheads = b_f.shape[0]
    npre = 0 if pre is None else pre.shape[1]
    total = -(-(npre + t) // LANES) * LANES
    if pre is None:
        pre = jnp.zeros((bsz, 8, LANES), F32)
    pp = pre.shape[1]
    hrows = -(-nheads // 8) * 8
    zf3 = zf[row_off:row_off + bsz * t].reshape(bsz, t, LANES)
    bfp = jnp.zeros((1, LANES), F32).at[0, :nheads].set(b_f.astype(F32))
    return pl.pallas_call(
        functools.partial(_fox_gate_kernel, npre=npre, t=t, blk=LANES),
        grid=(bsz,),
        in_specs=[pl.BlockSpec((1, pp, LANES), lambda b: (b, 0, 0)),
                  pl.BlockSpec((1, t, LANES), lambda b: (b, 0, 0)),
                  pl.BlockSpec((1, LANES), lambda b: (0, 0))],
        out_specs=(pl.BlockSpec((1, t, LANES), lambda b: (b, 0, 0)),
                   pl.BlockSpec((1, total, LANES), lambda b: (b, 0, 0)),
                   pl.BlockSpec((1, total // LANES, hrows, LANES), lambda b: (b, 0, 0, 0))),
        out_shape=(jax.ShapeDtypeStruct((bsz, t, LANES), F32),
                   jax.ShapeDtypeStruct((bsz, total, LANES), F32),
                   jax.ShapeDtypeStruct((bsz, total // LANES, hrows, LANES), F32)),
        scratch_shapes=[pltpu.VMEM((total, LANES), F32)],
        compiler_params=_cparams(("parallel",), 12 * total * LANES * 4),
        name="fox_gate",
    )(pre, zf3, bfp)


FOX_HEADS_PER_STEP = 4
FOX_TQ = 512


def _fox_attn_kernel(q_ref, k_ref, v_ref, cq_ref, ck_ref, o_ref, m_s, l_s, acc_s,
                     *, tq, tk, qoff, scale, hps):
    hg = pl.program_id(1)
    i = pl.program_id(2)
    hd = acc_s.shape[2]
    lane = lax.broadcasted_iota(jnp.int32, cq_ref.shape[1:], 1)
    cq = cq_ref[0]
    fq = [jnp.sum(jnp.where(lane == hg * hps + hh, cq, 0.0), axis=1, keepdims=True) for hh in range(hps)]
    m_s[...] = jnp.full(m_s.shape, NEG_INF, F32)
    l_s[...] = jnp.zeros(l_s.shape, F32)
    acc_s[...] = jnp.zeros(acc_s.shape, F32)

    def tile(j, masked):
        ks = pl.multiple_of(j * tk, tk)
        scores = []
        for hh in range(hps):
            cols = slice(hh * hd, (hh + 1) * hd)
            k = k_ref[pl.ds(ks, tk), cols]
            fk = jnp.concatenate([ck_ref[0, j * (tk // LANES) + c, pl.ds(hg * hps + hh, 1), :]
                                  for c in range(tk // LANES)], axis=1)
            s = lax.dot_general(q_ref[:, cols], k, (((1,), (1,)), ((), ())), preferred_element_type=F32) * scale
            scores.append(s + fq[hh] - fk)
        for hh, s in enumerate(scores):
            cols = slice(hh * hd, (hh + 1) * hd)
            if masked:
                qpos = qoff + i * tq + lax.broadcasted_iota(jnp.int32, (tq, tk), 0)
                kpos = ks + lax.broadcasted_iota(jnp.int32, (tq, tk), 1)
                s = jnp.where(kpos <= qpos, s, NEG_INF)
            m_prev = m_s[hh]
            m_new = jnp.maximum(m_prev, jnp.max(s, axis=1, keepdims=True))
            alpha = jnp.exp(m_prev - m_new)
            p = jnp.exp(s - m_new)
            l_s[hh] = alpha * l_s[hh] + jnp.sum(p, axis=1, keepdims=True)
            v = v_ref[pl.ds(ks, tk), cols]
            acc_s[hh] = alpha * acc_s[hh] + jnp.dot(p.astype(BF16), v, preferred_element_type=F32)
            m_s[hh] = m_new

    def full_tile(j, c):
        tile(j, False)
        return c

    n_full = (qoff + i * tq) // tk
    lax.fori_loop(0, n_full, full_tile, 0)
    tile(n_full, True)
    for hh in range(hps):
        o_ref[:, hh * hd:(hh + 1) * hd] = (acc_s[hh] / l_s[hh]).astype(o_ref.dtype)


def fox_attention(q_all, row_off, bsz, t, k, v, tkeys, cum, cumt, qoff, scale):
    hd = LANES
    nheads = q_all.shape[1] // hd
    hps = _pick(nheads, (FOX_HEADS_PER_STEP, 2, 1))
    tq = _pick(t, (FOX_TQ, 256, 128, 64, 32, 16, 8))
    tk = tkeys if tq * tkeys * 4 <= (256 << 10) else _pick(tkeys, (tq, 256, 128))
    assert tk % tq == 0 and qoff % tq == 0 and qoff + t <= tkeys
    nq = t // tq
    rb0 = row_off // tq
    cq0 = qoff // tq
    vmem = (8 * tkeys * hps * hd * 2 + 2 * cumt.shape[2] * tkeys * 4 + 4 * hps * tq * tk * 4
            + 8 * hps * tq * hd * 4 + (2 << 20))
    return pl.pallas_call(
        functools.partial(_fox_attn_kernel, tq=tq, tk=tk, qoff=qoff, scale=scale, hps=hps),
        grid=(bsz, nheads // hps, nq),
        in_specs=[pl.BlockSpec((tq, hps * hd), lambda b, h, i: (rb0 + b * nq + i, h)),
                  pl.BlockSpec((tkeys, hps * hd), lambda b, h, i: (b, h)),
                  pl.BlockSpec((tkeys, hps * hd), lambda b, h, i: (b, h)),
                  pl.BlockSpec((1, tq, LANES), lambda b, h, i: (b, cq0 + i, 0)),
                  pl.BlockSpec((1,) + cumt.shape[1:], lambda b, h, i: (b, 0, 0, 0))],
        out_specs=pl.BlockSpec((tq, hps * hd), lambda b, h, i: (b * nq + i, h)),
        out_shape=jax.ShapeDtypeStruct((bsz * t, nheads * hd), BF16),
        scratch_shapes=[pltpu.VMEM((hps, tq, 1), F32), pltpu.VMEM((hps, tq, 1), F32), pltpu.VMEM((hps, tq, hd), F32)],
        compiler_params=_cparams(("parallel", "parallel", "arbitrary"), vmem),
        name="fox_attention",
    )(q_all, k, v, cum, cumt)


FOX_DECODE_HEADS = 8
FOX_DECODE_CHUNK = 512


def _fox_decode_kernel(q_ref, pk_ref, pv_ref, nk_ref, nv_ref, cq_ref, fkp_ref, fkn_ref, o_ref, *, t, chunk, scale):
    hg = pl.program_id(1)
    g = pk_ref.shape[2]
    hd = pk_ref.shape[3]
    rows = g * t
    nt_dims = (((1,), (1,)), ((), ()))
    stack = lambda ref: jnp.concatenate([ref[:, h * hd:(h + 1) * hd] for h in range(g)], axis=0)
    q, kn, vn = stack(q_ref), stack(nk_ref), stack(nv_ref)
    lane = lax.broadcasted_iota(jnp.int32, cq_ref.shape[1:], 1)
    cq = cq_ref[0]
    fq = jnp.concatenate([jnp.sum(jnp.where(lane == hg * g + h, cq, 0.0), axis=1, keepdims=True)
                          for h in range(g)], axis=0)

    def update(carry, s, v):
        m, l, acc = carry
        m_new = jnp.maximum(m, jnp.max(s, axis=1, keepdims=True))
        alpha = jnp.exp(m - m_new)
        p = jnp.exp(s - m_new)
        return (m_new, alpha * l + jnp.sum(p, axis=1, keepdims=True),
                alpha * acc + jnp.dot(p.astype(BF16), v, preferred_element_type=F32))

    carry = (jnp.full((rows, 1), NEG_INF, F32), jnp.zeros((rows, 1), F32), jnp.zeros((rows, hd), F32))
    cols = chunk * g
    same_head = (lax.broadcasted_iota(jnp.int32, (rows, cols), 0) // t
                 == lax.broadcasted_iota(jnp.int32, (rows, cols), 1) % g)
    for c in range(pk_ref.shape[1] // chunk):
        kp = pk_ref[0, c * chunk:(c + 1) * chunk].reshape(cols, hd).astype(BF16)
        vp = pv_ref[0, c * chunk:(c + 1) * chunk].reshape(cols, hd).astype(BF16)
        s = lax.dot_general(q, kp, nt_dims, preferred_element_type=F32) * scale
        s = s + fq - fkp_ref[0, 0, :, c * cols:(c + 1) * cols]
        carry = update(carry, jnp.where(same_head, s, NEG_INF), vp)
    rr = lax.broadcasted_iota(jnp.int32, (rows, rows), 0)
    cc = lax.broadcasted_iota(jnp.int32, (rows, rows), 1)
    s = lax.dot_general(q, kn, nt_dims, preferred_element_type=F32) * scale + fq - fkn_ref[0, 0]
    s = jnp.where(rr // t == cc // t, jnp.where(cc % t <= rr % t, s, NEG_INF), NEG_INF)
    _, l, acc = update(carry, s, vn)
    out = acc / l
    for h in range(g):
        o_ref[:, h * hd:(h + 1) * hd] = out[h * t:(h + 1) * t].astype(o_ref.dtype)


def fox_decode_attention(q_all, row_off, bsz, t, past_k, past_v, new_k, new_v, cum, scale):
    _, npast, nheads, hd = past_k.shape
    g = FOX_DECODE_HEADS
    assert nheads % g == 0 and hd == LANES and npast % t == 0
    ng = nheads // g
    chunk = _pick(npast, (FOX_DECODE_CHUNK, 256, 128, 64, 32, 16, 8))
    rb0 = row_off // t
    fkp = jnp.transpose(cum[:, :npast, :nheads].reshape(bsz, npast, ng, g), (0, 2, 1, 3)).reshape(bsz, ng, 1, npast * g)
    fkn = jnp.transpose(cum[:, npast:npast + t, :nheads].reshape(bsz, t, ng, g), (0, 2, 3, 1)).reshape(bsz, ng, 1, g * t)
    vmem = 4 * npast * g * hd * 4 + 8 * g * t * chunk * g * 4 + 4 * chunk * g * hd * 2 + (4 << 20)
    return pl.pallas_call(
        functools.partial(_fox_decode_kernel, t=t, chunk=chunk, scale=scale),
        grid=(bsz, ng),
        in_specs=[pl.BlockSpec((t, g * hd), lambda b, h: (rb0 + b, h)),
                  pl.BlockSpec((1, npast, g, hd), lambda b, h: (b, 0, h, 0)),
                  pl.BlockSpec((1, npast, g, hd), lambda b, h: (b, 0, h, 0)),
                  pl.BlockSpec((t, g * hd), lambda b, h: (b, h)),
                  pl.BlockSpec((t, g * hd), lambda b, h: (b, h)),
                  pl.BlockSpec((1, t, LANES), lambda b, h: (b, npast // t, 0)),
                  pl.BlockSpec((1, 1, 1, npast * g), lambda b, h: (b, h, 0, 0)),
                  pl.BlockSpec((1, 1, 1, g * t), lambda b, h: (b, h, 0, 0))],
        out_specs=pl.BlockSpec((t, g * hd), lambda b, h: (b, h)),
        out_shape=jax.ShapeDtypeStruct((bsz * t, nheads * hd), BF16),
        compiler_params=_cparams(("parallel", "parallel"), vmem),
        name="fox_decode_attention",
    )(q_all, past_k, past_v, new_k, new_v, cum, fkp, fkn)


def _rwkv_prep_kernel(z_ref, zp_ref, sh_ref, mu_ref, w0_ref, a0_ref, w2_ref, a2_ref, g2_ref,
                      r_ref, k_ref, v_ref, lw_ref, a_ref, g_ref, *, w):
    i = pl.program_id(1)
    z = z_ref[...]
    prev_row = jnp.where(i == 0, sh_ref[0], zp_ref[7:8, :])
    row = lax.broadcasted_iota(jnp.int32, z.shape, 0)
    z_prev = jnp.where(row == 0, prev_row, pltpu.roll(z, 1, 0))
    zm = z + (z_prev - z) * mu_ref[...]
    slab = zm[:, 3 * w:]
    lora_w = jnp.dot(jnp.tanh(slab).astype(BF16), w2_ref[...], preferred_element_type=F32)
    x = -(w0_ref[...] + lora_w)
    softplus = jnp.maximum(x, 0.0) + jnp.log1p(jnp.exp(-jnp.abs(x)))
    lw = -jnp.exp(-softplus - 0.5)
    a = jax.nn.sigmoid(a0_ref[...] + jnp.dot(slab.astype(BF16), a2_ref[...], preferred_element_type=F32))
    g = jnp.dot(jax.nn.sigmoid(slab).astype(BF16), g2_ref[...], preferred_element_type=F32)
    nh, hd = r_ref.shape[1], r_ref.shape[3]
    for ref, val in ((r_ref, zm[:, 0:w]), (k_ref, zm[:, w:2 * w]), (v_ref, zm[:, 2 * w:3 * w]),
                     (lw_ref, lw), (a_ref, a), (g_ref, g)):
        for hh in range(nh):
            ref[0, hh] = val[:, hh * hd:(hh + 1) * hd]


def rwkv_prep(zr, row_off, bsz, t, shift_prev, prm, nh):
    wz = zr.shape[1]
    w = prm["w0"].shape[1]
    hd = w // nh
    tt = _pick(t, (128, 64, 32, 16, 8))
    nt = t // tt
    rb0 = row_off // tt
    full = lambda shape: pl.BlockSpec(shape, lambda b, i: (0,) * len(shape))
    ospec = pl.BlockSpec((1, nh, tt, hd), lambda b, i: (b, 0, i, 0))
    oshape = jax.ShapeDtypeStruct((bsz, nh, t, hd), F32)
    ls = wz - 3 * w
    vmem = 6 * tt * wz * 4 + 12 * tt * nh * LANES * 4 + 6 * ls * w * 2 + 8 * tt * w * 4
    return pl.pallas_call(
        functools.partial(_rwkv_prep_kernel, w=w),
        grid=(bsz, nt),
        in_specs=[pl.BlockSpec((tt, wz), lambda b, i: (rb0 + b * nt + i, 0)),
                  pl.BlockSpec((8, wz), lambda b, i: (jnp.maximum((row_off + (b * nt + i) * tt) // 8 - 1, 0), 0)),
                  pl.BlockSpec((1, 1, wz), lambda b, i: (b, 0, 0)),
                  full((1, wz)), full((1, w)), full((1, w)), full((ls, w)), full((ls, w)), full((ls, w))],
        out_specs=(ospec,) * 6,
        out_shape=(oshape,) * 6,
        compiler_params=_cparams(("parallel", "arbitrary"), vmem),
        name="rwkv_prep",
    )(zr, zr, shift_prev.reshape(bsz, 1, wz), prm["mu"], prm["w0"], prm["a0"], prm["w2"], prm["a2"], prm["g2"])


def _bdot(a, b, dims):
    return lax.dot_general(a.astype(BF16), b.astype(BF16), dims, preferred_element_type=F32)


def _cumsum_rows(tri, x):
    hi = x.astype(BF16)
    r1 = x - hi.astype(F32)
    mid = r1.astype(BF16)
    lo = (r1 - mid.astype(F32)).astype(BF16)
    t = tri.astype(BF16)
    dot = lambda p: lax.dot_general(t, p, _NN, preferred_element_type=F32)
    return dot(hi) + dot(mid) + dot(lo)


_NT = (((2,), (2,)), ((0,), (0,)))
_NN = (((2,), (1,)), ((0,), (0,)))
_TN = (((1,), (1,)), ((0,), (0,)))


def _rwkv_scan_kernel(r_ref, k_ref, v_ref, lw_ref, a_ref, g_ref, s0_ref, kk_ref, ka_ref, rk_ref,
                      lnw_ref, lnb_ref, y_ref, sout_ref, st_s, *, chunk):
    c = pl.program_id(2)

    @pl.when(c == 0)
    def _():
        st_s[...] = s0_ref[0]

    r = r_ref[0]
    k = k_ref[0]
    v = v_ref[0]
    lw = lw_ref[0]
    a = a_ref[0]
    hb = r.shape[0]
    s0 = st_s[...]

    kk = k * kk_ref[...]
    kk = kk / jnp.maximum(jnp.sqrt(jnp.sum(kk * kk, axis=-1, keepdims=True)), 1e-12)
    kmod = k * (1.0 + (a - 1.0) * ka_ref[...])

    li = lax.broadcasted_iota(jnp.int32, (chunk, chunk), 0)
    mi = lax.broadcasted_iota(jnp.int32, (chunk, chunk), 1)
    incl = (mi <= li).astype(F32)
    strict = (mi < li).astype(F32)
    cs = _cumsum_rows(jnp.broadcast_to(incl, (hb, chunk, chunk)), lw)
    dec_in = jnp.exp(cs)
    dec_ex = jnp.exp(cs - lw)
    inv = jnp.exp(-cs)
    p_rows = jnp.concatenate([-kk * dec_ex, r * dec_in], axis=1)
    q_rows = jnp.concatenate([kk * a * inv, kmod * inv], axis=1)
    mm = _bdot(p_rows, q_rows, _NT)
    a_ab = mm[:, :chunk, :chunk] * strict
    a_ak = mm[:, :chunk, chunk:] * strict
    r_b = mm[:, chunk:, :chunk] * incl
    r_k = mm[:, chunk:, chunk:] * incl
    ps = _bdot(p_rows, s0, _NT)
    x = ps[:, :chunk] + _bdot(a_ak, v, _NN)
    pw = a_ab
    n_iter = chunk.bit_length() - 1
    for it in range(n_iter):
        x = x + _bdot(pw, x, _NN)
        if it + 1 < n_iter:
            pw = _bdot(pw, pw, _NN)
    y = ps[:, chunk:] + _bdot(r_b, x, _NN) + _bdot(r_k, v, _NN)
    uv = jnp.concatenate([x, v], axis=1)
    s_new = (s0 + _bdot(uv, q_rows, _TN)) * dec_in[:, chunk - 1:chunk, :]
    st_s[...] = s_new
    sout_ref[0] = s_new

    mean = jnp.mean(y, axis=-1, keepdims=True)
    var = jnp.mean(jnp.square(y - mean), axis=-1, keepdims=True)
    yn = (y - mean) * lax.rsqrt(var + RWKV_LN_EPS) * lnw_ref[...] + lnb_ref[...]
    bonus = jnp.sum(r * kmod * rk_ref[...], axis=-1, keepdims=True) * v
    out = (yn + bonus) * g_ref[0]
    hd = out.shape[2]
    for hh in range(hb):
        y_ref[:, hh * hd:(hh + 1) * hd] = out[hh].astype(y_ref.dtype)


def rwkv_scan(r, k, v, lw, a, g, s0, prm):
    bsz, nh, t, hd = r.shape
    chunk = _pick(t, (64, 32, 16, 8))
    hb = _pick(nh, (32, 16, 8, 4, 2, 1))
    nc = t // chunk
    xspec = pl.BlockSpec((1, hb, chunk, hd), lambda b, h, c: (b, h, c, 0))
    pspec = pl.BlockSpec((hb, 1, hd), lambda b, h, c: (h, 0, 0))
    sspec = pl.BlockSpec((1, hb, hd, hd), lambda b, h, c: (b, h, 0, 0))
    vmem = 16 * hb * chunk * LANES * 4 + 40 * hb * 2 * chunk * LANES * 4 + 6 * hb * hd * LANES * 4
    y, s = pl.pallas_call(
        functools.partial(_rwkv_scan_kernel, chunk=chunk),
        grid=(bsz, nh // hb, nc),
        in_specs=[xspec] * 6 + [sspec] + [pspec] * 5,
        out_specs=(pl.BlockSpec((chunk, hb * hd), lambda b, h, c: (b * nc + c, h)), sspec),
        out_shape=(jax.ShapeDtypeStruct((bsz * t, nh * hd), BF16), jax.ShapeDtypeStruct((bsz, nh, hd, hd), F32)),
        scratch_shapes=[pltpu.VMEM((hb, hd, hd), F32)],
        compiler_params=_cparams(("parallel", "parallel", "arbitrary"), vmem),
        name="rwkv_scan",
    )(r, k, v, lw, a, g, s0, prm["k_k"], prm["k_a"], prm["r_k"], prm["ln_w"], prm["ln_b"])
    return y, s


def _router_kernel(x_ref, w_ref, comb_ref, *, n_experts):
    logits = jnp.dot(x_ref[...].astype(BF16), w_ref[...], preferred_element_type=F32)
    lane = lax.broadcasted_iota(jnp.int32, logits.shape, 1)
    big = jnp.int32(LANES)
    logits = jnp.where(lane < n_experts, logits, -jnp.inf)
    t1 = jnp.max(logits, axis=1, keepdims=True)
    i1 = jnp.min(jnp.where(logits == t1, lane, big), axis=1, keepdims=True)
    rest = jnp.where(lane == i1, -jnp.inf, logits)
    t2 = jnp.max(rest, axis=1, keepdims=True)
    i2 = jnp.min(jnp.where(rest == t2, lane, big), axis=1, keepdims=True)
    e2 = jnp.exp(t2 - t1)
    den = 1.0 + e2
    comb_ref[...] = jnp.where(lane == 0, 1.0 / den, jnp.where(lane == 1, e2 / den, jnp.where(
        lane == 2, i1.astype(F32), jnp.where(lane == 3, i2.astype(F32), 0.0))))


def moe_router(x, w_router):
    m, d = x.shape
    ne = w_router.shape[1]
    wp = jnp.zeros((d, LANES), BF16).at[:, :ne].set(w_router.astype(BF16))
    tm = _pick(m, (640, 512, 320, 256, 128, 64, 32, 16, 8))
    return pl.pallas_call(
        functools.partial(_router_kernel, n_experts=ne),
        grid=(m // tm,),
        in_specs=[pl.BlockSpec((tm, d), lambda i: (i, 0)), pl.BlockSpec((d, LANES), lambda i: (0, 0))],
        out_specs=pl.BlockSpec((tm, LANES), lambda i: (i, 0)),
        out_shape=jax.ShapeDtypeStruct((m, LANES), F32),
        compiler_params=_cparams(("parallel",), 3 * tm * d * 4 + 2 * d * LANES * 2 + 8 * tm * LANES * 4),
        name="moe_router",
    )(x, wp)


MOE_TILE = 512


def _moe_plan(rout, n_exp, tile):
    n = rout.shape[0]
    pair_e = rout[:, 2:4].astype(jnp.int32).reshape(-1)
    onehot = (pair_e[:, None] == jnp.arange(n_exp, dtype=jnp.int32)[None, :]).astype(jnp.int32)
    csum = jnp.cumsum(onehot, axis=0)
    rank = jnp.take_along_axis(csum - onehot, pair_e[:, None], axis=1)[:, 0]
    gsz = (csum[-1] + tile - 1) // tile * tile
    gend = jnp.cumsum(gsz)
    slot = ((gend - gsz)[pair_e] + rank).astype(jnp.int32)
    n_tiles = -(-2 * n // tile) + n_exp
    tok = jnp.zeros((n_tiles * tile,), jnp.int32).at[slot].set(jnp.arange(2 * n, dtype=jnp.int32) // 2)
    tile_start = jnp.arange(n_tiles, dtype=jnp.int32) * tile
    te = jnp.minimum(jnp.searchsorted(gend, tile_start, side="right"), n_exp - 1).astype(jnp.int32)
    used = (gend[-1] // tile).astype(jnp.int32).reshape(1)
    return slot, tok, te, used, n_tiles


def _row_copy(src_hbm, row, dst, r, sem):
    return pltpu.make_async_copy(src_hbm.at[pl.ds(row, 1)], dst.at[pl.ds(r, 1)], sem)


def _moe_gather_kernel(tok_ref, used_ref, x_hbm, o_ref, buf, sem, *, tile):
    i = pl.program_id(0)
    used = used_ref[0]

    def issue_tile(t_idx):
        slot = t_idx % 2
        base = t_idx * tile

        def issue(r, c):
            _row_copy(x_hbm, tok_ref[base + r], buf.at[slot], r, sem.at[slot]).start()
            return c

        lax.fori_loop(0, tile, issue, 0)

    @pl.when((i == 0) & (used > 0))
    def _():
        issue_tile(i)

    @pl.when(i + 1 < used)
    def _():
        issue_tile(i + 1)

    @pl.when(i < used)
    def _():
        slot = i % 2

        def wait(r, c):
            _row_copy(x_hbm, 0, buf.at[slot], r, sem.at[slot]).wait()
            return c

        lax.fori_loop(0, tile, wait, 0)
        o_ref[...] = buf[slot].astype(o_ref.dtype)

    @pl.when(i >= used)
    def _():
        o_ref[...] = jnp.zeros(o_ref.shape, o_ref.dtype)


def moe_gather(x, tok, used, n_tiles, tile):
    d = x.shape[1]
    return pl.pallas_call(
        functools.partial(_moe_gather_kernel, tile=tile),
        grid_spec=pltpu.PrefetchScalarGridSpec(
            num_scalar_prefetch=2, grid=(n_tiles,),
            in_specs=[pl.BlockSpec(memory_space=pl.ANY)],
            out_specs=pl.BlockSpec((tile, d), lambda i, tok_r, used_r: (i, 0)),
            scratch_shapes=[pltpu.VMEM((2, tile, d), x.dtype), pltpu.SemaphoreType.DMA((2,))]),
        out_shape=jax.ShapeDtypeStruct((n_tiles * tile, d), BF16),
        compiler_params=_cparams(("arbitrary",), 5 * tile * d * 4),
        name="moe_gather",
    )(tok, used, x)


def _moe_up_kernel(te_ref, used_ref, x_ref, wg_ref, wu_ref, wd_ref, o_ref, wdo_ref, wg_s, wu_s):
    i = pl.program_id(1)
    wdo_ref[...] = wd_ref[...].astype(BF16)

    @pl.when((i == 0) | (te_ref[i] != te_ref[jnp.maximum(i - 1, 0)]))
    def _():
        wg_s[...] = wg_ref[...].astype(BF16)
        wu_s[...] = wu_ref[...].astype(BF16)

    @pl.when(i < used_ref[0])
    def _():
        x = x_ref[...]
        g = jnp.dot(x, wg_s[...], preferred_element_type=F32)
        u = jnp.dot(x, wu_s[...], preferred_element_type=F32)
        o_ref[...] = (g * jax.nn.sigmoid(g) * u).astype(o_ref.dtype)

    @pl.when(i >= used_ref[0])
    def _():
        o_ref[...] = jnp.zeros(o_ref.shape, o_ref.dtype)


def moe_up(xs, wg, wu, wd, te, used, tile):
    p_rows, d = xs.shape
    n_exp, f = wg.shape[0], wg.shape[2]
    tn = _pick(f, (512, 256, 128))
    nj, ni = f // tn, p_rows // tile
    wd_rows = n_exp * f
    rows_c = next(c for c in (16, 32, 64, 128, 256, 512, 1024, 2048, 4096, wd_rows)
                  if wd_rows % c == 0 and wd_rows // c <= nj * ni)
    last_c = wd_rows // rows_c - 1
    cspec = pl.BlockSpec((rows_c, wd.shape[2]), lambda j, i, te_r, used_r: (jnp.minimum(j * ni + i, last_c), 0))
    wspec = pl.BlockSpec((None, d, tn), lambda j, i, te_r, used_r: (te_r[i], 0, j))
    vmem = 2 * tile * d * 2 + 4 * d * tn * 4 + 2 * d * tn * 2 + 6 * tile * tn * 4 + 12 * rows_c * wd.shape[2]
    act, wd_b = pl.pallas_call(
        _moe_up_kernel,
        grid_spec=pltpu.PrefetchScalarGridSpec(
            num_scalar_prefetch=2, grid=(nj, ni),
            in_specs=[pl.BlockSpec((tile, d), lambda j, i, te_r, used_r: (i, 0)), wspec, wspec, cspec],
            out_specs=(pl.BlockSpec((tile, tn), lambda j, i, te_r, used_r: (i, j)), cspec),
            scratch_shapes=[pltpu.VMEM((d, tn), BF16), pltpu.VMEM((d, tn), BF16)]),
        out_shape=(jax.ShapeDtypeStruct((p_rows, f), BF16), jax.ShapeDtypeStruct((wd_rows, wd.shape[2]), BF16)),
        compiler_params=_cparams(("arbitrary", "arbitrary"), vmem),
        name="moe_up",
    )(te, used, xs, wg, wu, wd.reshape(wd_rows, wd.shape[2]))
    return act, wd_b.reshape(wd.shape)


def _moe_down_kernel(te_ref, used_ref, x_ref, w_ref, o_ref, *, per_tile):
    i = pl.program_id(1)

    @pl.when(i < used_ref[0] * per_tile)
    def _():
        o_ref[...] = jnp.dot(x_ref[...], w_ref[...], preferred_element_type=F32)

    @pl.when(i >= used_ref[0] * per_tile)
    def _():
        o_ref[...] = jnp.zeros(o_ref.shape, o_ref.dtype)


def moe_down(act, wd, te, used, tile):
    p_rows, f = act.shape
    d = wd.shape[2]
    tn = _pick(d, (512, 256, 128))
    rows = _pick(tile, (256, 128, 64, 32, 16, 8))
    per_tile = tile // rows
    vmem = 2 * rows * f * 2 + 2 * f * tn * 2 + 4 * rows * tn * 4
    return pl.pallas_call(
        functools.partial(_moe_down_kernel, per_tile=per_tile),
        grid_spec=pltpu.PrefetchScalarGridSpec(
            num_scalar_prefetch=2, grid=(d // tn, p_rows // rows),
            in_specs=[pl.BlockSpec((rows, f), lambda j, i, te_r, used_r: (i, 0)),
                      pl.BlockSpec((None, f, tn), lambda j, i, te_r, used_r: (te_r[i // per_tile], 0, j))],
            out_specs=pl.BlockSpec((rows, tn), lambda j, i, te_r, used_r: (i, j))),
        out_shape=jax.ShapeDtypeStruct((p_rows, d), F32),
        compiler_params=_cparams(("parallel", "arbitrary"), vmem),
        name="moe_down",
    )(te, used, act, wd)


def _moe_combine_kernel(slot_ref, h_ref, g_ref, ys_hbm, fn_ref, o_ref, buf, sem, *, tc, rb0, nsteps):
    i = pl.program_id(0)

    def issue_tile(t_idx):
        bs = t_idx % 2
        base = (rb0 + t_idx) * tc

        def issue(r, c):
            p = 2 * (base + r)
            _row_copy(ys_hbm, slot_ref[p], buf.at[bs, 0], r, sem.at[bs]).start()
            _row_copy(ys_hbm, slot_ref[p + 1], buf.at[bs, 1], r, sem.at[bs]).start()
            return c

        lax.fori_loop(0, tc, issue, 0)

    @pl.when(i == 0)
    def _():
        issue_tile(i)

    @pl.when(i + 1 < nsteps)
    def _():
        issue_tile(i + 1)

    bs = i % 2

    def wait(r, c):
        _row_copy(ys_hbm, 0, buf.at[bs, 0], r, sem.at[bs]).wait()
        _row_copy(ys_hbm, 0, buf.at[bs, 1], r, sem.at[bs]).wait()
        return c

    lax.fori_loop(0, tc, wait, 0)
    g = g_ref[...]
    x = h_ref[...] + (g[:, 0:1] * buf[bs, 0] + g[:, 1:2] * buf[bs, 1])
    y = x * lax.rsqrt(jnp.mean(x * x, axis=-1, keepdims=True) + NORM_EPS)
    o_ref[...] = y * fn_ref[...]


def moe_combine_norm(h, rout, ys, slot, final_norm, row0, n):
    d = h.shape[1]
    tc = _pick(math.gcd(n, row0) if row0 else n, (256, 128, 64, 32, 16, 8))
    rb0 = row0 // tc
    return pl.pallas_call(
        functools.partial(_moe_combine_kernel, tc=tc, rb0=rb0, nsteps=n // tc),
        grid_spec=pltpu.PrefetchScalarGridSpec(
            num_scalar_prefetch=1, grid=(n // tc,),
            in_specs=[pl.BlockSpec((tc, d), lambda i, s: (rb0 + i, 0)),
                      pl.BlockSpec((tc, LANES), lambda i, s: (rb0 + i, 0)),
                      pl.BlockSpec(memory_space=pl.ANY),
                      pl.BlockSpec((1, d), lambda i, s: (0, 0))],
            out_specs=pl.BlockSpec((tc, d), lambda i, s: (i, 0)),
            scratch_shapes=[pltpu.VMEM((2, 2, tc, d), F32), pltpu.SemaphoreType.DMA((2,))]),
        out_shape=jax.ShapeDtypeStruct((n, d), F32),
        compiler_params=_cparams(("arbitrary",), 10 * tc * d * 4),
        name="moe_combine_norm",
    )(slot, h, rout, ys, final_norm.astype(F32).reshape(1, d))


def _rope_tables(pos, rope, width, lead):
    inv = ROPE_THETA ** (-jnp.arange(0, rope, 2, dtype=F32) / rope)
    ang = pos.astype(F32)[:, None] * inv[None, :]
    cos, sin = jnp.cos(ang), jnp.sin(ang)
    n = pos.shape[0]
    ctab = jnp.concatenate([jnp.ones((n, lead), F32), cos, cos, jnp.zeros((n, width - lead - rope), F32)], axis=1)
    stab = jnp.concatenate([jnp.zeros((n, lead), F32), -sin, sin, jnp.zeros((n, width - lead - rope), F32)], axis=1)
    return ctab, stab


def _swap_halves(w):
    half = w.shape[-1] // 2
    return jnp.concatenate([w[..., half:], w[..., :half]], axis=-1)


def kernel(x_prompt, x_sample, cache_mla_ckv, cache_mla_kpe, state_s5_re, state_s5_im, state_rwkv_wkv, state_rwkv_shift, cache_fox_k, cache_fox_v, cache_fox_logf, ln0_mix, w_in0, s5_a_re, s5_a_im, s5_log_dt, s5_b_re, s5_b_im, s5_c_re, s5_c_im, s5_d, s5_w_glu, s5_b_glu, mla_q_norm, mla_w_q_up, mla_kv_norm, mla_w_uk, mla_w_uv, w_out0, ln0_ffn, ffn_w_gate, ffn_w_up, ffn_w_down, ln1_mix, w_in1, rwkv_mu, rwkv_w0, rwkv_w2, rwkv_a0, rwkv_a2, rwkv_g2, rwkv_k_k, rwkv_k_a, rwkv_r_k, rwkv_ln_w, rwkv_ln_b, fox_b_f, w_out1, ln1_ffn, moe_w_router, moe_w_gate, moe_w_up, moe_w_down, final_norm):
    bp, tp, d = x_prompt.shape
    bs, ts, _ = x_sample.shape
    past = cache_mla_ckv.shape[1]
    n_p, n_s = bp * tp, bs * ts
    ntok = n_p + n_s
    streams = ((0, bp, tp), (n_p, bs, ts))

    h = jnp.concatenate([x_prompt.reshape(n_p, d), x_sample.reshape(n_s, d)], axis=0)

    s5_w = s5_d.shape[0]
    q_rank = mla_q_norm.shape[0]
    kv_rank = mla_kv_norm.shape[0]
    n_mh, qk = mla_w_q_up.shape[1], mla_w_q_up.shape[2]
    nope = mla_w_uk.shape[2]
    rope = qk - nope
    vdim = mla_w_uv.shape[2]
    mla_scale = float(qk) ** -0.5
    qw = 2 * LANES
    kvw = kv_rank + LANES

    (xn,) = rmsnorm(h, ln0_mix, (BF16,))
    w_in0b = w_in0.astype(BF16)
    (z_uq,) = matmul(xn, w_in0b[:, :s5_w + q_rank])
    off_kv = s5_w + q_rank
    w_kpe = w_in0b[:, off_kv + kv_rank:]
    zpad = jnp.zeros((d, kvw - kv_rank - rope), BF16)
    w_kv1 = jnp.concatenate([w_in0b[:, off_kv:off_kv + kv_rank], w_kpe, zpad], axis=1)
    w_kv2 = jnp.concatenate([jnp.zeros((d, kv_rank), BF16), _swap_halves(w_kpe), zpad], axis=1)
    pos_p = jnp.arange(tp)
    pos_s = past + jnp.arange(ts)

    def token_tables(width, lead):
        cp, sp = _rope_tables(pos_p, rope, width, lead)
        cs, ss = _rope_tables(pos_s, rope, width, lead)
        return (jnp.concatenate([jnp.tile(cp, (bp, 1)), jnp.tile(cs, (bs, 1))], axis=0),
                jnp.concatenate([jnp.tile(sp, (bp, 1)), jnp.tile(ss, (bs, 1))], axis=0))

    ckv_c, ckv_s = token_tables(kvw, kv_rank)
    z_kv = rope_matmul(xn, w_kv1, w_kv2, ckv_c, ckv_s, F32, tn=kvw)
    ckv_f, ckv_b = rmsnorm(z_kv, mla_kv_norm, (F32, BF16), col_block=0, width=kv_rank)
    kpe_f = z_kv[:, kv_rank:kv_rank + rope]
    kpe_b = z_kv[:, kv_rank:].astype(BF16)

    (cqn,) = rmsnorm(z_uq, mla_q_norm, (BF16,), col_block=s5_w // q_rank, width=q_rank)
    wq = mla_w_q_up.astype(BF16)
    zq = jnp.zeros((q_rank, n_mh, qw - qk), BF16)
    wq1 = jnp.concatenate([wq, zq], axis=2).reshape(q_rank, n_mh * qw)
    wq2 = jnp.concatenate([jnp.zeros((q_rank, n_mh, nope), BF16), _swap_halves(wq[:, :, nope:]), zq],
                          axis=2).reshape(q_rank, n_mh * qw)
    q_c, q_s = token_tables(qw, nope)
    q_all = rope_matmul(cqn, wq1, wq2, q_c, q_s, BF16, tn=qw)

    s5p = s5_params(s5_a_re, s5_a_im, s5_log_dt, s5_b_re, s5_b_im, s5_c_re, s5_c_im, s5_d, s5_w_glu, s5_b_glu)
    g5, p5 = s5_a_re.shape
    wuk = jnp.transpose(mla_w_uk, (1, 2, 0)).astype(BF16)
    wuv = jnp.transpose(mla_w_uv, (1, 0, 2)).astype(BF16)

    y_s5, s5_re, s5_im, y_mla = [], [], [], []
    for si, (off, bsz, t) in enumerate(streams):
        if si == 0:
            h0r = jnp.zeros((bsz, g5 * p5), F32)
            h0i = h0r
            ckv_k = ckv_b[off:off + bsz * t].reshape(bsz, t, kv_rank)
            kpe_k = kpe_b[off:off + bsz * t].reshape(bsz, t, LANES)
            klen, causal = t, True
        else:
            h0r, h0i = state_s5_re.astype(F32), state_s5_im.astype(F32)
            klen, causal = past + t, False
            padk = -(-klen // LANES) * LANES - klen
            ckv_k = jnp.concatenate([cache_mla_ckv.astype(BF16), ckv_b[off:off + bsz * t].reshape(bsz, t, kv_rank),
                                     jnp.zeros((bsz, padk, kv_rank), BF16)], axis=1)
            kpe_cache = jnp.concatenate([cache_mla_kpe.astype(BF16),
                                         jnp.zeros((bsz, past, LANES - rope), BF16)], axis=2)
            kpe_k = jnp.concatenate([kpe_cache, kpe_b[off:off + bsz * t].reshape(bsz, t, LANES),
                                     jnp.zeros((bsz, padk, LANES), BF16)], axis=1)
        ys, hr, hi = s5_mixer(z_uq, off, bsz, t, h0r, h0i, s5p)
        y_s5.append(ys)
        s5_re.append(hr.reshape(bsz, g5, p5))
        s5_im.append(hi.reshape(bsz, g5, p5))
        y_mla.append(mla_attention(q_all, off, bsz, t, ckv_k, kpe_k, wuk, wuv, klen, causal, mla_scale))

    mix0 = jnp.concatenate([jnp.concatenate(y_s5, axis=0), jnp.concatenate(y_mla, axis=0)], axis=1)
    (h,) = matmul(mix0, w_out0.astype(BF16), res=h)
    (hn,) = rmsnorm(h, ln0_ffn, (BF16,))
    act = swiglu_up(hn, ffn_w_gate, ffn_w_up)
    (h,) = matmul(act, ffn_w_down.astype(BF16), res=h, tm=_pick(ntok, (640, 512, 256, 128, 64, 32, 16, 8)),
                  tn=256, tk=ffn_w_down.shape[0])

    rw = rwkv_w0.shape[0]
    nh_r, hd_r = rwkv_k_k.shape
    shift_w = rwkv_mu.shape[0]
    lora_w = shift_w - 3 * rw
    slab = -(-lora_w // LANES) * LANES
    wz = 3 * rw + slab
    nh_f = fox_b_f.shape[0]
    fw = (w_in1.shape[1] - shift_w - nh_f) // 3
    fox_scale = float(fw // nh_f) ** -0.5
    d_lw, d_la = rwkv_w2.shape[0], rwkv_a2.shape[0]

    (xn,) = rmsnorm(h, ln1_mix, (BF16,))
    w_in1b = w_in1.astype(BF16)
    (zr,) = matmul(xn, jnp.pad(w_in1b[:, :shift_w], ((0, 0), (0, wz - shift_w))))
    (fq,) = matmul(xn, w_in1b[:, shift_w:shift_w + fw], (BF16,))
    fk = [matmul_heads(xn, w_in1b[:, shift_w + fw:shift_w + 2 * fw], nh_f, (o_, b_ * t_)) for o_, b_, t_ in streams]
    fv = [matmul_heads(xn, w_in1b[:, shift_w + 2 * fw:shift_w + 3 * fw], nh_f, (o_, b_ * t_)) for o_, b_, t_ in streams]
    (zf,) = matmul(xn, jnp.pad(w_in1b[:, shift_w + 3 * fw:], ((0, 0), (0, LANES - nh_f))))

    padrow = lambda wgt, lo: jnp.zeros((slab, rw), BF16).at[lo:lo + wgt.shape[0]].set(wgt.astype(BF16))
    row2 = lambda x_, n_: x_.astype(F32).reshape(1, n_)
    head3 = lambda x_: x_.astype(F32).reshape(nh_r, 1, hd_r)
    rprm = dict(mu=jnp.pad(row2(rwkv_mu, shift_w), ((0, 0), (0, wz - shift_w))), w0=row2(rwkv_w0, rw),
                a0=row2(rwkv_a0, rw), w2=padrow(rwkv_w2, 0), a2=padrow(rwkv_a2, d_lw),
                g2=padrow(rwkv_g2, d_lw + d_la), k_k=head3(rwkv_k_k), k_a=head3(rwkv_k_a), r_k=head3(rwkv_r_k),
                ln_w=head3(rwkv_ln_w), ln_b=head3(rwkv_ln_b))

    y_r, wkv, shift_new, y_f, logf_out = [], [], [], [], []
    for si, (off, bsz, t) in enumerate(streams):
        if si == 0:
            shift_prev = jnp.zeros((bsz, wz), F32)
            s0 = jnp.zeros((bsz, nh_r, hd_r, hd_r), F32)
            pre = None
            k_all, v_all, tkeys = fk[si][1], fv[si][1], t
            qoff = 0
        else:
            shift_prev = jnp.pad(state_rwkv_shift.astype(F32), ((0, 0), (0, wz - shift_w)))
            s0 = state_rwkv_wkv.astype(F32)
            pre = jnp.pad(cache_fox_logf.astype(F32), ((0, 0), (0, 0), (0, LANES - nh_f)))
        parts = rwkv_prep(zr, off, bsz, t, shift_prev, rprm, nh_r)
        yr, s_fin = rwkv_scan(*parts, s0, rprm)
        y_r.append(yr)
        wkv.append(s_fin)
        shift_new.append(zr[off + t - 1:off + bsz * t:t, :shift_w])
        logf, cum, cumt = fox_gate(zf, off, bsz, t, fox_b_f, pre)
        logf_out.append(logf[:, :, :nh_f])
        if si == 0:
            y_f.append(fox_attention(fq, off, bsz, t, k_all, v_all, tkeys, cum, cumt, qoff, fox_scale))
        else:
            y_f.append(fox_decode_attention(fq, off, bsz, t, cache_fox_k.astype(F32), cache_fox_v.astype(F32),
                                            fk[si][1], fv[si][1], cum, fox_scale))

    mix1 = jnp.concatenate([jnp.concatenate(y_r, axis=0), jnp.concatenate(y_f, axis=0)], axis=1)
    (h,) = matmul(mix1, w_out1.astype(BF16), res=h)
    (hn,) = rmsnorm(h, ln1_ffn, (F32,))
    rout = moe_router(hn, moe_w_router)
    n_exp = moe_w_gate.shape[0]
    slot, tok, te, used, n_tiles = _moe_plan(rout, n_exp, MOE_TILE)
    xs = moe_gather(hn, tok, used, n_tiles, MOE_TILE)
    act, wd_b = moe_up(xs, moe_w_gate, moe_w_up, moe_w_down, te, used, MOE_TILE)
    ys = moe_down(act, wd_b, te, used, MOE_TILE)
    nfh = fw // nh_f
    outs = [moe_combine_norm(h, rout, ys, slot, final_norm, off, bsz * t).reshape(bsz, t, d) for off, bsz, t in streams]
    for si, (off, bsz, t) in enumerate(streams):
        rows = slice(off, off + bsz * t)
        outs += [ckv_f[rows].reshape(bsz, t, kv_rank), kpe_f[rows].reshape(bsz, t, rope), s5_re[si], s5_im[si],
                 wkv[si], shift_new[si], fk[si][0].reshape(bsz, t, nh_f, nfh), fv[si][0].reshape(bsz, t, nh_f, nfh),
                 logf_out[si]]
    return tuple(outs)
```

```python
import functools
import math

import jax
import jax.numpy as jnp
from jax import lax
from jax.experimental import pallas as pl
from jax.experimental.pallas import tpu as pltpu

F32 = jnp.float32
BF16 = jnp.bfloat16

V7X_VMEM_BYTES = 64 * 1024 * 1024
VMEM_CAP = V7X_VMEM_BYTES - 4 * 1024 * 1024
LANES = 128

NORM_EPS = 1e-6
NEG_INF = -1e30
CHUNK = 64
ROPE_THETA = 10000.0
RWKV_LN_EPS = 64e-5
S5_BLOCK_GROUPS = 8


def _pick(n, cands):
    for c in cands:
        if c <= n and n % c == 0:
            return c
    return n


def _cparams(sem, vmem_bytes):
    limit = int(min(max(vmem_bytes * 1.25 + (4 << 20), 24 << 20), VMEM_CAP))
    return pltpu.CompilerParams(dimension_semantics=sem, vmem_limit_bytes=limit)


def _rmsnorm_kernel(x_ref, g_ref, *o_refs):
    x = x_ref[...].astype(F32)
    y = x * lax.rsqrt(jnp.mean(x * x, axis=-1, keepdims=True) + NORM_EPS)
    y = y * g_ref[...]
    for o in o_refs:
        o[...] = y.astype(o.dtype)


def rmsnorm(x, g, out_dtypes, col_block=0, width=None):
    m = x.shape[0]
    width = x.shape[1] if width is None else width
    tm = _pick(m, (512, 320, 256, 128, 64, 32, 16, 8))
    outs = tuple(jax.ShapeDtypeStruct((m, width), d) for d in out_dtypes)
    res = pl.pallas_call(
        _rmsnorm_kernel,
        grid=(m // tm,),
        in_specs=[pl.BlockSpec((tm, width), lambda i: (i, col_block)),
                  pl.BlockSpec((1, width), lambda i: (0, 0))],
        out_specs=tuple(pl.BlockSpec((tm, width), lambda i: (i, 0)) for _ in out_dtypes),
        out_shape=outs,
        compiler_params=_cparams(("parallel",), tm * width * 4 * 2 * (1 + len(out_dtypes))),
        name="rmsnorm",
    )(x, g.reshape(1, width).astype(F32))
    return res


def _mm_kernel(*refs, nk, has_res, has_scale, n_out):
    x_ref, w_ref = refs[0], refs[1]
    pos = 2
    res_ref = scale_ref = None
    if has_res:
        res_ref = refs[pos]
        pos += 1
    if has_scale:
        scale_ref = refs[pos]
        pos += 1
    o_refs = refs[pos:pos + n_out]
    acc_ref = refs[pos + n_out] if nk > 1 else None

    part = jnp.dot(x_ref[...].astype(BF16), w_ref[...].astype(BF16), preferred_element_type=F32)

    def finish(acc):
        if has_scale:
            acc = acc * scale_ref[...]
        if has_res:
            acc = res_ref[...] + acc
        for o in o_refs:
            o[...] = acc.astype(o.dtype)

    if nk == 1:
        finish(part)
    else:
        k = pl.program_id(2)

        @pl.when(k == 0)
        def _():
            acc_ref[...] = part

        @pl.when(k > 0)
        def _():
            acc_ref[...] += part

        @pl.when(k == nk - 1)
        def _():
            finish(acc_ref[...])


def matmul(x, w, out_dtypes=(F32,), res=None, scale=None, w_index=None, tm=None, tn=None, tk=None, rows=None):
    row0, m = rows if rows is not None else (0, x.shape[0])
    kdim = x.shape[1]
    n = w.shape[-1]
    tm = tm or _pick(math.gcd(m, row0) if row0 else m, (1280, 1024, 640, 512, 320, 256, 128, 64, 32, 16, 8))
    tn = tn or _pick(n, (512, 384, 256, 128))
    tk = tk or (kdim if kdim <= 4096 else _pick(kdim, (2048, 1792, 1024, 512, 256, 128)))
    nk = kdim // tk
    grid = (m // tm, n // tn, nk)
    rb0 = row0 // tm
    in_specs = [pl.BlockSpec((tm, tk), lambda i, j, k: (rb0 + i, k))]
    if w.ndim == 3:
        in_specs.append(pl.BlockSpec((None, tk, tn), lambda i, j, k: (w_index, k, j)))
    else:
        in_specs.append(pl.BlockSpec((tk, tn), lambda i, j, k: (k, j)))
    args = [x, w]
    if res is not None:
        in_specs.append(pl.BlockSpec((tm, tn), lambda i, j, k: (i, j)))
        args.append(res)
    if scale is not None:
        in_specs.append(pl.BlockSpec((tm, 1), lambda i, j, k: (i, 0)))
        args.append(scale)
    out_specs = tuple(pl.BlockSpec((tm, tn), lambda i, j, k: (i, j)) for _ in out_dtypes)
    out_shape = tuple(jax.ShapeDtypeStruct((m, n), d) for d in out_dtypes)
    scratch = [pltpu.VMEM((tm, tn), F32)] if nk > 1 else []
    vmem = (2 * tm * tk * x.dtype.itemsize + 2 * tk * tn * w.dtype.itemsize
            + tm * tn * 4 * (2 * len(out_dtypes) + 1 + (2 if res is not None else 0)))
    outs = pl.pallas_call(
        functools.partial(_mm_kernel, nk=nk, has_res=res is not None, has_scale=scale is not None,
                          n_out=len(out_dtypes)),
        grid=grid, in_specs=in_specs, out_specs=out_specs, out_shape=out_shape,
        scratch_shapes=scratch,
        compiler_params=_cparams(("parallel", "parallel", "arbitrary"), vmem),
        name="matmul",
    )(*args)
    return outs


def _mm2_kernel(xa_ref, xb_ref, wa_ref, wb_ref, res_ref, o_ref):
    acc = jnp.dot(xa_ref[...], wa_ref[...], preferred_element_type=F32)
    acc = acc + jnp.dot(xb_ref[...], wb_ref[...], preferred_element_type=F32)
    o_ref[...] = res_ref[...] + acc


def matmul_pair(xa, xb, w, res):
    m, ka = xa.shape
    kb = xb.shape[1]
    n = w.shape[1]
    tm = _pick(m, (1280, 1024, 640, 512, 320, 256, 128, 64, 32, 16, 8))
    tn = _pick(n, (512, 384, 256, 128))
    vmem = 2 * tm * (ka + kb) * 2 + 2 * (ka + kb) * tn * 2 + 5 * tm * tn * 4
    return pl.pallas_call(
        _mm2_kernel,
        grid=(m // tm, n // tn),
        in_specs=[pl.BlockSpec((tm, ka), lambda i, j: (i, 0)), pl.BlockSpec((tm, kb), lambda i, j: (i, 0)),
                  pl.BlockSpec((ka, tn), lambda i, j: (0, j)), pl.BlockSpec((kb, tn), lambda i, j: (0, j)),
                  pl.BlockSpec((tm, tn), lambda i, j: (i, j))],
        out_specs=pl.BlockSpec((tm, tn), lambda i, j: (i, j)),
        out_shape=jax.ShapeDtypeStruct((m, n), F32),
        compiler_params=_cparams(("parallel", "parallel"), vmem),
        name="matmul_pair",
    )(xa, xb, w[:ka], w[ka:], res)


def _mm_heads_kernel(x_ref, w_ref, o3_ref, ob_ref):
    acc = jnp.dot(x_ref[...], w_ref[...], preferred_element_type=F32)
    ob_ref[...] = acc.astype(ob_ref.dtype)
    hd = o3_ref.shape[2]
    for hh in range(o3_ref.shape[1]):
        o3_ref[:, hh, :] = acc[:, hh * hd:(hh + 1) * hd]


def matmul_heads(x, w, nheads, rows):
    row0, m = rows
    kdim, n = w.shape
    hd = n // nheads
    tm = _pick(math.gcd(m, row0) if row0 else m, (256, 128, 64, 32, 16, 8))
    rb0 = row0 // tm
    vmem = 2 * tm * kdim * 2 + 2 * kdim * n * 2 + 8 * tm * n * 4
    return pl.pallas_call(
        _mm_heads_kernel,
        grid=(m // tm,),
        in_specs=[pl.BlockSpec((tm, kdim), lambda i: (rb0 + i, 0)), pl.BlockSpec((kdim, n), lambda i: (0, 0))],
        out_specs=(pl.BlockSpec((tm, nheads, hd), lambda i: (i, 0, 0)), pl.BlockSpec((tm, n), lambda i: (i, 0))),
        out_shape=(jax.ShapeDtypeStruct((m, nheads, hd), F32), jax.ShapeDtypeStruct((m, n), BF16)),
        compiler_params=_cparams(("parallel",), vmem),
        name="matmul_heads",
    )(x, w)


def _rope_mm_kernel(x_ref, w1_ref, w2_ref, c_ref, s_ref, o_ref):
    x = x_ref[...]
    a = jnp.dot(x, w1_ref[...], preferred_element_type=F32)
    b = jnp.dot(x, w2_ref[...], preferred_element_type=F32)
    o_ref[...] = (a * c_ref[...] + b * s_ref[...]).astype(o_ref.dtype)


def rope_matmul(x, w1, w2, ctab, stab, out_dtype, tn):
    m, kdim = x.shape
    n = w1.shape[1]
    est = lambda rows: 2 * rows * kdim * 2 + 4 * kdim * tn * 2 + 8 * rows * tn * 4
    tm = next((c for c in (1280, 1024, 640, 512, 320, 256, 128, 64, 32, 16, 8)
               if m % c == 0 and est(c) <= VMEM_CAP // 2), 8)
    vmem = est(tm)
    return pl.pallas_call(
        _rope_mm_kernel,
        grid=(m // tm, n // tn),
        in_specs=[pl.BlockSpec((tm, kdim), lambda i, j: (i, 0)),
                  pl.BlockSpec((kdim, tn), lambda i, j: (0, j)),
                  pl.BlockSpec((kdim, tn), lambda i, j: (0, j)),
                  pl.BlockSpec((tm, tn), lambda i, j: (i, 0)),
                  pl.BlockSpec((tm, tn), lambda i, j: (i, 0))],
        out_specs=pl.BlockSpec((tm, tn), lambda i, j: (i, j)),
        out_shape=jax.ShapeDtypeStruct((m, n), out_dtype),
        compiler_params=_cparams(("parallel", "parallel"), vmem),
        name="rope_matmul",
    )(x, w1, w2, ctab, stab)


def _swiglu_up_kernel(x_ref, wg_ref, wu_ref, o_ref):
    x = x_ref[...]
    g = jnp.dot(x, wg_ref[...].astype(BF16), preferred_element_type=F32)
    u = jnp.dot(x, wu_ref[...].astype(BF16), preferred_element_type=F32)
    o_ref[...] = (g * jax.nn.sigmoid(g) * u).astype(o_ref.dtype)


def swiglu_up(x, wg, wu, w_index=None):
    m, kdim = x.shape
    n = wg.shape[-1]
    tm = _pick(m, (1280, 1024, 640, 512, 320, 256, 128, 64, 32, 16, 8))
    tn = _pick(n, (256, 128))
    if wg.ndim == 3:
        wspec = pl.BlockSpec((None, kdim, tn), lambda i, j: (w_index, 0, j))
    else:
        wspec = pl.BlockSpec((kdim, tn), lambda i, j: (0, j))
    vmem = 2 * tm * kdim * 2 + 4 * kdim * tn * wg.dtype.itemsize + 6 * tm * tn * 4
    return pl.pallas_call(
        _swiglu_up_kernel,
        grid=(m // tm, n // tn),
        in_specs=[pl.BlockSpec((tm, kdim), lambda i, j: (i, 0)), wspec, wspec],
        out_specs=pl.BlockSpec((tm, tn), lambda i, j: (i, j)),
        out_shape=jax.ShapeDtypeStruct((m, n), BF16),
        compiler_params=_cparams(("parallel", "parallel"), vmem),
        name="swiglu_up",
    )(x, wg, wu)


def _s5_kernel(u_ref, h0r_ref, h0i_ref, lr_ref, li_ref, bbr_ref, bbi_ref, ccr_ref, cci_ref,
               d_ref, wglu_ref, bglu_ref, y_ref, hr_ref, hi_ref, xr_s, xi_s, st_r, st_i, *, tc, nblk):
    c = pl.program_id(1)

    @pl.when(c == 0)
    def _():
        st_r[...] = h0r_ref[0]
        st_i[...] = h0i_ref[0]

    u = u_ref[...]
    ub = u.astype(BF16)
    sw = xr_s.shape[1] // nblk
    for k in range(nblk):
        uk = ub[:, k * LANES:(k + 1) * LANES]
        xr_s[:, k * sw:(k + 1) * sw] = jnp.dot(uk, bbr_ref[k], preferred_element_type=F32)
        xi_s[:, k * sw:(k + 1) * sw] = jnp.dot(uk, bbi_ref[k], preferred_element_type=F32)

    scan_w = 1024
    for q in range(xr_s.shape[1] // scan_w):
        cols = slice(q * scan_w, (q + 1) * scan_w)
        lr = lr_ref[:, cols]
        li = li_ref[:, cols]

        def body(t, carry, cols=cols, lr=lr, li=li):
            hr, hi = carry
            nr = lr * hr - li * hi + xr_s[pl.ds(t, 1), cols]
            ni = lr * hi + li * hr + xi_s[pl.ds(t, 1), cols]
            xr_s[pl.ds(t, 1), cols] = nr
            xi_s[pl.ds(t, 1), cols] = ni
            return nr, ni

        hr, hi = lax.fori_loop(0, tc, body, (st_r[:, cols], st_i[:, cols]))
        st_r[:, cols] = hr
        st_i[:, cols] = hi

    hr_ref[0] = st_r[...]
    hi_ref[0] = st_i[...]

    ys = []
    for k in range(nblk):
        xr = xr_s[:, k * sw:(k + 1) * sw].astype(BF16)
        xi = xi_s[:, k * sw:(k + 1) * sw].astype(BF16)
        ys.append(jnp.dot(xr, ccr_ref[k], preferred_element_type=F32)
                  - jnp.dot(xi, cci_ref[k], preferred_element_type=F32))
    y = jnp.concatenate(ys, axis=1) + d_ref[...] * u
    y = jax.nn.gelu(y)
    gate = jax.nn.sigmoid(jnp.dot(y.astype(BF16), wglu_ref[...], preferred_element_type=F32) + bglu_ref[...])
    y_ref[...] = (y * gate).astype(y_ref.dtype)


def s5_mixer(z, row_off, bsz, t, h0_re, h0_im, prm):
    width = prm["d"].shape[1]
    nblk = width // LANES
    nstate = prm["lr"].shape[1]
    tc = _pick(t, (256, 128, 64, 32, 16, 8))
    nt = t // tc
    rb0 = row_off // tc
    full = lambda shape: pl.BlockSpec(shape, lambda b, c: (0,) * len(shape))
    vmem = (4 * tc * width * 4 + 2 * tc * nstate * 4 + 4 * nblk * LANES * (nstate // nblk) * 2 * 2
            + 2 * width * width * 2 + 8 * tc * width * 4)
    y, hr, hi = pl.pallas_call(
        functools.partial(_s5_kernel, tc=tc, nblk=nblk),
        grid=(bsz, nt),
        in_specs=[pl.BlockSpec((tc, width), lambda b, c: (rb0 + b * nt + c, 0)),
                  pl.BlockSpec((1, 1, nstate), lambda b, c: (b, 0, 0)),
                  pl.BlockSpec((1, 1, nstate), lambda b, c: (b, 0, 0)),
                  full((1, nstate)), full((1, nstate)),
                  full(prm["bbr"].shape), full(prm["bbi"].shape),
                  full(prm["ccr"].shape), full(prm["cci"].shape),
                  full((1, width)), full((width, width)), full((1, width))],
        out_specs=(pl.BlockSpec((tc, width), lambda b, c: (b * nt + c, 0)),
                   pl.BlockSpec((1, 1, nstate), lambda b, c: (b, 0, 0)),
                   pl.BlockSpec((1, 1, nstate), lambda b, c: (b, 0, 0))),
        out_shape=(jax.ShapeDtypeStruct((bsz * t, width), BF16),
                   jax.ShapeDtypeStruct((bsz, 1, nstate), F32),
                   jax.ShapeDtypeStruct((bsz, 1, nstate), F32)),
        scratch_shapes=[pltpu.VMEM((tc, nstate), F32), pltpu.VMEM((tc, nstate), F32),
                        pltpu.VMEM((1, nstate), F32), pltpu.VMEM((1, nstate), F32)],
        compiler_params=_cparams(("parallel", "arbitrary"), vmem),
        name="s5_mixer",
    )(z, h0_re.reshape(bsz, 1, nstate), h0_im.reshape(bsz, 1, nstate), prm["lr"], prm["li"],
      prm["bbr"], prm["bbi"], prm["ccr"], prm["cci"], prm["d"], prm["wglu"], prm["bglu"])
    return y, hr, hi


def s5_params(a_re, a_im, log_dt, b_re, b_im, c_re, c_im, d_skip, w_glu, b_glu):
    g, p = a_re.shape
    nch = b_re.shape[2]
    dt = jnp.exp(log_dt.astype(F32))[:, None]
    ar, ai = a_re.astype(F32), a_im.astype(F32)
    mag = jnp.exp(ar * dt)
    lr = mag * jnp.cos(ai * dt)
    li = mag * jnp.sin(ai * dt)
    den = ar * ar + ai * ai
    fr = ((lr - 1.0) * ar + li * ai) / den
    fi = (li * ar - (lr - 1.0) * ai) / den
    br, bi = b_re.astype(F32), b_im.astype(F32)
    bbr = fr[..., None] * br - fi[..., None] * bi
    bbi = fr[..., None] * bi + fi[..., None] * br
    gb = S5_BLOCK_GROUPS
    nblk = g // gb
    eye = jnp.eye(gb, dtype=F32)

    def blk_in(m):
        m = m.reshape(nblk, gb, p, nch)
        return jnp.einsum("kgpn,gh->kgnhp", m, eye).reshape(nblk, gb * nch, gb * p).astype(BF16)

    def blk_out(m):
        m = m.astype(F32).reshape(nblk, gb, nch, p)
        return jnp.einsum("kgnp,gh->kgphn", m, eye).reshape(nblk, gb * p, gb * nch).astype(BF16)

    width = g * nch
    return dict(lr=lr.reshape(1, g * p), li=li.reshape(1, g * p), bbr=blk_in(bbr), bbi=blk_in(bbi),
                ccr=blk_out(c_re), cci=blk_out(c_im), d=d_skip.astype(F32).reshape(1, width),
                wglu=w_glu.astype(BF16), bglu=b_glu.astype(F32).reshape(1, width))


MLA_ROWS = 1024
MLA_TQ = 512
MLA_CHAIN_ROWS = 256


def _mla_kernel(q_ref, ckv_ref, kpe_ref, wuk_ref, wuv_ref, o_ref, qa_s, qpe_s, m_s, l_s, acc_s,
                *, tq, tk, nk_total, klen, causal, scale, hps):
    i = pl.program_id(1)
    nope = wuk_ref.shape[1]
    qw = q_ref.shape[1] // hps
    vdim = wuv_ref.shape[2]
    rows_all = hps * tq
    rc = MLA_CHAIN_ROWS if rows_all % MLA_CHAIN_ROWS == 0 else rows_all
    for hh in range(hps):
        rows = slice(hh * tq, (hh + 1) * tq)
        qa = jnp.dot(q_ref[:, hh * qw:hh * qw + nope], wuk_ref[hh], preferred_element_type=F32)
        qa_s[rows, :] = qa.astype(qa_s.dtype)
        qpe_s[rows, :] = q_ref[:, hh * qw + nope:(hh + 1) * qw]
    m_s[...] = jnp.full(m_s.shape, NEG_INF, F32)
    l_s[...] = jnp.zeros(l_s.shape, F32)
    acc_s[...] = jnp.zeros(acc_s.shape, F32)

    def tile(j, masked):
        ks = pl.multiple_of(j * tk, tk)
        ckv = ckv_ref[0, pl.ds(ks, tk), :]
        kpe = kpe_ref[0, pl.ds(ks, tk), :]
        scores = []
        for c0 in range(0, rows_all, rc):
            rows = slice(c0, c0 + rc)
            s = lax.dot_general(qa_s[rows, :].astype(BF16), ckv, (((1,), (1,)), ((), ())), preferred_element_type=F32)
            s = s + lax.dot_general(qpe_s[rows, :], kpe, (((1,), (1,)), ((), ())), preferred_element_type=F32)
            scores.append(s * scale)
        for c0, s in zip(range(0, rows_all, rc), scores):
            rows = slice(c0, c0 + rc)
            if masked:
                kpos = ks + lax.broadcasted_iota(jnp.int32, (rc, tk), 1)
                if causal:
                    qpos = i * tq + (c0 + lax.broadcasted_iota(jnp.int32, (rc, tk), 0)) % tq
                    s = jnp.where(kpos // CHUNK <= qpos // CHUNK, s, NEG_INF)
                if klen < nk_total * tk:
                    s = jnp.where(kpos < klen, s, NEG_INF)
            m_prev = m_s[rows, :]
            m_new = jnp.maximum(m_prev, jnp.max(s, axis=1, keepdims=True))
            alpha = jnp.exp(m_prev - m_new)
            p = jnp.exp(s - m_new)
            l_s[rows, :] = alpha * l_s[rows, :] + jnp.sum(p, axis=1, keepdims=True)
            acc_s[rows, :] = alpha * acc_s[rows, :] + jnp.dot(p.astype(BF16), ckv, preferred_element_type=F32)
            m_s[rows, :] = m_new

    def full_tile(j, c):
        tile(j, False)
        return c

    n_full = i if causal else nk_total - 1
    lax.fori_loop(0, n_full, full_tile, 0)
    tile(n_full, True)
    o_lat = (acc_s[...] / l_s[...]).astype(BF16)
    for hh in range(hps):
        o_ref[:, hh * vdim:(hh + 1) * vdim] = jnp.dot(
            o_lat[hh * tq:(hh + 1) * tq], wuv_ref[hh], preferred_element_type=F32).astype(o_ref.dtype)


def mla_attention(q_all, row_off, bsz, t, ckv, kpe, wuk, wuv, klen, causal, scale):
    nheads, nope, lat = wuk.shape
    vdim = wuv.shape[2]
    qw = q_all.shape[1] // nheads
    tkeys = ckv.shape[1]
    tq = _pick(t, (MLA_TQ, 256, 128, 64, 32, 16, 8))
    tk = tq if causal else _pick(tkeys, (256, 128))
    if causal:
        assert tq % CHUNK == 0 and tkeys == t
    hps = _pick(nheads, tuple(c for c in (24, 16, 12, 8, 6, 4, 3, 2, 1) if c * tq <= MLA_ROWS))
    nq = t // tq
    rb0 = row_off // tq
    rows_all = hps * tq
    vmem = (2 * tkeys * (lat + LANES) * 2 + 4 * tq * hps * qw * 2 + 3 * rows_all * lat * 4 + 8 * rows_all * tk * 4
            + 4 * hps * (nope + vdim) * lat * 2 + (2 << 20))
    return pl.pallas_call(
        functools.partial(_mla_kernel, tq=tq, tk=tk, nk_total=tkeys // tk, klen=klen, causal=causal, scale=scale,
                          hps=hps),
        grid=(bsz, nq, nheads // hps),
        in_specs=[pl.BlockSpec((tq, hps * qw), lambda b, i, h: (rb0 + b * nq + i, h)),
                  pl.BlockSpec((1, tkeys, lat), lambda b, i, h: (b, 0, 0)),
                  pl.BlockSpec((1, tkeys, LANES), lambda b, i, h: (b, 0, 0)),
                  pl.BlockSpec((hps, nope, lat), lambda b, i, h: (h, 0, 0)),
                  pl.BlockSpec((hps, lat, vdim), lambda b, i, h: (h, 0, 0))],
        out_specs=pl.BlockSpec((tq, hps * vdim), lambda b, i, h: (b * nq + i, h)),
        out_shape=jax.ShapeDtypeStruct((bsz * t, nheads * vdim), BF16),
        scratch_shapes=[pltpu.VMEM((rows_all, lat), F32), pltpu.VMEM((rows_all, LANES), BF16),
                        pltpu.VMEM((rows_all, 1), F32), pltpu.VMEM((rows_all, 1), F32),
                        pltpu.VMEM((rows_all, lat), F32)],
        compiler_params=_cparams(("parallel", "parallel", "arbitrary"), vmem),
        name="mla_attention",
    )(q_all, ckv, kpe, wuk, wuv)


def _fox_gate_kernel(pre_ref, zf_ref, bf_ref, logf_ref, cum_ref, cumt_ref, lf_s, *, npre, t, blk):
    total = lf_s.shape[0]
    z = zf_ref[0] + bf_ref[...]
    logf = jnp.minimum(z, 0.0) - jnp.log1p(jnp.exp(-jnp.abs(z)))
    logf_ref[0] = logf
    if npre + t < total:
        lf_s[...] = jnp.zeros(lf_s.shape, F32)
    if npre:
        lf_s[0:npre, :] = pre_ref[0]
    lf_s[npre:npre + t, :] = logf
    tri = (lax.broadcasted_iota(jnp.int32, (blk, blk), 1)
           <= lax.broadcasted_iota(jnp.int32, (blk, blk), 0)).astype(F32)
    carry = jnp.zeros((1, LANES), F32)
    for c in range(total // blk):
        rows = slice(c * blk, (c + 1) * blk)
        cum = jnp.dot(tri, lf_s[rows, :], preferred_element_type=F32, precision=lax.Precision.HIGHEST) + carry
        cum_ref[0, rows, :] = cum
        cumt_ref[0, c] = cum.T[:cumt_ref.shape[2], :]
        carry = cum[blk - 1:blk, :]


def fox_gate(zf, row_off, bsz, t, b_f, pre):
    nheads = b_f.shape[0]
    npre = 0 if pre is None else pre.shape[1]
    total = -(-(npre + t) // LANES) * LANES
    if pre is None:
        pre = jnp.zeros((bsz, 8, LANES), F32)
    pp = pre.shape[1]
    hrows = -(-nheads // 8) * 8
    zf3 = zf[row_off:row_off + bsz * t].reshape(bsz, t, LANES)
    bfp = jnp.zeros((1, LANES), F32).at[0, :nheads].set(b_f.astype(F32))
    return pl.pallas_call(
        functools.partial(_fox_gate_kernel, npre=npre, t=t, blk=LANES),
        grid=(bsz,),
        in_specs=[pl.BlockSpec((1, pp, LANES), lambda b: (b, 0, 0)),
                  pl.BlockSpec((1, t, LANES), lambda b: (b, 0, 0)),
                  pl.BlockSpec((1, LANES), lambda b: (0, 0))],
        out_specs=(pl.BlockSpec((1, t, LANES), lambda b: (b, 0, 0)),
                   pl.BlockSpec((1, total, LANES), lambda b: (b, 0, 0)),
                   pl.BlockSpec((1, total // LANES, hrows, LANES), lambda b: (b, 0, 0, 0))),
        out_shape=(jax.ShapeDtypeStruct((bsz, t, LANES), F32),
                   jax.ShapeDtypeStruct((bsz, total, LANES), F32),
                   jax.ShapeDtypeStruct((bsz, total // LANES, hrows, LANES), F32)),
        scratch_shapes=[pltpu.VMEM((total, LANES), F32)],
        compiler_params=_cparams(("parallel",), 12 * total * LANES * 4),
        name="fox_gate",
    )(pre, zf3, bfp)


FOX_HEADS_PER_STEP = 8
FOX_TQ = 512


def _fox_attn_kernel(q_ref, k_ref, v_ref, cq_ref, ck_ref, o_ref, m_s, l_s, acc_s,
                     *, tq, tk, qoff, scale, hps):
    hg = pl.program_id(1)
    i = pl.program_id(2)
    hd = acc_s.shape[2]
    lane = lax.broadcasted_iota(jnp.int32, cq_ref.shape[1:], 1)
    cq = cq_ref[0]
    fq = [jnp.sum(jnp.where(lane == hg * hps + hh, cq, 0.0), axis=1, keepdims=True) for hh in range(hps)]
    m_s[...] = jnp.full(m_s.shape, NEG_INF, F32)
    l_s[...] = jnp.zeros(l_s.shape, F32)
    acc_s[...] = jnp.zeros(acc_s.shape, F32)

    def tile(j, masked):
        ks = pl.multiple_of(j * tk, tk)
        scores = []
        for hh in range(hps):
            cols = slice(hh * hd, (hh + 1) * hd)
            k = k_ref[pl.ds(ks, tk), cols]
            fk = jnp.concatenate([ck_ref[0, j * (tk // LANES) + c, pl.ds(hg * hps + hh, 1), :]
                                  for c in range(tk // LANES)], axis=1)
            s = lax.dot_general(q_ref[:, cols], k, (((1,), (1,)), ((), ())), preferred_element_type=F32) * scale
            scores.append(s + fq[hh] - fk)
        for hh, s in enumerate(scores):
            cols = slice(hh * hd, (hh + 1) * hd)
            if masked:
                qpos = qoff + i * tq + lax.broadcasted_iota(jnp.int32, (tq, tk), 0)
                kpos = ks + lax.broadcasted_iota(jnp.int32, (tq, tk), 1)
                s = jnp.where(kpos <= qpos, s, NEG_INF)
            m_prev = m_s[hh]
            m_new = jnp.maximum(m_prev, jnp.max(s, axis=1, keepdims=True))
            alpha = jnp.exp(m_prev - m_new)
            p = jnp.exp(s - m_new)
            l_s[hh] = alpha * l_s[hh] + jnp.sum(p, axis=1, keepdims=True)
            v = v_ref[pl.ds(ks, tk), cols]
            acc_s[hh] = alpha * acc_s[hh] + jnp.dot(p.astype(BF16), v, preferred_element_type=F32)
            m_s[hh] = m_new

    def full_tile(j, c):
        tile(j, False)
        return c

    n_full = (qoff + i * tq) // tk
    lax.fori_loop(0, n_full, full_tile, 0)
    tile(n_full, True)
    for hh in range(hps):
        o_ref[:, hh * hd:(hh + 1) * hd] = (acc_s[hh] / l_s[hh]).astype(o_ref.dtype)


def fox_attention(q_all, row_off, bsz, t, k, v, tkeys, cum, cumt, qoff, scale):
    hd = LANES
    nheads = q_all.shape[1] // hd
    hps = _pick(nheads, (FOX_HEADS_PER_STEP, 2, 1))
    tq = _pick(t, (FOX_TQ, 256, 128, 64, 32, 16, 8))
    tk = tkeys if tq * tkeys * 4 <= (256 << 10) else _pick(tkeys, (tq, 256, 128))
    assert tk % tq == 0 and qoff % tq == 0 and qoff + t <= tkeys
    nq = t // tq
    rb0 = row_off // tq
    cq0 = qoff // tq
    vmem = (8 * tkeys * hps * hd * 2 + 2 * cumt.shape[2] * tkeys * 4 + 4 * hps * tq * tk * 4
            + 8 * hps * tq * hd * 4 + (2 << 20))
    return pl.pallas_call(
        functools.partial(_fox_attn_kernel, tq=tq, tk=tk, qoff=qoff, scale=scale, hps=hps),
        grid=(bsz, nheads // hps, nq),
        in_specs=[pl.BlockSpec((tq, hps * hd), lambda b, h, i: (rb0 + b * nq + i, h)),
                  pl.BlockSpec((tkeys, hps * hd), lambda b, h, i: (b, h)),
                  pl.BlockSpec((tkeys, hps * hd), lambda b, h, i: (b, h)),
                  pl.BlockSpec((1, tq, LANES), lambda b, h, i: (b, cq0 + i, 0)),
                  pl.BlockSpec((1,) + cumt.shape[1:], lambda b, h, i: (b, 0, 0, 0))],
        out_specs=pl.BlockSpec((tq, hps * hd), lambda b, h, i: (b * nq + i, h)),
        out_shape=jax.ShapeDtypeStruct((bsz * t, nheads * hd), BF16),
        scratch_shapes=[pltpu.VMEM((hps, tq, 1), F32), pltpu.VMEM((hps, tq, 1), F32), pltpu.VMEM((hps, tq, hd), F32)],
        compiler_params=_cparams(("parallel", "parallel", "arbitrary"), vmem),
        name="fox_attention",
    )(q_all, k, v, cum, cumt)


FOX_DECODE_HEADS = 8
FOX_DECODE_CHUNK = 512


def _fox_decode_kernel(q_ref, pk_ref, pv_ref, nk_ref, nv_ref, cq_ref, fkp_ref, fkn_ref, o_ref, *, t, chunk, scale):
    hg = pl.program_id(1)
    g = pk_ref.shape[2]
    hd = pk_ref.shape[3]
    rows = g * t
    nt_dims = (((1,), (1,)), ((), ()))
    stack = lambda ref: jnp.concatenate([ref[:, h * hd:(h + 1) * hd] for h in range(g)], axis=0)
    q, kn, vn = stack(q_ref), stack(nk_ref), stack(nv_ref)
    lane = lax.broadcasted_iota(jnp.int32, cq_ref.shape[1:], 1)
    cq = cq_ref[0]
    fq = jnp.concatenate([jnp.sum(jnp.where(lane == hg * g + h, cq, 0.0), axis=1, keepdims=True)
                          for h in range(g)], axis=0)

    def update(carry, s, v):
        m, l, acc = carry
        m_new = jnp.maximum(m, jnp.max(s, axis=1, keepdims=True))
        alpha = jnp.exp(m - m_new)
        p = jnp.exp(s - m_new)
        return (m_new, alpha * l + jnp.sum(p, axis=1, keepdims=True),
                alpha * acc + jnp.dot(p.astype(BF16), v, preferred_element_type=F32))

    carry = (jnp.full((rows, 1), NEG_INF, F32), jnp.zeros((rows, 1), F32), jnp.zeros((rows, hd), F32))
    cols = chunk * g
    same_head = (lax.broadcasted_iota(jnp.int32, (rows, cols), 0) // t
                 == lax.broadcasted_iota(jnp.int32, (rows, cols), 1) % g)
    for c in range(pk_ref.shape[1] // chunk):
        kp = pk_ref[0, c * chunk:(c + 1) * chunk].reshape(cols, hd).astype(BF16)
        vp = pv_ref[0, c * chunk:(c + 1) * chunk].reshape(cols, hd).astype(BF16)
        s = lax.dot_general(q, kp, nt_dims, preferred_element_type=F32) * scale
        s = s + fq - fkp_ref[0, 0, :, c * cols:(c + 1) * cols]
        carry = update(carry, jnp.where(same_head, s, NEG_INF), vp)
    rr = lax.broadcasted_iota(jnp.int32, (rows, rows), 0)
    cc = lax.broadcasted_iota(jnp.int32, (rows, rows), 1)
    s = lax.dot_general(q, kn, nt_dims, preferred_element_type=F32) * scale + fq - fkn_ref[0, 0]
    s = jnp.where(rr // t == cc // t, jnp.where(cc % t <= rr % t, s, NEG_INF), NEG_INF)
    _, l, acc = update(carry, s, vn)
    out = acc / l
    for h in range(g):
        o_ref[:, h * hd:(h + 1) * hd] = out[h * t:(h + 1) * t].astype(o_ref.dtype)


def fox_decode_attention(q_all, row_off, bsz, t, past_k, past_v, new_k, new_v, cum, scale):
    _, npast, nheads, hd = past_k.shape
    g = FOX_DECODE_HEADS
    assert nheads % g == 0 and hd == LANES and npast % t == 0
    ng = nheads // g
    chunk = _pick(npast, (FOX_DECODE_CHUNK, 256, 128, 64, 32, 16, 8))
    rb0 = row_off // t
    fkp = jnp.transpose(cum[:, :npast, :nheads].reshape(bsz, npast, ng, g), (0, 2, 1, 3)).reshape(bsz, ng, 1, npast * g)
    fkn = jnp.transpose(cum[:, npast:npast + t, :nheads].reshape(bsz, t, ng, g), (0, 2, 3, 1)).reshape(bsz, ng, 1, g * t)
    vmem = 4 * npast * g * hd * 4 + 8 * g * t * chunk * g * 4 + 4 * chunk * g * hd * 2 + (4 << 20)
    return pl.pallas_call(
        functools.partial(_fox_decode_kernel, t=t, chunk=chunk, scale=scale),
        grid=(bsz, ng),
        in_specs=[pl.BlockSpec((t, g * hd), lambda b, h: (rb0 + b, h)),
                  pl.BlockSpec((1, npast, g, hd), lambda b, h: (b, 0, h, 0)),
                  pl.BlockSpec((1, npast, g, hd), lambda b, h: (b, 0, h, 0)),
                  pl.BlockSpec((t, g * hd), lambda b, h: (b, h)),
                  pl.BlockSpec((t, g * hd), lambda b, h: (b, h)),
                  pl.BlockSpec((1, t, LANES), lambda b, h: (b, npast // t, 0)),
                  pl.BlockSpec((1, 1, 1, npast * g), lambda b, h: (b, h, 0, 0)),
                  pl.BlockSpec((1, 1, 1, g * t), lambda b, h: (b, h, 0, 0))],
        out_specs=pl.BlockSpec((t, g * hd), lambda b, h: (b, h)),
        out_shape=jax.ShapeDtypeStruct((bsz * t, nheads * hd), BF16),
        compiler_params=_cparams(("parallel", "parallel"), vmem),
        name="fox_decode_attention",
    )(q_all, past_k, past_v, new_k, new_v, cum, fkp, fkn)


def _rwkv_prep_kernel(z_ref, zp_ref, sh_ref, mu_ref, w0_ref, a0_ref, w2_ref, a2_ref, g2_ref,
                      r_ref, k_ref, v_ref, lw_ref, a_ref, g_ref, *, w):
    i = pl.program_id(1)
    z = z_ref[...]
    prev_row = jnp.where(i == 0, sh_ref[0], zp_ref[7:8, :])
    row = lax.broadcasted_iota(jnp.int32, z.shape, 0)
    z_prev = jnp.where(row == 0, prev_row, pltpu.roll(z, 1, 0))
    zm = z + (z_prev - z) * mu_ref[...]
    slab = zm[:, 3 * w:]
    lora_w = jnp.dot(jnp.tanh(slab).astype(BF16), w2_ref[...], preferred_element_type=F32)
    x = -(w0_ref[...] + lora_w)
    softplus = jnp.maximum(x, 0.0) + jnp.log1p(jnp.exp(-jnp.abs(x)))
    lw = -jnp.exp(-softplus - 0.5)
    a = jax.nn.sigmoid(a0_ref[...] + jnp.dot(slab.astype(BF16), a2_ref[...], preferred_element_type=F32))
    g = jnp.dot(jax.nn.sigmoid(slab).astype(BF16), g2_ref[...], preferred_element_type=F32)
    nh, hd = r_ref.shape[1], r_ref.shape[3]
    for ref, val in ((r_ref, zm[:, 0:w]), (k_ref, zm[:, w:2 * w]), (v_ref, zm[:, 2 * w:3 * w]),
                     (lw_ref, lw), (a_ref, a), (g_ref, g)):
        for hh in range(nh):
            ref[0, hh] = val[:, hh * hd:(hh + 1) * hd]


def rwkv_prep(zr, row_off, bsz, t, shift_prev, prm, nh):
    wz = zr.shape[1]
    w = prm["w0"].shape[1]
    hd = w // nh
    tt = _pick(t, (128, 64, 32, 16, 8))
    nt = t // tt
    rb0 = row_off // tt
    full = lambda shape: pl.BlockSpec(shape, lambda b, i: (0,) * len(shape))
    ospec = pl.BlockSpec((1, nh, tt, hd), lambda b, i: (b, 0, i, 0))
    oshape = jax.ShapeDtypeStruct((bsz, nh, t, hd), F32)
    ls = wz - 3 * w
    vmem = 6 * tt * wz * 4 + 12 * tt * nh * LANES * 4 + 6 * ls * w * 2 + 8 * tt * w * 4
    return pl.pallas_call(
        functools.partial(_rwkv_prep_kernel, w=w),
        grid=(bsz, nt),
        in_specs=[pl.BlockSpec((tt, wz), lambda b, i: (rb0 + b * nt + i, 0)),
                  pl.BlockSpec((8, wz), lambda b, i: (jnp.maximum((row_off + (b * nt + i) * tt) // 8 - 1, 0), 0)),
                  pl.BlockSpec((1, 1, wz), lambda b, i: (b, 0, 0)),
                  full((1, wz)), full((1, w)), full((1, w)), full((ls, w)), full((ls, w)), full((ls, w))],
        out_specs=(ospec,) * 6,
        out_shape=(oshape,) * 6,
        compiler_params=_cparams(("parallel", "arbitrary"), vmem),
        name="rwkv_prep",
    )(zr, zr, shift_prev.reshape(bsz, 1, wz), prm["mu"], prm["w0"], prm["a0"], prm["w2"], prm["a2"], prm["g2"])


def _bdot(a, b, dims):
    return lax.dot_general(a.astype(BF16), b.astype(BF16), dims, preferred_element_type=F32)


def _cumsum_rows(tri, x):
    hi = x.astype(BF16)
    r1 = x - hi.astype(F32)
    mid = r1.astype(BF16)
    lo = (r1 - mid.astype(F32)).astype(BF16)
    t = tri.astype(BF16)
    dot = lambda p: lax.dot_general(t, p, _NN, preferred_element_type=F32)
    return dot(hi) + dot(mid) + dot(lo)


_NT = (((2,), (2,)), ((0,), (0,)))
_NN = (((2,), (1,)), ((0,), (0,)))
_TN = (((1,), (1,)), ((0,), (0,)))


def _rwkv_scan_kernel(r_ref, k_ref, v_ref, lw_ref, a_ref, g_ref, s0_ref, kk_ref, ka_ref, rk_ref,
                      lnw_ref, lnb_ref, y_ref, sout_ref, st_s, *, chunk):
    c = pl.program_id(2)

    @pl.when(c == 0)
    def _():
        st_s[...] = s0_ref[0]

    r = r_ref[0]
    k = k_ref[0]
    v = v_ref[0]
    lw = lw_ref[0]
    a = a_ref[0]
    hb = r.shape[0]
    s0 = st_s[...]

    kk = k * kk_ref[...]
    kk = kk / jnp.maximum(jnp.sqrt(jnp.sum(kk * kk, axis=-1, keepdims=True)), 1e-12)
    kmod = k * (1.0 + (a - 1.0) * ka_ref[...])

    li = lax.broadcasted_iota(jnp.int32, (chunk, chunk), 0)
    mi = lax.broadcasted_iota(jnp.int32, (chunk, chunk), 1)
    incl = (mi <= li).astype(F32)
    strict = (mi < li).astype(F32)
    cs = _cumsum_rows(jnp.broadcast_to(incl, (hb, chunk, chunk)), lw)
    dec_in = jnp.exp(cs)
    dec_ex = jnp.exp(cs - lw)
    inv = jnp.exp(-cs)
    p_rows = jnp.concatenate([-kk * dec_ex, r * dec_in], axis=1)
    q_rows = jnp.concatenate([kk * a * inv, kmod * inv], axis=1)
    mm = _bdot(p_rows, q_rows, _NT)
    a_ab = mm[:, :chunk, :chunk] * strict
    a_ak = mm[:, :chunk, chunk:] * strict
    r_b = mm[:, chunk:, :chunk] * incl
    r_k = mm[:, chunk:, chunk:] * incl
    ps = _bdot(p_rows, s0, _NT)
    x = ps[:, :chunk] + _bdot(a_ak, v, _NN)
    pw = a_ab
    n_iter = chunk.bit_length() - 1
    for it in range(n_iter):
        x = x + _bdot(pw, x, _NN)
        if it + 1 < n_iter:
            pw = _bdot(pw, pw, _NN)
    y = ps[:, chunk:] + _bdot(r_b, x, _NN) + _bdot(r_k, v, _NN)
    uv = jnp.concatenate([x, v], axis=1)
    s_new = (s0 + _bdot(uv, q_rows, _TN)) * dec_in[:, chunk - 1:chunk, :]
    st_s[...] = s_new
    sout_ref[0] = s_new

    mean = jnp.mean(y, axis=-1, keepdims=True)
    var = jnp.mean(jnp.square(y - mean), axis=-1, keepdims=True)
    yn = (y - mean) * lax.rsqrt(var + RWKV_LN_EPS) * lnw_ref[...] + lnb_ref[...]
    bonus = jnp.sum(r * kmod * rk_ref[...], axis=-1, keepdims=True) * v
    out = (yn + bonus) * g_ref[0]
    hd = out.shape[2]
    for hh in range(hb):
        y_ref[:, hh * hd:(hh + 1) * hd] = out[hh].astype(y_ref.dtype)


def rwkv_scan(r, k, v, lw, a, g, s0, prm):
    bsz, nh, t, hd = r.shape
    chunk = _pick(t, (64, 32, 16, 8))
    hb = _pick(nh, (32, 16, 8, 4, 2, 1))
    nc = t // chunk
    xspec = pl.BlockSpec((1, hb, chunk, hd), lambda b, h, c: (b, h, c, 0))
    pspec = pl.BlockSpec((hb, 1, hd), lambda b, h, c: (h, 0, 0))
    sspec = pl.BlockSpec((1, hb, hd, hd), lambda b, h, c: (b, h, 0, 0))
    vmem = 16 * hb * chunk * LANES * 4 + 40 * hb * 2 * chunk * LANES * 4 + 6 * hb * hd * LANES * 4
    y, s = pl.pallas_call(
        functools.partial(_rwkv_scan_kernel, chunk=chunk),
        grid=(bsz, nh // hb, nc),
        in_specs=[xspec] * 6 + [sspec] + [pspec] * 5,
        out_specs=(pl.BlockSpec((chunk, hb * hd), lambda b, h, c: (b * nc + c, h)), sspec),
        out_shape=(jax.ShapeDtypeStruct((bsz * t, nh * hd), BF16), jax.ShapeDtypeStruct((bsz, nh, hd, hd), F32)),
        scratch_shapes=[pltpu.VMEM((hb, hd, hd), F32)],
        compiler_params=_cparams(("parallel", "parallel", "arbitrary"), vmem),
        name="rwkv_scan",
    )(r, k, v, lw, a, g, s0, prm["k_k"], prm["k_a"], prm["r_k"], prm["ln_w"], prm["ln_b"])
    return y, s


def _router_kernel(x_ref, w_ref, comb_ref, *, n_experts):
    logits = jnp.dot(x_ref[...].astype(BF16), w_ref[...], preferred_element_type=F32)
    lane = lax.broadcasted_iota(jnp.int32, logits.shape, 1)
    big = jnp.int32(LANES)
    logits = jnp.where(lane < n_experts, logits, -jnp.inf)
    t1 = jnp.max(logits, axis=1, keepdims=True)
    i1 = jnp.min(jnp.where(logits == t1, lane, big), axis=1, keepdims=True)
    rest = jnp.where(lane == i1, -jnp.inf, logits)
    t2 = jnp.max(rest, axis=1, keepdims=True)
    i2 = jnp.min(jnp.where(rest == t2, lane, big), axis=1, keepdims=True)
    e2 = jnp.exp(t2 - t1)
    den = 1.0 + e2
    comb_ref[...] = jnp.where(lane == 0, 1.0 / den, jnp.where(lane == 1, e2 / den, jnp.where(
        lane == 2, i1.astype(F32), jnp.where(lane == 3, i2.astype(F32), 0.0))))


def moe_router(x, w_router):
    m, d = x.shape
    ne = w_router.shape[1]
    wp = jnp.zeros((d, LANES), BF16).at[:, :ne].set(w_router.astype(BF16))
    tm = _pick(m, (640, 512, 320, 256, 128, 64, 32, 16, 8))
    return pl.pallas_call(
        functools.partial(_router_kernel, n_experts=ne),
        grid=(m // tm,),
        in_specs=[pl.BlockSpec((tm, d), lambda i: (i, 0)), pl.BlockSpec((d, LANES), lambda i: (0, 0))],
        out_specs=pl.BlockSpec((tm, LANES), lambda i: (i, 0)),
        out_shape=jax.ShapeDtypeStruct((m, LANES), F32),
        compiler_params=_cparams(("parallel",), 3 * tm * d * 4 + 2 * d * LANES * 2 + 8 * tm * LANES * 4),
        name="moe_router",
    )(x, wp)


MOE_TILE = 512


def _moe_plan(rout, n_exp, tile):
    n = rout.shape[0]
    pair_e = rout[:, 2:4].astype(jnp.int32).reshape(-1)
    onehot = (pair_e[:, None] == jnp.arange(n_exp, dtype=jnp.int32)[None, :]).astype(jnp.int32)
    csum = jnp.cumsum(onehot, axis=0)
    rank = jnp.take_along_axis(csum - onehot, pair_e[:, None], axis=1)[:, 0]
    gsz = (csum[-1] + tile - 1) // tile * tile
    gend = jnp.cumsum(gsz)
    slot = ((gend - gsz)[pair_e] + rank).astype(jnp.int32)
    n_tiles = -(-2 * n // tile) + n_exp
    tok = jnp.zeros((n_tiles * tile,), jnp.int32).at[slot].set(jnp.arange(2 * n, dtype=jnp.int32) // 2)
    tile_start = jnp.arange(n_tiles, dtype=jnp.int32) * tile
    te = jnp.minimum(jnp.searchsorted(gend, tile_start, side="right"), n_exp - 1).astype(jnp.int32)
    used = (gend[-1] // tile).astype(jnp.int32).reshape(1)
    return slot, tok, te, used, n_tiles


def _row_copy(src_hbm, row, dst, r, sem):
    return pltpu.make_async_copy(src_hbm.at[pl.ds(row, 1)], dst.at[pl.ds(r, 1)], sem)


def _moe_gather_kernel(tok_ref, used_ref, x_hbm, o_ref, buf, sem, *, tile):
    i = pl.program_id(0)
    used = used_ref[0]

    def issue_tile(t_idx):
        slot = t_idx % 2
        base = t_idx * tile

        def issue(r, c):
            _row_copy(x_hbm, tok_ref[base + r], buf.at[slot], r, sem.at[slot]).start()
            return c

        lax.fori_loop(0, tile, issue, 0)

    @pl.when((i == 0) & (used > 0))
    def _():
        issue_tile(i)

    @pl.when(i + 1 < used)
    def _():
        issue_tile(i + 1)

    @pl.when(i < used)
    def _():
        slot = i % 2

        def wait(r, c):
            _row_copy(x_hbm, 0, buf.at[slot], r, sem.at[slot]).wait()
            return c

        lax.fori_loop(0, tile, wait, 0)
        o_ref[...] = buf[slot].astype(o_ref.dtype)

    @pl.when(i >= used)
    def _():
        o_ref[...] = jnp.zeros(o_ref.shape, o_ref.dtype)


def moe_gather(x, tok, used, n_tiles, tile):
    d = x.shape[1]
    return pl.pallas_call(
        functools.partial(_moe_gather_kernel, tile=tile),
        grid_spec=pltpu.PrefetchScalarGridSpec(
            num_scalar_prefetch=2, grid=(n_tiles,),
            in_specs=[pl.BlockSpec(memory_space=pl.ANY)],
            out_specs=pl.BlockSpec((tile, d), lambda i, tok_r, used_r: (i, 0)),
            scratch_shapes=[pltpu.VMEM((2, tile, d), x.dtype), pltpu.SemaphoreType.DMA((2,))]),
        out_shape=jax.ShapeDtypeStruct((n_tiles * tile, d), BF16),
        compiler_params=_cparams(("arbitrary",), 5 * tile * d * 4),
        name="moe_gather",
    )(tok, used, x)


def _moe_up_kernel(te_ref, used_ref, x_ref, wg_ref, wu_ref, wd_ref, o_ref, wdo_ref, wg_s, wu_s):
    i = pl.program_id(1)
    wdo_ref[...] = wd_ref[...].astype(BF16)

    @pl.when((i == 0) | (te_ref[i] != te_ref[jnp.maximum(i - 1, 0)]))
    def _():
        wg_s[...] = wg_ref[...].astype(BF16)
        wu_s[...] = wu_ref[...].astype(BF16)

    @pl.when(i < used_ref[0])
    def _():
        x = x_ref[...]
        g = jnp.dot(x, wg_s[...], preferred_element_type=F32)
        u = jnp.dot(x, wu_s[...], preferred_element_type=F32)
        o_ref[...] = (g * jax.nn.sigmoid(g) * u).astype(o_ref.dtype)

    @pl.when(i >= used_ref[0])
    def _():
        o_ref[...] = jnp.zeros(o_ref.shape, o_ref.dtype)


def moe_up(xs, wg, wu, wd, te, used, tile):
    p_rows, d = xs.shape
    n_exp, f = wg.shape[0], wg.shape[2]
    tn = _pick(f, (512, 256, 128))
    nj, ni = f // tn, p_rows // tile
    wd_rows = n_exp * f
    rows_c = next(c for c in (16, 32, 64, 128, 256, 512, 1024, 2048, 4096, wd_rows)
                  if wd_rows % c == 0 and wd_rows // c <= nj * ni)
    last_c = wd_rows // rows_c - 1
    cspec = pl.BlockSpec((rows_c, wd.shape[2]), lambda j, i, te_r, used_r: (jnp.minimum(j * ni + i, last_c), 0))
    wspec = pl.BlockSpec((None, d, tn), lambda j, i, te_r, used_r: (te_r[i], 0, j))
    vmem = 2 * tile * d * 2 + 4 * d * tn * 4 + 2 * d * tn * 2 + 6 * tile * tn * 4 + 12 * rows_c * wd.shape[2]
    act, wd_b = pl.pallas_call(
        _moe_up_kernel,
        grid_spec=pltpu.PrefetchScalarGridSpec(
            num_scalar_prefetch=2, grid=(nj, ni),
            in_specs=[pl.BlockSpec((tile, d), lambda j, i, te_r, used_r: (i, 0)), wspec, wspec, cspec],
            out_specs=(pl.BlockSpec((tile, tn), lambda j, i, te_r, used_r: (i, j)), cspec),
            scratch_shapes=[pltpu.VMEM((d, tn), BF16), pltpu.VMEM((d, tn), BF16)]),
        out_shape=(jax.ShapeDtypeStruct((p_rows, f), BF16), jax.ShapeDtypeStruct((wd_rows, wd.shape[2]), BF16)),
        compiler_params=_cparams(("arbitrary", "arbitrary"), vmem),
        name="moe_up",
    )(te, used, xs, wg, wu, wd.reshape(wd_rows, wd.shape[2]))
    return act, wd_b.reshape(wd.shape)


def _moe_down_kernel(te_ref, used_ref, x_ref, w_ref, o_ref, *, per_tile):
    i = pl.program_id(1)

    @pl.when(i < used_ref[0] * per_tile)
    def _():
        o_ref[...] = jnp.dot(x_ref[...], w_ref[...], preferred_element_type=F32)

    @pl.when(i >= used_ref[0] * per_tile)
    def _():
        o_ref[...] = jnp.zeros(o_ref.shape, o_ref.dtype)


def moe_down(act, wd, te, used, tile):
    p_rows, f = act.shape
    d = wd.shape[2]
    tn = _pick(d, (512, 256, 128))
    rows = _pick(tile, (256, 128, 64, 32, 16, 8))
    per_tile = tile // rows
    vmem = 2 * rows * f * 2 + 2 * f * tn * 2 + 4 * rows * tn * 4
    return pl.pallas_call(
        functools.partial(_moe_down_kernel, per_tile=per_tile),
        grid_spec=pltpu.PrefetchScalarGridSpec(
            num_scalar_prefetch=2, grid=(d // tn, p_rows // rows),
            in_specs=[pl.BlockSpec((rows, f), lambda j, i, te_r, used_r: (i, 0)),
                      pl.BlockSpec((None, f, tn), lambda j, i, te_r, used_r: (te_r[i // per_tile], 0, j))],
            out_specs=pl.BlockSpec((rows, tn), lambda j, i, te_r, used_r: (i, j))),
        out_shape=jax.ShapeDtypeStruct((p_rows, d), F32),
        compiler_params=_cparams(("parallel", "arbitrary"), vmem),
        name="moe_down",
    )(te, used, act, wd)


def _moe_combine_kernel(slot_ref, h_ref, g_ref, ys_hbm, fn_ref, o_ref, buf, sem, *, tc, rb0, nsteps):
    i = pl.program_id(0)

    def issue_tile(t_idx):
        bs = t_idx % 2
        base = (rb0 + t_idx) * tc

        def issue(r, c):
            p = 2 * (base + r)
            _row_copy(ys_hbm, slot_ref[p], buf.at[bs, 0], r, sem.at[bs]).start()
            _row_copy(ys_hbm, slot_ref[p + 1], buf.at[bs, 1], r, sem.at[bs]).start()
            return c

        lax.fori_loop(0, tc, issue, 0)

    @pl.when(i == 0)
    def _():
        issue_tile(i)

    @pl.when(i + 1 < nsteps)
    def _():
        issue_tile(i + 1)

    bs = i % 2

    def wait(r, c):
        _row_copy(ys_hbm, 0, buf.at[bs, 0], r, sem.at[bs]).wait()
        _row_copy(ys_hbm, 0, buf.at[bs, 1], r, sem.at[bs]).wait()
        return c

    lax.fori_loop(0, tc, wait, 0)
    g = g_ref[...]
    x = h_ref[...] + (g[:, 0:1] * buf[bs, 0] + g[:, 1:2] * buf[bs, 1])
    y = x * lax.rsqrt(jnp.mean(x * x, axis=-1, keepdims=True) + NORM_EPS)
    o_ref[...] = y * fn_ref[...]


def moe_combine_norm(h, rout, ys, slot, final_norm, row0, n):
    d = h.shape[1]
    tc = _pick(math.gcd(n, row0) if row0 else n, (256, 128, 64, 32, 16, 8))
    rb0 = row0 // tc
    return pl.pallas_call(
        functools.partial(_moe_combine_kernel, tc=tc, rb0=rb0, nsteps=n // tc),
        grid_spec=pltpu.PrefetchScalarGridSpec(
            num_scalar_prefetch=1, grid=(n // tc,),
            in_specs=[pl.BlockSpec((tc, d), lambda i, s: (rb0 + i, 0)),
                      pl.BlockSpec((tc, LANES), lambda i, s: (rb0 + i, 0)),
                      pl.BlockSpec(memory_space=pl.ANY),
                      pl.BlockSpec((1, d), lambda i, s: (0, 0))],
            out_specs=pl.BlockSpec((tc, d), lambda i, s: (i, 0)),
            scratch_shapes=[pltpu.VMEM((2, 2, tc, d), F32), pltpu.SemaphoreType.DMA((2,))]),
        out_shape=jax.ShapeDtypeStruct((n, d), F32),
        compiler_params=_cparams(("arbitrary",), 10 * tc * d * 4),
        name="moe_combine_norm",
    )(slot, h, rout, ys, final_norm.astype(F32).reshape(1, d))


def _rope_tables(pos, rope, width, lead):
    inv = ROPE_THETA ** (-jnp.arange(0, rope, 2, dtype=F32) / rope)
    ang = pos.astype(F32)[:, None] * inv[None, :]
    cos, sin = jnp.cos(ang), jnp.sin(ang)
    n = pos.shape[0]
    ctab = jnp.concatenate([jnp.ones((n, lead), F32), cos, cos, jnp.zeros((n, width - lead - rope), F32)], axis=1)
    stab = jnp.concatenate([jnp.zeros((n, lead), F32), -sin, sin, jnp.zeros((n, width - lead - rope), F32)], axis=1)
    return ctab, stab


def _swap_halves(w):
    half = w.shape[-1] // 2
    return jnp.concatenate([w[..., half:], w[..., :half]], axis=-1)


def kernel(x_prompt, x_sample, cache_mla_ckv, cache_mla_kpe, state_s5_re, state_s5_im, state_rwkv_wkv, state_rwkv_shift, cache_fox_k, cache_fox_v, cache_fox_logf, ln0_mix, w_in0, s5_a_re, s5_a_im, s5_log_dt, s5_b_re, s5_b_im, s5_c_re, s5_c_im, s5_d, s5_w_glu, s5_b_glu, mla_q_norm, mla_w_q_up, mla_kv_norm, mla_w_uk, mla_w_uv, w_out0, ln0_ffn, ffn_w_gate, ffn_w_up, ffn_w_down, ln1_mix, w_in1, rwkv_mu, rwkv_w0, rwkv_w2, rwkv_a0, rwkv_a2, rwkv_g2, rwkv_k_k, rwkv_k_a, rwkv_r_k, rwkv_ln_w, rwkv_ln_b, fox_b_f, w_out1, ln1_ffn, moe_w_router, moe_w_gate, moe_w_up, moe_w_down, final_norm):
    bp, tp, d = x_prompt.shape
    bs, ts, _ = x_sample.shape
    past = cache_mla_ckv.shape[1]
    n_p, n_s = bp * tp, bs * ts
    ntok = n_p + n_s
    streams = ((0, bp, tp), (n_p, bs, ts))

    h = jnp.concatenate([x_prompt.reshape(n_p, d), x_sample.reshape(n_s, d)], axis=0)

    s5_w = s5_d.shape[0]
    q_rank = mla_q_norm.shape[0]
    kv_rank = mla_kv_norm.shape[0]
    n_mh, qk = mla_w_q_up.shape[1], mla_w_q_up.shape[2]
    nope = mla_w_uk.shape[2]
    rope = qk - nope
    vdim = mla_w_uv.shape[2]
    mla_scale = float(qk) ** -0.5
    qw = 2 * LANES
    kvw = kv_rank + LANES

    (xn,) = rmsnorm(h, ln0_mix, (BF16,))
    w_in0b = w_in0.astype(BF16)
    (z_uq,) = matmul(xn, w_in0b[:, :s5_w + q_rank])
    off_kv = s5_w + q_rank
    w_kpe = w_in0b[:, off_kv + kv_rank:]
    zpad = jnp.zeros((d, kvw - kv_rank - rope), BF16)
    w_kv1 = jnp.concatenate([w_in0b[:, off_kv:off_kv + kv_rank], w_kpe, zpad], axis=1)
    w_kv2 = jnp.concatenate([jnp.zeros((d, kv_rank), BF16), _swap_halves(w_kpe), zpad], axis=1)
    pos_p = jnp.arange(tp)
    pos_s = past + jnp.arange(ts)

    def token_tables(width, lead):
        cp, sp = _rope_tables(pos_p, rope, width, lead)
        cs, ss = _rope_tables(pos_s, rope, width, lead)
        return (jnp.concatenate([jnp.tile(cp, (bp, 1)), jnp.tile(cs, (bs, 1))], axis=0),
                jnp.concatenate([jnp.tile(sp, (bp, 1)), jnp.tile(ss, (bs, 1))], axis=0))

    ckv_c, ckv_s = token_tables(kvw, kv_rank)
    z_kv = rope_matmul(xn, w_kv1, w_kv2, ckv_c, ckv_s, F32, tn=kvw)
    ckv_f, ckv_b = rmsnorm(z_kv, mla_kv_norm, (F32, BF16), col_block=0, width=kv_rank)
    kpe_f = z_kv[:, kv_rank:kv_rank + rope]
    kpe_b = z_kv[:, kv_rank:].astype(BF16)

    (cqn,) = rmsnorm(z_uq, mla_q_norm, (BF16,), col_block=s5_w // q_rank, width=q_rank)
    wq = mla_w_q_up.astype(BF16)
    zq = jnp.zeros((q_rank, n_mh, qw - qk), BF16)
    wq1 = jnp.concatenate([wq, zq], axis=2).reshape(q_rank, n_mh * qw)
    wq2 = jnp.concatenate([jnp.zeros((q_rank, n_mh, nope), BF16), _swap_halves(wq[:, :, nope:]), zq],
                          axis=2).reshape(q_rank, n_mh * qw)
    q_c, q_s = token_tables(qw, nope)
    q_all = rope_matmul(cqn, wq1, wq2, q_c, q_s, BF16, tn=qw)

    s5p = s5_params(s5_a_re, s5_a_im, s5_log_dt, s5_b_re, s5_b_im, s5_c_re, s5_c_im, s5_d, s5_w_glu, s5_b_glu)
    g5, p5 = s5_a_re.shape
    wuk = jnp.transpose(mla_w_uk, (1, 2, 0)).astype(BF16)
    wuv = jnp.transpose(mla_w_uv, (1, 0, 2)).astype(BF16)

    y_s5, s5_re, s5_im, y_mla = [], [], [], []
    for si, (off, bsz, t) in enumerate(streams):
        if si == 0:
            h0r = jnp.zeros((bsz, g5 * p5), F32)
            h0i = h0r
            ckv_k = ckv_b[off:off + bsz * t].reshape(bsz, t, kv_rank)
            kpe_k = kpe_b[off:off + bsz * t].reshape(bsz, t, LANES)
            klen, causal = t, True
        else:
            h0r, h0i = state_s5_re.astype(F32), state_s5_im.astype(F32)
            klen, causal = past + t, False
            padk = -(-klen // LANES) * LANES - klen
            ckv_k = jnp.concatenate([cache_mla_ckv.astype(BF16), ckv_b[off:off + bsz * t].reshape(bsz, t, kv_rank),
                                     jnp.zeros((bsz, padk, kv_rank), BF16)], axis=1)
            kpe_cache = jnp.concatenate([cache_mla_kpe.astype(BF16),
                                         jnp.zeros((bsz, past, LANES - rope), BF16)], axis=2)
            kpe_k = jnp.concatenate([kpe_cache, kpe_b[off:off + bsz * t].reshape(bsz, t, LANES),
                                     jnp.zeros((bsz, padk, LANES), BF16)], axis=1)
        ys, hr, hi = s5_mixer(z_uq, off, bsz, t, h0r, h0i, s5p)
        y_s5.append(ys)
        s5_re.append(hr.reshape(bsz, g5, p5))
        s5_im.append(hi.reshape(bsz, g5, p5))
        y_mla.append(mla_attention(q_all, off, bsz, t, ckv_k, kpe_k, wuk, wuv, klen, causal, mla_scale))

    h = matmul_pair(jnp.concatenate(y_s5, axis=0), jnp.concatenate(y_mla, axis=0), w_out0.astype(BF16), h)
    (hn,) = rmsnorm(h, ln0_ffn, (BF16,))
    act = swiglu_up(hn, ffn_w_gate, ffn_w_up)
    (h,) = matmul(act, ffn_w_down.astype(BF16), res=h, tm=_pick(ntok, (640, 512, 256, 128, 64, 32, 16, 8)),
                  tn=256, tk=ffn_w_down.shape[0])

    rw = rwkv_w0.shape[0]
    nh_r, hd_r = rwkv_k_k.shape
    shift_w = rwkv_mu.shape[0]
    lora_w = shift_w - 3 * rw
    slab = -(-lora_w // LANES) * LANES
    wz = 3 * rw + slab
    nh_f = fox_b_f.shape[0]
    fw = (w_in1.shape[1] - shift_w - nh_f) // 3
    fox_scale = float(fw // nh_f) ** -0.5
    d_lw, d_la = rwkv_w2.shape[0], rwkv_a2.shape[0]

    (xn,) = rmsnorm(h, ln1_mix, (BF16,))
    w_in1b = w_in1.astype(BF16)
    (zr,) = matmul(xn, jnp.pad(w_in1b[:, :shift_w], ((0, 0), (0, wz - shift_w))))
    (fq,) = matmul(xn, w_in1b[:, shift_w:shift_w + fw], (BF16,))
    fk = [matmul_heads(xn, w_in1b[:, shift_w + fw:shift_w + 2 * fw], nh_f, (o_, b_ * t_)) for o_, b_, t_ in streams]
    fv = [matmul_heads(xn, w_in1b[:, shift_w + 2 * fw:shift_w + 3 * fw], nh_f, (o_, b_ * t_)) for o_, b_, t_ in streams]
    (zf,) = matmul(xn, jnp.pad(w_in1b[:, shift_w + 3 * fw:], ((0, 0), (0, LANES - nh_f))))

    padrow = lambda wgt, lo: jnp.zeros((slab, rw), BF16).at[lo:lo + wgt.shape[0]].set(wgt.astype(BF16))
    row2 = lambda x_, n_: x_.astype(F32).reshape(1, n_)
    head3 = lambda x_: x_.astype(F32).reshape(nh_r, 1, hd_r)
    rprm = dict(mu=jnp.pad(row2(rwkv_mu, shift_w), ((0, 0), (0, wz - shift_w))), w0=row2(rwkv_w0, rw),
                a0=row2(rwkv_a0, rw), w2=padrow(rwkv_w2, 0), a2=padrow(rwkv_a2, d_lw),
                g2=padrow(rwkv_g2, d_lw + d_la), k_k=head3(rwkv_k_k), k_a=head3(rwkv_k_a), r_k=head3(rwkv_r_k),
                ln_w=head3(rwkv_ln_w), ln_b=head3(rwkv_ln_b))

    y_r, wkv, shift_new, y_f, logf_out = [], [], [], [], []
    for si, (off, bsz, t) in enumerate(streams):
        if si == 0:
            shift_prev = jnp.zeros((bsz, wz), F32)
            s0 = jnp.zeros((bsz, nh_r, hd_r, hd_r), F32)
            pre = None
            k_all, v_all, tkeys = fk[si][1], fv[si][1], t
            qoff = 0
        else:
            shift_prev = jnp.pad(state_rwkv_shift.astype(F32), ((0, 0), (0, wz - shift_w)))
            s0 = state_rwkv_wkv.astype(F32)
            pre = jnp.pad(cache_fox_logf.astype(F32), ((0, 0), (0, 0), (0, LANES - nh_f)))
        parts = rwkv_prep(zr, off, bsz, t, shift_prev, rprm, nh_r)
        yr, s_fin = rwkv_scan(*parts, s0, rprm)
        y_r.append(yr)
        wkv.append(s_fin)
        shift_new.append(zr[off + t - 1:off + bsz * t:t, :shift_w])
        logf, cum, cumt = fox_gate(zf, off, bsz, t, fox_b_f, pre)
        logf_out.append(logf[:, :, :nh_f])
        if si == 0:
            y_f.append(fox_attention(fq, off, bsz, t, k_all, v_all, tkeys, cum, cumt, qoff, fox_scale))
        else:
            y_f.append(fox_decode_attention(fq, off, bsz, t, cache_fox_k.astype(F32), cache_fox_v.astype(F32),
                                            fk[si][1], fv[si][1], cum, fox_scale))

    h = matmul_pair(jnp.concatenate(y_r, axis=0), jnp.concatenate(y_f, axis=0), w_out1.astype(BF16), h)
    (hn,) = rmsnorm(h, ln1_ffn, (F32,))
    rout = moe_router(hn, moe_w_router)
    n_exp = moe_w_gate.shape[0]
    slot, tok, te, used, n_tiles = _moe_plan(rout, n_exp, MOE_TILE)
    xs = moe_gather(hn, tok, used, n_tiles, MOE_TILE)
    act, wd_b = moe_up(xs, moe_w_gate, moe_w_up, moe_w_down, te, used, MOE_TILE)
    ys = moe_down(act, wd_b, te, used, MOE_TILE)
    nfh = fw // nh_f
    outs = [moe_combine_norm(h, rout, ys, slot, final_norm, off, bsz * t).reshape(bsz, t, d) for off, bsz, t in streams]
    for si, (off, bsz, t) in enumerate(streams):
        rows = slice(off, off + bsz * t)
        outs += [ckv_f[rows].reshape(bsz, t, kv_rank), kpe_f[rows].reshape(bsz, t, rope), s5_re[si], s5_im[si],
                 wkv[si], shift_new[si], fk[si][0].reshape(bsz, t, nh_f, nfh), fv[si][0].reshape(bsz, t, nh_f, nfh),
                 logf_out[si]]
    return tuple(outs)
```

```python
import functools
import math

import jax
import jax.numpy as jnp
from jax import lax
from jax.experimental import pallas as pl
from jax.experimental.pallas import tpu as pltpu

F32 = jnp.float32
BF16 = jnp.bfloat16

V7X_VMEM_BYTES = 64 * 1024 * 1024
VMEM_CAP = V7X_VMEM_BYTES - 4 * 1024 * 1024
LANES = 128

NORM_EPS = 1e-6
NEG_INF = -1e30
CHUNK = 64
ROPE_THETA = 10000.0
RWKV_LN_EPS = 64e-5
S5_BLOCK_GROUPS = 8


def _pick(n, cands):
    for c in cands:
        if c <= n and n % c == 0:
            return c
    return n


def _cparams(sem, vmem_bytes):
    limit = int(min(max(vmem_bytes * 1.25 + (4 << 20), 24 << 20), VMEM_CAP))
    return pltpu.CompilerParams(dimension_semantics=sem, vmem_limit_bytes=limit)


def _rmsnorm_kernel(x_ref, g_ref, *o_refs):
    x = x_ref[...].astype(F32)
    y = x * lax.rsqrt(jnp.mean(x * x, axis=-1, keepdims=True) + NORM_EPS)
    y = y * g_ref[...]
    for o in o_refs:
        o[...] = y.astype(o.dtype)


def rmsnorm(x, g, out_dtypes, col_block=0, width=None):
    m = x.shape[0]
    width = x.shape[1] if width is None else width
    tm = _pick(m, (512, 320, 256, 128, 64, 32, 16, 8))
    outs = tuple(jax.ShapeDtypeStruct((m, width), d) for d in out_dtypes)
    res = pl.pallas_call(
        _rmsnorm_kernel,
        grid=(m // tm,),
        in_specs=[pl.BlockSpec((tm, width), lambda i: (i, col_block)),
                  pl.BlockSpec((1, width), lambda i: (0, 0))],
        out_specs=tuple(pl.BlockSpec((tm, width), lambda i: (i, 0)) for _ in out_dtypes),
        out_shape=outs,
        compiler_params=_cparams(("parallel",), tm * width * 4 * 2 * (1 + len(out_dtypes))),
        name="rmsnorm",
    )(x, g.reshape(1, width).astype(F32))
    return res


def _mm_kernel(*refs, nk, has_res, n_out):
    x_ref, w_ref = refs[0], refs[1]
    pos = 2
    res_ref = None
    if has_res:
        res_ref = refs[pos]
        pos += 1
    o_refs = refs[pos:pos + n_out]
    acc_ref = refs[pos + n_out] if nk > 1 else None

    part = jnp.dot(x_ref[...].astype(BF16), w_ref[...].astype(BF16), preferred_element_type=F32)

    def finish(acc):
        if has_res:
            acc = res_ref[...] + acc
        for o in o_refs:
            o[...] = acc.astype(o.dtype)

    if nk == 1:
        finish(part)
    else:
        k = pl.program_id(2)

        @pl.when(k == 0)
        def _():
            acc_ref[...] = part

        @pl.when(k > 0)
        def _():
            acc_ref[...] += part

        @pl.when(k == nk - 1)
        def _():
            finish(acc_ref[...])


def matmul(x, w, out_dtypes=(F32,), res=None, tm=None, tn=None, tk=None, rows=None):
    row0, m = rows if rows is not None else (0, x.shape[0])
    kdim = x.shape[1]
    n = w.shape[-1]
    tm = tm or _pick(math.gcd(m, row0) if row0 else m, (1280, 1024, 640, 512, 320, 256, 128, 64, 32, 16, 8))
    tn = tn or _pick(n, (512, 384, 256, 128))
    tk = tk or (kdim if kdim <= 4096 else _pick(kdim, (2048, 1792, 1024, 512, 256, 128)))
    nk = kdim // tk
    grid = (m // tm, n // tn, nk)
    rb0 = row0 // tm
    in_specs = [pl.BlockSpec((tm, tk), lambda i, j, k: (rb0 + i, k)),
                pl.BlockSpec((tk, tn), lambda i, j, k: (k, j))]
    args = [x, w]
    if res is not None:
        in_specs.append(pl.BlockSpec((tm, tn), lambda i, j, k: (i, j)))
        args.append(res)
    out_specs = tuple(pl.BlockSpec((tm, tn), lambda i, j, k: (i, j)) for _ in out_dtypes)
    out_shape = tuple(jax.ShapeDtypeStruct((m, n), d) for d in out_dtypes)
    scratch = [pltpu.VMEM((tm, tn), F32)] if nk > 1 else []
    vmem = (2 * tm * tk * x.dtype.itemsize + 2 * tk * tn * w.dtype.itemsize
            + tm * tn * 4 * (2 * len(out_dtypes) + 1 + (2 if res is not None else 0)))
    outs = pl.pallas_call(
        functools.partial(_mm_kernel, nk=nk, has_res=res is not None, n_out=len(out_dtypes)),
        grid=grid, in_specs=in_specs, out_specs=out_specs, out_shape=out_shape,
        scratch_shapes=scratch,
        compiler_params=_cparams(("parallel", "parallel", "arbitrary"), vmem),
        name="matmul",
    )(*args)
    return outs


def _mm2_kernel(xa_ref, xb_ref, wa_ref, wb_ref, res_ref, o_ref):
    acc = jnp.dot(xa_ref[...], wa_ref[...], preferred_element_type=F32)
    acc = acc + jnp.dot(xb_ref[...], wb_ref[...], preferred_element_type=F32)
    o_ref[...] = res_ref[...] + acc


def matmul_pair(xa, xb, w, res):
    m, ka = xa.shape
    kb = xb.shape[1]
    n = w.shape[1]
    tm = _pick(m, (1280, 1024, 640, 512, 320, 256, 128, 64, 32, 16, 8))
    tn = _pick(n, (512, 384, 256, 128))
    vmem = 2 * tm * (ka + kb) * 2 + 2 * (ka + kb) * tn * 2 + 5 * tm * tn * 4
    return pl.pallas_call(
        _mm2_kernel,
        grid=(m // tm, n // tn),
        in_specs=[pl.BlockSpec((tm, ka), lambda i, j: (i, 0)), pl.BlockSpec((tm, kb), lambda i, j: (i, 0)),
                  pl.BlockSpec((ka, tn), lambda i, j: (0, j)), pl.BlockSpec((kb, tn), lambda i, j: (0, j)),
                  pl.BlockSpec((tm, tn), lambda i, j: (i, j))],
        out_specs=pl.BlockSpec((tm, tn), lambda i, j: (i, j)),
        out_shape=jax.ShapeDtypeStruct((m, n), F32),
        compiler_params=_cparams(("parallel", "parallel"), vmem),
        name="matmul_pair",
    )(xa, xb, w[:ka], w[ka:], res)


def _mm_heads_kernel(x_ref, w_ref, o3_ref, ob_ref):
    acc = jnp.dot(x_ref[...], w_ref[...], preferred_element_type=F32)
    ob_ref[...] = acc.astype(ob_ref.dtype)
    hd = o3_ref.shape[2]
    for hh in range(o3_ref.shape[1]):
        o3_ref[:, hh, :] = acc[:, hh * hd:(hh + 1) * hd]


def matmul_heads(x, w, nheads, rows):
    row0, m = rows
    kdim, n = w.shape
    hd = n // nheads
    tm = _pick(math.gcd(m, row0) if row0 else m, (256, 128, 64, 32, 16, 8))
    rb0 = row0 // tm
    vmem = 2 * tm * kdim * 2 + 2 * kdim * n * 2 + 8 * tm * n * 4
    return pl.pallas_call(
        _mm_heads_kernel,
        grid=(m // tm,),
        in_specs=[pl.BlockSpec((tm, kdim), lambda i: (rb0 + i, 0)), pl.BlockSpec((kdim, n), lambda i: (0, 0))],
        out_specs=(pl.BlockSpec((tm, nheads, hd), lambda i: (i, 0, 0)), pl.BlockSpec((tm, n), lambda i: (i, 0))),
        out_shape=(jax.ShapeDtypeStruct((m, nheads, hd), F32), jax.ShapeDtypeStruct((m, n), BF16)),
        compiler_params=_cparams(("parallel",), vmem),
        name="matmul_heads",
    )(x, w)


def _rope_mm_kernel(x_ref, w1_ref, w2_ref, c_ref, s_ref, o_ref):
    x = x_ref[...]
    a = jnp.dot(x, w1_ref[...], preferred_element_type=F32)
    b = jnp.dot(x, w2_ref[...], preferred_element_type=F32)
    o_ref[...] = (a * c_ref[...] + b * s_ref[...]).astype(o_ref.dtype)


def rope_matmul(x, w1, w2, ctab, stab, out_dtype, tn):
    m, kdim = x.shape
    n = w1.shape[1]
    est = lambda rows: 2 * rows * kdim * 2 + 4 * kdim * tn * 2 + 8 * rows * tn * 4
    tm = next((c for c in (1280, 1024, 640, 512, 320, 256, 128, 64, 32, 16, 8)
               if m % c == 0 and est(c) <= VMEM_CAP // 2), 8)
    vmem = est(tm)
    return pl.pallas_call(
        _rope_mm_kernel,
        grid=(m // tm, n // tn),
        in_specs=[pl.BlockSpec((tm, kdim), lambda i, j: (i, 0)),
                  pl.BlockSpec((kdim, tn), lambda i, j: (0, j)),
                  pl.BlockSpec((kdim, tn), lambda i, j: (0, j)),
                  pl.BlockSpec((tm, tn), lambda i, j: (i, 0)),
                  pl.BlockSpec((tm, tn), lambda i, j: (i, 0))],
        out_specs=pl.BlockSpec((tm, tn), lambda i, j: (i, j)),
        out_shape=jax.ShapeDtypeStruct((m, n), out_dtype),
        compiler_params=_cparams(("parallel", "parallel"), vmem),
        name="rope_matmul",
    )(x, w1, w2, ctab, stab)


def _swiglu_up_kernel(x_ref, wg_ref, wu_ref, o_ref):
    x = x_ref[...]
    g = jnp.dot(x, wg_ref[...].astype(BF16), preferred_element_type=F32)
    u = jnp.dot(x, wu_ref[...].astype(BF16), preferred_element_type=F32)
    o_ref[...] = (g * jax.nn.sigmoid(g) * u).astype(o_ref.dtype)


def swiglu_up(x, wg, wu):
    m, kdim = x.shape
    n = wg.shape[-1]
    tm = _pick(m, (1280, 1024, 640, 512, 320, 256, 128, 64, 32, 16, 8))
    tn = _pick(n, (256, 128))
    wspec = pl.BlockSpec((kdim, tn), lambda i, j: (0, j))
    vmem = 2 * tm * kdim * 2 + 4 * kdim * tn * wg.dtype.itemsize + 6 * tm * tn * 4
    return pl.pallas_call(
        _swiglu_up_kernel,
        grid=(m // tm, n // tn),
        in_specs=[pl.BlockSpec((tm, kdim), lambda i, j: (i, 0)), wspec, wspec],
        out_specs=pl.BlockSpec((tm, tn), lambda i, j: (i, j)),
        out_shape=jax.ShapeDtypeStruct((m, n), BF16),
        compiler_params=_cparams(("parallel", "parallel"), vmem),
        name="swiglu_up",
    )(x, wg, wu)


def _s5_kernel(u_ref, h0r_ref, h0i_ref, lr_ref, li_ref, bbr_ref, bbi_ref, ccr_ref, cci_ref,
               d_ref, wglu_ref, bglu_ref, y_ref, hr_ref, hi_ref, xr_s, xi_s, st_r, st_i, *, tc, nblk):
    c = pl.program_id(1)

    @pl.when(c == 0)
    def _():
        st_r[...] = h0r_ref[0]
        st_i[...] = h0i_ref[0]

    u = u_ref[...]
    ub = u.astype(BF16)
    sw = xr_s.shape[1] // nblk
    for k in range(nblk):
        uk = ub[:, k * LANES:(k + 1) * LANES]
        xr_s[:, k * sw:(k + 1) * sw] = jnp.dot(uk, bbr_ref[k], preferred_element_type=F32)
        xi_s[:, k * sw:(k + 1) * sw] = jnp.dot(uk, bbi_ref[k], preferred_element_type=F32)

    scan_w = 2048
    for q in range(xr_s.shape[1] // scan_w):
        cols = slice(q * scan_w, (q + 1) * scan_w)
        lr = lr_ref[:, cols]
        li = li_ref[:, cols]

        def body(t, carry, cols=cols, lr=lr, li=li):
            hr, hi = carry
            nr = lr * hr - li * hi + xr_s[pl.ds(t, 1), cols]
            ni = lr * hi + li * hr + xi_s[pl.ds(t, 1), cols]
            xr_s[pl.ds(t, 1), cols] = nr
            xi_s[pl.ds(t, 1), cols] = ni
            return nr, ni

        hr, hi = lax.fori_loop(0, tc, body, (st_r[:, cols], st_i[:, cols]))
        st_r[:, cols] = hr
        st_i[:, cols] = hi

    hr_ref[0] = st_r[...]
    hi_ref[0] = st_i[...]

    ys = []
    for k in range(nblk):
        xr = xr_s[:, k * sw:(k + 1) * sw].astype(BF16)
        xi = xi_s[:, k * sw:(k + 1) * sw].astype(BF16)
        ys.append(jnp.dot(xr, ccr_ref[k], preferred_element_type=F32)
                  - jnp.dot(xi, cci_ref[k], preferred_element_type=F32))
    y = jnp.concatenate(ys, axis=1) + d_ref[...] * u
    y = jax.nn.gelu(y)
    gate = jax.nn.sigmoid(jnp.dot(y.astype(BF16), wglu_ref[...], preferred_element_type=F32) + bglu_ref[...])
    y_ref[...] = (y * gate).astype(y_ref.dtype)


def s5_mixer(z, row_off, bsz, t, h0_re, h0_im, prm):
    width = prm["d"].shape[1]
    nblk = width // LANES
    nstate = prm["lr"].shape[1]
    tc = _pick(t, (256, 128, 64, 32, 16, 8))
    nt = t // tc
    rb0 = row_off // tc
    full = lambda shape: pl.BlockSpec(shape, lambda b, c: (0,) * len(shape))
    vmem = (4 * tc * width * 4 + 2 * tc * nstate * 4 + 4 * nblk * LANES * (nstate // nblk) * 2 * 2
            + 2 * width * width * 2 + 8 * tc * width * 4)
    y, hr, hi = pl.pallas_call(
        functools.partial(_s5_kernel, tc=tc, nblk=nblk),
        grid=(bsz, nt),
        in_specs=[pl.BlockSpec((tc, width), lambda b, c: (rb0 + b * nt + c, 0)),
                  pl.BlockSpec((1, 1, nstate), lambda b, c: (b, 0, 0)),
                  pl.BlockSpec((1, 1, nstate), lambda b, c: (b, 0, 0)),
                  full((1, nstate)), full((1, nstate)),
                  full(prm["bbr"].shape), full(prm["bbi"].shape),
                  full(prm["ccr"].shape), full(prm["cci"].shape),
                  full((1, width)), full((width, width)), full((1, width))],
        out_specs=(pl.BlockSpec((tc, width), lambda b, c: (b * nt + c, 0)),
                   pl.BlockSpec((1, 1, nstate), lambda b, c: (b, 0, 0)),
                   pl.BlockSpec((1, 1, nstate), lambda b, c: (b, 0, 0))),
        out_shape=(jax.ShapeDtypeStruct((bsz * t, width), BF16),
                   jax.ShapeDtypeStruct((bsz, 1, nstate), F32),
                   jax.ShapeDtypeStruct((bsz, 1, nstate), F32)),
        scratch_shapes=[pltpu.VMEM((tc, nstate), F32), pltpu.VMEM((tc, nstate), F32),
                        pltpu.VMEM((1, nstate), F32), pltpu.VMEM((1, nstate), F32)],
        compiler_params=_cparams(("parallel", "arbitrary"), vmem),
        name="s5_mixer",
    )(z, h0_re.reshape(bsz, 1, nstate), h0_im.reshape(bsz, 1, nstate), prm["lr"], prm["li"],
      prm["bbr"], prm["bbi"], prm["ccr"], prm["cci"], prm["d"], prm["wglu"], prm["bglu"])
    return y, hr, hi


def s5_params(a_re, a_im, log_dt, b_re, b_im, c_re, c_im, d_skip, w_glu, b_glu):
    g, p = a_re.shape
    nch = b_re.shape[2]
    dt = jnp.exp(log_dt.astype(F32))[:, None]
    ar, ai = a_re.astype(F32), a_im.astype(F32)
    mag = jnp.exp(ar * dt)
    lr = mag * jnp.cos(ai * dt)
    li = mag * jnp.sin(ai * dt)
    den = ar * ar + ai * ai
    fr = ((lr - 1.0) * ar + li * ai) / den
    fi = (li * ar - (lr - 1.0) * ai) / den
    br, bi = b_re.astype(F32), b_im.astype(F32)
    bbr = fr[..., None] * br - fi[..., None] * bi
    bbi = fr[..., None] * bi + fi[..., None] * br
    gb = S5_BLOCK_GROUPS
    nblk = g // gb
    eye = jnp.eye(gb, dtype=F32)

    def blk_in(m):
        m = m.reshape(nblk, gb, p, nch)
        return jnp.einsum("kgpn,gh->kgnhp", m, eye).reshape(nblk, gb * nch, gb * p).astype(BF16)

    def blk_out(m):
        m = m.astype(F32).reshape(nblk, gb, nch, p)
        return jnp.einsum("kgnp,gh->kgphn", m, eye).reshape(nblk, gb * p, gb * nch).astype(BF16)

    width = g * nch
    return dict(lr=lr.reshape(1, g * p), li=li.reshape(1, g * p), bbr=blk_in(bbr), bbi=blk_in(bbi),
                ccr=blk_out(c_re), cci=blk_out(c_im), d=d_skip.astype(F32).reshape(1, width),
                wglu=w_glu.astype(BF16), bglu=b_glu.astype(F32).reshape(1, width))


MLA_ROWS = 2048
MLA_TQ = 512
MLA_CHAIN_ROWS = 256


def _mla_kernel(q_ref, ckv_ref, kpe_ref, wuk_ref, wuv_ref, o_ref, qa_s, qpe_s, m_s, l_s, acc_s,
                *, tq, tk, nk_total, klen, causal, scale, hps):
    i = pl.program_id(1)
    nope = wuk_ref.shape[1]
    qw = q_ref.shape[1] // hps
    vdim = wuv_ref.shape[2]
    rows_all = hps * tq
    rc = MLA_CHAIN_ROWS if rows_all % MLA_CHAIN_ROWS == 0 else rows_all
    for hh in range(hps):
        rows = slice(hh * tq, (hh + 1) * tq)
        qa = jnp.dot(q_ref[:, hh * qw:hh * qw + nope], wuk_ref[hh], preferred_element_type=F32)
        qa_s[rows, :] = qa.astype(qa_s.dtype)
        qpe_s[rows, :] = q_ref[:, hh * qw + nope:(hh + 1) * qw]
    m_s[...] = jnp.full(m_s.shape, NEG_INF, F32)
    l_s[...] = jnp.zeros(l_s.shape, F32)
    acc_s[...] = jnp.zeros(acc_s.shape, F32)

    def tile(j, masked):
        ks = pl.multiple_of(j * tk, tk)
        ckv = ckv_ref[0, pl.ds(ks, tk), :]
        kpe = kpe_ref[0, pl.ds(ks, tk), :]
        scores = []
        for c0 in range(0, rows_all, rc):
            rows = slice(c0, c0 + rc)
            s = lax.dot_general(qa_s[rows, :].astype(BF16), ckv, (((1,), (1,)), ((), ())), preferred_element_type=F32)
            s = s + lax.dot_general(qpe_s[rows, :], kpe, (((1,), (1,)), ((), ())), preferred_element_type=F32)
            scores.append(s * scale)
        for c0, s in zip(range(0, rows_all, rc), scores):
            rows = slice(c0, c0 + rc)
            if masked:
                kpos = ks + lax.broadcasted_iota(jnp.int32, (rc, tk), 1)
                if causal:
                    qpos = i * tq + (c0 + lax.broadcasted_iota(jnp.int32, (rc, tk), 0)) % tq
                    s = jnp.where(kpos // CHUNK <= qpos // CHUNK, s, NEG_INF)
                if klen < nk_total * tk:
                    s = jnp.where(kpos < klen, s, NEG_INF)
            m_prev = m_s[rows, :]
            m_new = jnp.maximum(m_prev, jnp.max(s, axis=1, keepdims=True))
            alpha = jnp.exp(m_prev - m_new)
            p = jnp.exp(s - m_new)
            l_s[rows, :] = alpha * l_s[rows, :] + jnp.sum(p, axis=1, keepdims=True)
            acc_s[rows, :] = alpha * acc_s[rows, :] + jnp.dot(p.astype(BF16), ckv, preferred_element_type=F32)
            m_s[rows, :] = m_new

    def full_tile(j, c):
        tile(j, False)
        return c

    n_full = i if causal else nk_total - 1
    lax.fori_loop(0, n_full, full_tile, 0)
    tile(n_full, True)
    o_lat = (acc_s[...] / l_s[...]).astype(BF16)
    for hh in range(hps):
        o_ref[:, hh * vdim:(hh + 1) * vdim] = jnp.dot(
            o_lat[hh * tq:(hh + 1) * tq], wuv_ref[hh], preferred_element_type=F32).astype(o_ref.dtype)


def mla_attention(q_all, row_off, bsz, t, ckv, kpe, wuk, wuv, klen, causal, scale):
    nheads, nope, lat = wuk.shape
    vdim = wuv.shape[2]
    qw = q_all.shape[1] // nheads
    tkeys = ckv.shape[1]
    tq = _pick(t, (MLA_TQ, 256, 128, 64, 32, 16, 8))
    tk = tq if causal else _pick(tkeys, (256, 128))
    if causal:
        assert tq % CHUNK == 0 and tkeys == t
    hps = _pick(nheads, tuple(c for c in (24, 16, 12, 8, 6, 4, 3, 2, 1) if c * tq <= MLA_ROWS))
    nq = t // tq
    rb0 = row_off // tq
    rows_all = hps * tq
    vmem = (2 * tkeys * (lat + LANES) * 2 + 4 * tq * hps * qw * 2 + 3 * rows_all * lat * 4 + 8 * rows_all * tk * 4
            + 4 * hps * (nope + vdim) * lat * 2 + (2 << 20))
    return pl.pallas_call(
        functools.partial(_mla_kernel, tq=tq, tk=tk, nk_total=tkeys // tk, klen=klen, causal=causal, scale=scale,
                          hps=hps),
        grid=(bsz, nq, nheads // hps),
        in_specs=[pl.BlockSpec((tq, hps * qw), lambda b, i, h: (rb0 + b * nq + i, h)),
                  pl.BlockSpec((1, tkeys, lat), lambda b, i, h: (b, 0, 0)),
                  pl.BlockSpec((1, tkeys, LANES), lambda b, i, h: (b, 0, 0)),
                  pl.BlockSpec((hps, nope, lat), lambda b, i, h: (h, 0, 0)),
                  pl.BlockSpec((hps, lat, vdim), lambda b, i, h: (h, 0, 0))],
        out_specs=pl.BlockSpec((tq, hps * vdim), lambda b, i, h: (b * nq + i, h)),
        out_shape=jax.ShapeDtypeStruct((bsz * t, nheads * vdim), BF16),
        scratch_shapes=[pltpu.VMEM((rows_all, lat), F32), pltpu.VMEM((rows_all, LANES), BF16),
                        pltpu.VMEM((rows_all, 1), F32), pltpu.VMEM((rows_all, 1), F32),
                        pltpu.VMEM((rows_all, lat), F32)],
        compiler_params=_cparams(("parallel", "parallel", "arbitrary"), vmem),
        name="mla_attention",
    )(q_all, ckv, kpe, wuk, wuv)


def _fox_gate_kernel(pre_ref, zf_ref, bf_ref, logf_ref, cum_ref, cumt_ref, lf_s, *, npre, t, blk):
    total = lf_s.shape[0]
    z = zf_ref[0] + bf_ref[...]
    logf = jnp.minimum(z, 0.0) - jnp.log1p(jnp.exp(-jnp.abs(z)))
    logf_ref[0] = logf
    if npre + t < total:
        lf_s[...] = jnp.zeros(lf_s.shape, F32)
    if npre:
        lf_s[0:npre, :] = pre_ref[0]
    lf_s[npre:npre + t, :] = logf
    tri = (lax.broadcasted_iota(jnp.int32, (blk, blk), 1)
           <= lax.broadcasted_iota(jnp.int32, (blk, blk), 0)).astype(F32)
    carry = jnp.zeros((1, LANES), F32)
    for c in range(total // blk):
        rows = slice(c * blk, (c + 1) * blk)
        cum = jnp.dot(tri, lf_s[rows, :], preferred_element_type=F32, precision=lax.Precision.HIGHEST) + carry
        cum_ref[0, rows, :] = cum
        cumt_ref[0, c] = cum.T[:cumt_ref.shape[2], :]
        carry = cum[blk - 1:blk, :]


def fox_gate(zf, row_off, bsz, t, b_f, pre):
    nheads = b_f.shape[0]
    npre = 0 if pre is None else pre.shape[1]
    total = -(-(npre + t) // LANES) * LANES
    if pre is None:
        pre = jnp.zeros((bsz, 8, LANES), F32)
    pp = pre.shape[1]
    hrows = -(-nheads // 8) * 8
    zf3 = zf[row_off:row_off + bsz * t].reshape(bsz, t, LANES)
    bfp = jnp.zeros((1, LANES), F32).at[0, :nheads].set(b_f.astype(F32))
    return pl.pallas_call(
        functools.partial(_fox_gate_kernel, npre=npre, t=t, blk=LANES),
        grid=(bsz,),
        in_specs=[pl.BlockSpec((1, pp, LANES), lambda b: (b, 0, 0)),
                  pl.BlockSpec((1, t, LANES), lambda b: (b, 0, 0)),
                  pl.BlockSpec((1, LANES), lambda b: (0, 0))],
        out_specs=(pl.BlockSpec((1, t, LANES), lambda b: (b, 0, 0)),
                   pl.BlockSpec((1, total, LANES), lambda b: (b, 0, 0)),
                   pl.BlockSpec((1, total // LANES, hrows, LANES), lambda b: (b, 0, 0, 0))),
        out_shape=(jax.ShapeDtypeStruct((bsz, t, LANES), F32),
                   jax.ShapeDtypeStruct((bsz, total, LANES), F32),
                   jax.ShapeDtypeStruct((bsz, total // LANES, hrows, LANES), F32)),
        scratch_shapes=[pltpu.VMEM((total, LANES), F32)],
        compiler_params=_cparams(("parallel",), 12 * total * LANES * 4),
        name="fox_gate",
    )(pre, zf3, bfp)


FOX_HEADS_PER_STEP = 8
FOX_TQ = 512


def _fox_attn_kernel(q_ref, k_ref, v_ref, cq_ref, ck_ref, o_ref, m_s, l_s, acc_s,
                     *, tq, tk, qoff, scale, hps):
    hg = pl.program_id(1)
    i = pl.program_id(2)
    hd = acc_s.shape[2]
    lane = lax.broadcasted_iota(jnp.int32, cq_ref.shape[1:], 1)
    cq = cq_ref[0]
    fq = [jnp.sum(jnp.where(lane == hg * hps + hh, cq, 0.0), axis=1, keepdims=True) for hh in range(hps)]
    m_s[...] = jnp.full(m_s.shape, NEG_INF, F32)
    l_s[...] = jnp.zeros(l_s.shape, F32)
    acc_s[...] = jnp.zeros(acc_s.shape, F32)

    def tile(j, masked):
        ks = pl.multiple_of(j * tk, tk)
        scores = []
        for hh in range(hps):
            cols = slice(hh * hd, (hh + 1) * hd)
            k = k_ref[pl.ds(ks, tk), cols]
            fk = jnp.concatenate([ck_ref[0, j * (tk // LANES) + c, pl.ds(hg * hps + hh, 1), :]
                                  for c in range(tk // LANES)], axis=1)
            s = lax.dot_general(q_ref[:, cols], k, (((1,), (1,)), ((), ())), preferred_element_type=F32) * scale
            scores.append(s + fq[hh] - fk)
        for hh, s in enumerate(scores):
            cols = slice(hh * hd, (hh + 1) * hd)
            if masked:
                qpos = qoff + i * tq + lax.broadcasted_iota(jnp.int32, (tq, tk), 0)
                kpos = ks + lax.broadcasted_iota(jnp.int32, (tq, tk), 1)
                s = jnp.where(kpos <= qpos, s, NEG_INF)
            m_prev = m_s[hh]
            m_new = jnp.maximum(m_prev, jnp.max(s, axis=1, keepdims=True))
            alpha = jnp.exp(m_prev - m_new)
            p = jnp.exp(s - m_new)
            l_s[hh] = alpha * l_s[hh] + jnp.sum(p, axis=1, keepdims=True)
            v = v_ref[pl.ds(ks, tk), cols]
            acc_s[hh] = alpha * acc_s[hh] + jnp.dot(p.astype(BF16), v, preferred_element_type=F32)
            m_s[hh] = m_new

    def full_tile(j, c):
        tile(j, False)
        return c

    n_full = (qoff + i * tq) // tk
    lax.fori_loop(0, n_full, full_tile, 0)
    tile(n_full, True)
    for hh in range(hps):
        o_ref[:, hh * hd:(hh + 1) * hd] = (acc_s[hh] / l_s[hh]).astype(o_ref.dtype)


def fox_attention(q_all, row_off, bsz, t, k, v, tkeys, cum, cumt, qoff, scale):
    hd = LANES
    nheads = q_all.shape[1] // hd
    hps = _pick(nheads, (FOX_HEADS_PER_STEP, 2, 1))
    tq = _pick(t, (FOX_TQ, 256, 128, 64, 32, 16, 8))
    tk = tkeys if tq * tkeys * 4 <= (256 << 10) else _pick(tkeys, (tq, 256, 128))
    assert tk % tq == 0 and qoff % tq == 0 and qoff + t <= tkeys
    nq = t // tq
    rb0 = row_off // tq
    cq0 = qoff // tq
    vmem = (8 * tkeys * hps * hd * 2 + 2 * cumt.shape[2] * tkeys * 4 + 4 * hps * tq * tk * 4
            + 8 * hps * tq * hd * 4 + (2 << 20))
    return pl.pallas_call(
        functools.partial(_fox_attn_kernel, tq=tq, tk=tk, qoff=qoff, scale=scale, hps=hps),
        grid=(bsz, nheads // hps, nq),
        in_specs=[pl.BlockSpec((tq, hps * hd), lambda b, h, i: (rb0 + b * nq + i, h)),
                  pl.BlockSpec((tkeys, hps * hd), lambda b, h, i: (b, h)),
                  pl.BlockSpec((tkeys, hps * hd), lambda b, h, i: (b, h)),
                  pl.BlockSpec((1, tq, LANES), lambda b, h, i: (b, cq0 + i, 0)),
                  pl.BlockSpec((1,) + cumt.shape[1:], lambda b, h, i: (b, 0, 0, 0))],
        out_specs=pl.BlockSpec((tq, hps * hd), lambda b, h, i: (b * nq + i, h)),
        out_shape=jax.ShapeDtypeStruct((bsz * t, nheads * hd), BF16),
        scratch_shapes=[pltpu.VMEM((hps, tq, 1), F32), pltpu.VMEM((hps, tq, 1), F32), pltpu.VMEM((hps, tq, hd), F32)],
        compiler_params=_cparams(("parallel", "parallel", "arbitrary"), vmem),
        name="fox_attention",
    )(q_all, k, v, cum, cumt)


FOX_DECODE_HEADS = 8
FOX_DECODE_CHUNK = 512


def _fox_decode_kernel(q_ref, pk_ref, pv_ref, nk_ref, nv_ref, cq_ref, fkp_ref, fkn_ref, o_ref, *, t, chunk, scale):
    hg = pl.program_id(1)
    g = pk_ref.shape[2]
    hd = pk_ref.shape[3]
    rows = g * t
    nt_dims = (((1,), (1,)), ((), ()))
    stack = lambda ref: jnp.concatenate([ref[:, h * hd:(h + 1) * hd] for h in range(g)], axis=0)
    q, kn, vn = stack(q_ref), stack(nk_ref), stack(nv_ref)
    lane = lax.broadcasted_iota(jnp.int32, cq_ref.shape[1:], 1)
    cq = cq_ref[0]
    fq = jnp.concatenate([jnp.sum(jnp.where(lane == hg * g + h, cq, 0.0), axis=1, keepdims=True)
                          for h in range(g)], axis=0)

    def update(carry, s, v):
        m, l, acc = carry
        m_new = jnp.maximum(m, jnp.max(s, axis=1, keepdims=True))
        alpha = jnp.exp(m - m_new)
        p = jnp.exp(s - m_new)
        return (m_new, alpha * l + jnp.sum(p, axis=1, keepdims=True),
                alpha * acc + jnp.dot(p.astype(BF16), v, preferred_element_type=F32))

    carry = (jnp.full((rows, 1), NEG_INF, F32), jnp.zeros((rows, 1), F32), jnp.zeros((rows, hd), F32))
    cols = chunk * g
    same_head = (lax.broadcasted_iota(jnp.int32, (rows, cols), 0) // t
                 == lax.broadcasted_iota(jnp.int32, (rows, cols), 1) % g)
    for c in range(pk_ref.shape[1] // chunk):
        kp = pk_ref[0, c * chunk:(c + 1) * chunk].reshape(cols, hd).astype(BF16)
        vp = pv_ref[0, c * chunk:(c + 1) * chunk].reshape(cols, hd).astype(BF16)
        s = lax.dot_general(q, kp, nt_dims, preferred_element_type=F32) * scale
        s = s + fq - fkp_ref[0, 0, :, c * cols:(c + 1) * cols]
        carry = update(carry, jnp.where(same_head, s, NEG_INF), vp)
    rr = lax.broadcasted_iota(jnp.int32, (rows, rows), 0)
    cc = lax.broadcasted_iota(jnp.int32, (rows, rows), 1)
    s = lax.dot_general(q, kn, nt_dims, preferred_element_type=F32) * scale + fq - fkn_ref[0, 0]
    s = jnp.where(rr // t == cc // t, jnp.where(cc % t <= rr % t, s, NEG_INF), NEG_INF)
    _, l, acc = update(carry, s, vn)
    out = acc / l
    for h in range(g):
        o_ref[:, h * hd:(h + 1) * hd] = out[h * t:(h + 1) * t].astype(o_ref.dtype)


def fox_decode_attention(q_all, row_off, bsz, t, past_k, past_v, new_k, new_v, cum, scale):
    _, npast, nheads, hd = past_k.shape
    g = FOX_DECODE_HEADS
    assert nheads % g == 0 and hd == LANES and npast % t == 0
    ng = nheads // g
    chunk = _pick(npast, (FOX_DECODE_CHUNK, 256, 128, 64, 32, 16, 8))
    rb0 = row_off // t
    fkp = jnp.transpose(cum[:, :npast, :nheads].reshape(bsz, npast, ng, g), (0, 2, 1, 3)).reshape(bsz, ng, 1, npast * g)
    fkn = jnp.transpose(cum[:, npast:npast + t, :nheads].reshape(bsz, t, ng, g), (0, 2, 3, 1)).reshape(bsz, ng, 1, g * t)
    vmem = 4 * npast * g * hd * 4 + 8 * g * t * chunk * g * 4 + 4 * chunk * g * hd * 2 + (4 << 20)
    return pl.pallas_call(
        functools.partial(_fox_decode_kernel, t=t, chunk=chunk, scale=scale),
        grid=(bsz, ng),
        in_specs=[pl.BlockSpec((t, g * hd), lambda b, h: (rb0 + b, h)),
                  pl.BlockSpec((1, npast, g, hd), lambda b, h: (b, 0, h, 0)),
                  pl.BlockSpec((1, npast, g, hd), lambda b, h: (b, 0, h, 0)),
                  pl.BlockSpec((t, g * hd), lambda b, h: (b, h)),
                  pl.BlockSpec((t, g * hd), lambda b, h: (b, h)),
                  pl.BlockSpec((1, t, LANES), lambda b, h: (b, npast // t, 0)),
                  pl.BlockSpec((1, 1, 1, npast * g), lambda b, h: (b, h, 0, 0)),
                  pl.BlockSpec((1, 1, 1, g * t), lambda b, h: (b, h, 0, 0))],
        out_specs=pl.BlockSpec((t, g * hd), lambda b, h: (b, h)),
        out_shape=jax.ShapeDtypeStruct((bsz * t, nheads * hd), BF16),
        compiler_params=_cparams(("parallel", "parallel"), vmem),
        name="fox_decode_attention",
    )(q_all, past_k, past_v, new_k, new_v, cum, fkp, fkn)


def _rwkv_prep_kernel(z_ref, zp_ref, sh_ref, mu_ref, w0_ref, a0_ref, w2_ref, a2_ref, g2_ref,
                      r_ref, k_ref, v_ref, lw_ref, a_ref, g_ref, *, w):
    i = pl.program_id(1)
    z = z_ref[...]
    prev_row = jnp.where(i == 0, sh_ref[0], zp_ref[7:8, :])
    row = lax.broadcasted_iota(jnp.int32, z.shape, 0)
    z_prev = jnp.where(row == 0, prev_row, pltpu.roll(z, 1, 0))
    zm = z + (z_prev - z) * mu_ref[...]
    slab = zm[:, 3 * w:]
    lora_w = jnp.dot(jnp.tanh(slab).astype(BF16), w2_ref[...], preferred_element_type=F32)
    x = -(w0_ref[...] + lora_w)
    softplus = jnp.maximum(x, 0.0) + jnp.log1p(jnp.exp(-jnp.abs(x)))
    lw = -jnp.exp(-softplus - 0.5)
    a = jax.nn.sigmoid(a0_ref[...] + jnp.dot(slab.astype(BF16), a2_ref[...], preferred_element_type=F32))
    g = jnp.dot(jax.nn.sigmoid(slab).astype(BF16), g2_ref[...], preferred_element_type=F32)
    nh, hd = r_ref.shape[1], r_ref.shape[3]
    for ref, val in ((r_ref, zm[:, 0:w]), (k_ref, zm[:, w:2 * w]), (v_ref, zm[:, 2 * w:3 * w]),
                     (lw_ref, lw), (a_ref, a), (g_ref, g)):
        for hh in range(nh):
            ref[0, hh] = val[:, hh * hd:(hh + 1) * hd]


def rwkv_prep(zr, row_off, bsz, t, shift_prev, prm, nh):
    wz = zr.shape[1]
    w = prm["w0"].shape[1]
    hd = w // nh
    tt = _pick(t, (128, 64, 32, 16, 8))
    nt = t // tt
    rb0 = row_off // tt
    full = lambda shape: pl.BlockSpec(shape, lambda b, i: (0,) * len(shape))
    ospec = pl.BlockSpec((1, nh, tt, hd), lambda b, i: (b, 0, i, 0))
    oshape = jax.ShapeDtypeStruct((bsz, nh, t, hd), F32)
    ls = wz - 3 * w
    vmem = 6 * tt * wz * 4 + 12 * tt * nh * LANES * 4 + 6 * ls * w * 2 + 8 * tt * w * 4
    return pl.pallas_call(
        functools.partial(_rwkv_prep_kernel, w=w),
        grid=(bsz, nt),
        in_specs=[pl.BlockSpec((tt, wz), lambda b, i: (rb0 + b * nt + i, 0)),
                  pl.BlockSpec((8, wz), lambda b, i: (jnp.maximum((row_off + (b * nt + i) * tt) // 8 - 1, 0), 0)),
                  pl.BlockSpec((1, 1, wz), lambda b, i: (b, 0, 0)),
                  full((1, wz)), full((1, w)), full((1, w)), full((ls, w)), full((ls, w)), full((ls, w))],
        out_specs=(ospec,) * 6,
        out_shape=(oshape,) * 6,
        compiler_params=_cparams(("parallel", "arbitrary"), vmem),
        name="rwkv_prep",
    )(zr, zr, shift_prev.reshape(bsz, 1, wz), prm["mu"], prm["w0"], prm["a0"], prm["w2"], prm["a2"], prm["g2"])


def _bdot(a, b, dims):
    return lax.dot_general(a.astype(BF16), b.astype(BF16), dims, preferred_element_type=F32)


def _cumsum_rows(tri, x):
    hi = x.astype(BF16)
    r1 = x - hi.astype(F32)
    mid = r1.astype(BF16)
    lo = (r1 - mid.astype(F32)).astype(BF16)
    t = tri.astype(BF16)
    dot = lambda p: lax.dot_general(t, p, _NN, preferred_element_type=F32)
    return dot(hi) + dot(mid) + dot(lo)


_NT = (((2,), (2,)), ((0,), (0,)))
_NN = (((2,), (1,)), ((0,), (0,)))
_TN = (((1,), (1,)), ((0,), (0,)))


def _rwkv_scan_kernel(r_ref, k_ref, v_ref, lw_ref, a_ref, g_ref, s0_ref, kk_ref, ka_ref, rk_ref,
                      lnw_ref, lnb_ref, y_ref, sout_ref, st_s, *, chunk):
    c = pl.program_id(2)

    @pl.when(c == 0)
    def _():
        st_s[...] = s0_ref[0]

    r = r_ref[0]
    k = k_ref[0]
    v = v_ref[0]
    lw = lw_ref[0]
    a = a_ref[0]
    hb = r.shape[0]
    s0 = st_s[...]

    kk = k * kk_ref[...]
    kk = kk / jnp.maximum(jnp.sqrt(jnp.sum(kk * kk, axis=-1, keepdims=True)), 1e-12)
    kmod = k * (1.0 + (a - 1.0) * ka_ref[...])

    li = lax.broadcasted_iota(jnp.int32, (chunk, chunk), 0)
    mi = lax.broadcasted_iota(jnp.int32, (chunk, chunk), 1)
    incl = (mi <= li).astype(F32)
    strict = (mi < li).astype(F32)
    cs = _cumsum_rows(jnp.broadcast_to(incl, (hb, chunk, chunk)), lw)
    dec_in = jnp.exp(cs)
    dec_ex = jnp.exp(cs - lw)
    inv = jnp.exp(-cs)
    p_rows = jnp.concatenate([-kk * dec_ex, r * dec_in], axis=1)
    q_rows = jnp.concatenate([kk * a * inv, kmod * inv], axis=1)
    mm = _bdot(p_rows, q_rows, _NT)
    a_ab = mm[:, :chunk, :chunk] * strict
    a_ak = mm[:, :chunk, chunk:] * strict
    r_b = mm[:, chunk:, :chunk] * incl
    r_k = mm[:, chunk:, chunk:] * incl
    ps = _bdot(p_rows, s0, _NT)
    x = ps[:, :chunk] + _bdot(a_ak, v, _NN)
    pw = a_ab
    n_iter = chunk.bit_length() - 1
    for it in range(n_iter):
        x = x + _bdot(pw, x, _NN)
        if it + 1 < n_iter:
            pw = _bdot(pw, pw, _NN)
    y = ps[:, chunk:] + _bdot(r_b, x, _NN) + _bdot(r_k, v, _NN)
    uv = jnp.concatenate([x, v], axis=1)
    s_new = (s0 + _bdot(uv, q_rows, _TN)) * dec_in[:, chunk - 1:chunk, :]
    st_s[...] = s_new
    sout_ref[0] = s_new

    mean = jnp.mean(y, axis=-1, keepdims=True)
    var = jnp.mean(jnp.square(y - mean), axis=-1, keepdims=True)
    yn = (y - mean) * lax.rsqrt(var + RWKV_LN_EPS) * lnw_ref[...] + lnb_ref[...]
    bonus = jnp.sum(r * kmod * rk_ref[...], axis=-1, keepdims=True) * v
    out = (yn + bonus) * g_ref[0]
    hd = out.shape[2]
    for hh in range(hb):
        y_ref[:, hh * hd:(hh + 1) * hd] = out[hh].astype(y_ref.dtype)


def rwkv_scan(r, k, v, lw, a, g, s0, prm):
    bsz, nh, t, hd = r.shape
    chunk = _pick(t, (64, 32, 16, 8))
    hb = _pick(nh, (32, 16, 8, 4, 2, 1))
    nc = t // chunk
    xspec = pl.BlockSpec((1, hb, chunk, hd), lambda b, h, c: (b, h, c, 0))
    pspec = pl.BlockSpec((hb, 1, hd), lambda b, h, c: (h, 0, 0))
    sspec = pl.BlockSpec((1, hb, hd, hd), lambda b, h, c: (b, h, 0, 0))
    vmem = 16 * hb * chunk * LANES * 4 + 40 * hb * 2 * chunk * LANES * 4 + 6 * hb * hd * LANES * 4
    y, s = pl.pallas_call(
        functools.partial(_rwkv_scan_kernel, chunk=chunk),
        grid=(bsz, nh // hb, nc),
        in_specs=[xspec] * 6 + [sspec] + [pspec] * 5,
        out_specs=(pl.BlockSpec((chunk, hb * hd), lambda b, h, c: (b * nc + c, h)), sspec),
        out_shape=(jax.ShapeDtypeStruct((bsz * t, nh * hd), BF16), jax.ShapeDtypeStruct((bsz, nh, hd, hd), F32)),
        scratch_shapes=[pltpu.VMEM((hb, hd, hd), F32)],
        compiler_params=_cparams(("parallel", "parallel", "arbitrary"), vmem),
        name="rwkv_scan",
    )(r, k, v, lw, a, g, s0, prm["k_k"], prm["k_a"], prm["r_k"], prm["ln_w"], prm["ln_b"])
    return y, s


def _router_kernel(x_ref, w_ref, comb_ref, *, n_experts):
    logits = jnp.dot(x_ref[...].astype(BF16), w_ref[...], preferred_element_type=F32)
    lane = lax.broadcasted_iota(jnp.int32, logits.shape, 1)
    big = jnp.int32(LANES)
    logits = jnp.where(lane < n_experts, logits, -jnp.inf)
    t1 = jnp.max(logits, axis=1, keepdims=True)
    i1 = jnp.min(jnp.where(logits == t1, lane, big), axis=1, keepdims=True)
    rest = jnp.where(lane == i1, -jnp.inf, logits)
    t2 = jnp.max(rest, axis=1, keepdims=True)
    i2 = jnp.min(jnp.where(rest == t2, lane, big), axis=1, keepdims=True)
    e2 = jnp.exp(t2 - t1)
    den = 1.0 + e2
    comb_ref[...] = jnp.where(lane == 0, 1.0 / den, jnp.where(lane == 1, e2 / den, jnp.where(
        lane == 2, i1.astype(F32), jnp.where(lane == 3, i2.astype(F32), 0.0))))


def moe_router(x, w_router):
    m, d = x.shape
    ne = w_router.shape[1]
    wp = jnp.zeros((d, LANES), BF16).at[:, :ne].set(w_router.astype(BF16))
    tm = _pick(m, (640, 512, 320, 256, 128, 64, 32, 16, 8))
    return pl.pallas_call(
        functools.partial(_router_kernel, n_experts=ne),
        grid=(m // tm,),
        in_specs=[pl.BlockSpec((tm, d), lambda i: (i, 0)), pl.BlockSpec((d, LANES), lambda i: (0, 0))],
        out_specs=pl.BlockSpec((tm, LANES), lambda i: (i, 0)),
        out_shape=jax.ShapeDtypeStruct((m, LANES), F32),
        compiler_params=_cparams(("parallel",), 3 * tm * d * 4 + 2 * d * LANES * 2 + 8 * tm * LANES * 4),
        name="moe_router",
    )(x, wp)


MOE_TILE = 512


def _moe_plan(rout, n_exp, tile):
    n = rout.shape[0]
    pair_e = rout[:, 2:4].astype(jnp.int32).reshape(-1)
    onehot = (pair_e[:, None] == jnp.arange(n_exp, dtype=jnp.int32)[None, :]).astype(jnp.int32)
    csum = jnp.cumsum(onehot, axis=0)
    rank = jnp.take_along_axis(csum - onehot, pair_e[:, None], axis=1)[:, 0]
    gsz = (csum[-1] + tile - 1) // tile * tile
    gend = jnp.cumsum(gsz)
    slot = ((gend - gsz)[pair_e] + rank).astype(jnp.int32)
    n_tiles = -(-2 * n // tile) + n_exp
    tok = jnp.zeros((n_tiles * tile,), jnp.int32).at[slot].set(jnp.arange(2 * n, dtype=jnp.int32) // 2)
    tile_start = jnp.arange(n_tiles, dtype=jnp.int32) * tile
    te = jnp.minimum(jnp.searchsorted(gend, tile_start, side="right"), n_exp - 1).astype(jnp.int32)
    used = (gend[-1] // tile).astype(jnp.int32).reshape(1)
    return slot, tok, te, used, n_tiles


def _row_copy(src_hbm, row, dst, r, sem):
    return pltpu.make_async_copy(src_hbm.at[pl.ds(row, 1)], dst.at[pl.ds(r, 1)], sem)


def _moe_gather_kernel(tok_ref, used_ref, x_hbm, o_ref, buf, sem, *, tile):
    i = pl.program_id(0)
    used = used_ref[0]

    def issue_tile(t_idx):
        slot = t_idx % 2
        base = t_idx * tile

        def issue(r, c):
            _row_copy(x_hbm, tok_ref[base + r], buf.at[slot], r, sem.at[slot]).start()
            return c

        lax.fori_loop(0, tile, issue, 0)

    @pl.when((i == 0) & (used > 0))
    def _():
        issue_tile(i)

    @pl.when(i + 1 < used)
    def _():
        issue_tile(i + 1)

    @pl.when(i < used)
    def _():
        slot = i % 2

        def wait(r, c):
            _row_copy(x_hbm, 0, buf.at[slot], r, sem.at[slot]).wait()
            return c

        lax.fori_loop(0, tile, wait, 0)
        o_ref[...] = buf[slot].astype(o_ref.dtype)

    @pl.when(i >= used)
    def _():
        o_ref[...] = jnp.zeros(o_ref.shape, o_ref.dtype)


def moe_gather(x, tok, used, n_tiles, tile):
    d = x.shape[1]
    return pl.pallas_call(
        functools.partial(_moe_gather_kernel, tile=tile),
        grid_spec=pltpu.PrefetchScalarGridSpec(
            num_scalar_prefetch=2, grid=(n_tiles,),
            in_specs=[pl.BlockSpec(memory_space=pl.ANY)],
            out_specs=pl.BlockSpec((tile, d), lambda i, tok_r, used_r: (i, 0)),
            scratch_shapes=[pltpu.VMEM((2, tile, d), x.dtype), pltpu.SemaphoreType.DMA((2,))]),
        out_shape=jax.ShapeDtypeStruct((n_tiles * tile, d), BF16),
        compiler_params=_cparams(("arbitrary",), 5 * tile * d * 4),
        name="moe_gather",
    )(tok, used, x)


def _moe_up_kernel(te_ref, used_ref, x_ref, wg_ref, wu_ref, wd_ref, o_ref, wdo_ref, wg_s, wu_s):
    i = pl.program_id(1)
    wdo_ref[...] = wd_ref[...].astype(BF16)

    @pl.when((i == 0) | (te_ref[i] != te_ref[jnp.maximum(i - 1, 0)]))
    def _():
        wg_s[...] = wg_ref[...].astype(BF16)
        wu_s[...] = wu_ref[...].astype(BF16)

    @pl.when(i < used_ref[0])
    def _():
        x = x_ref[...]
        g = jnp.dot(x, wg_s[...], preferred_element_type=F32)
        u = jnp.dot(x, wu_s[...], preferred_element_type=F32)
        o_ref[...] = (g * jax.nn.sigmoid(g) * u).astype(o_ref.dtype)

    @pl.when(i >= used_ref[0])
    def _():
        o_ref[...] = jnp.zeros(o_ref.shape, o_ref.dtype)


def moe_up(xs, wg, wu, wd, te, used, tile):
    p_rows, d = xs.shape
    n_exp, f = wg.shape[0], wg.shape[2]
    tn = _pick(f, (512, 256, 128))
    nj, ni = f // tn, p_rows // tile
    wd_rows = n_exp * f
    rows_c = next(c for c in (16, 32, 64, 128, 256, 512, 1024, 2048, 4096, wd_rows)
                  if wd_rows % c == 0 and wd_rows // c <= nj * ni)
    last_c = wd_rows // rows_c - 1
    cspec = pl.BlockSpec((rows_c, wd.shape[2]), lambda j, i, te_r, used_r: (jnp.minimum(j * ni + i, last_c), 0))
    wspec = pl.BlockSpec((None, d, tn), lambda j, i, te_r, used_r: (te_r[i], 0, j))
    vmem = 2 * tile * d * 2 + 4 * d * tn * 4 + 2 * d * tn * 2 + 6 * tile * tn * 4 + 12 * rows_c * wd.shape[2]
    act, wd_b = pl.pallas_call(
        _moe_up_kernel,
        grid_spec=pltpu.PrefetchScalarGridSpec(
            num_scalar_prefetch=2, grid=(nj, ni),
            in_specs=[pl.BlockSpec((tile, d), lambda j, i, te_r, used_r: (i, 0)), wspec, wspec, cspec],
            out_specs=(pl.BlockSpec((tile, tn), lambda j, i, te_r, used_r: (i, j)), cspec),
            scratch_shapes=[pltpu.VMEM((d, tn), BF16), pltpu.VMEM((d, tn), BF16)]),
        out_shape=(jax.ShapeDtypeStruct((p_rows, f), BF16), jax.ShapeDtypeStruct((wd_rows, wd.shape[2]), BF16)),
        compiler_params=_cparams(("arbitrary", "arbitrary"), vmem),
        name="moe_up",
    )(te, used, xs, wg, wu, wd.reshape(wd_rows, wd.shape[2]))
    return act, wd_b.reshape(wd.shape)


def _moe_down_kernel(te_ref, used_ref, x_ref, w_ref, o_ref, *, per_tile):
    i = pl.program_id(1)

    @pl.when(i < used_ref[0] * per_tile)
    def _():
        o_ref[...] = jnp.dot(x_ref[...], w_ref[...], preferred_element_type=F32)

    @pl.when(i >= used_ref[0] * per_tile)
    def _():
        o_ref[...] = jnp.zeros(o_ref.shape, o_ref.dtype)


def moe_down(act, wd, te, used, tile):
    p_rows, f = act.shape
    d = wd.shape[2]
    tn = _pick(d, (512, 256, 128))
    rows = _pick(tile, (256, 128, 64, 32, 16, 8))
    per_tile = tile // rows
    vmem = 2 * rows * f * 2 + 2 * f * tn * 2 + 4 * rows * tn * 4
    return pl.pallas_call(
        functools.partial(_moe_down_kernel, per_tile=per_tile),
        grid_spec=pltpu.PrefetchScalarGridSpec(
            num_scalar_prefetch=2, grid=(d // tn, p_rows // rows),
            in_specs=[pl.BlockSpec((rows, f), lambda j, i, te_r, used_r: (i, 0)),
                      pl.BlockSpec((None, f, tn), lambda j, i, te_r, used_r: (te_r[i // per_tile], 0, j))],
            out_specs=pl.BlockSpec((rows, tn), lambda j, i, te_r, used_r: (i, j))),
        out_shape=jax.ShapeDtypeStruct((p_rows, d), F32),
        compiler_params=_cparams(("parallel", "arbitrary"), vmem),
        name="moe_down",
    )(te, used, act, wd)


def _moe_combine_kernel(slot_ref, h_ref, g_ref, ys_hbm, fn_ref, o_ref, buf, sem, *, tc, rb0, nsteps):
    i = pl.program_id(0)

    def issue_tile(t_idx):
        bs = t_idx % 2
        base = (rb0 + t_idx) * tc

        def issue(r, c):
            p = 2 * (base + r)
            _row_copy(ys_hbm, slot_ref[p], buf.at[bs, 0], r, sem.at[bs]).start()
            _row_copy(ys_hbm, slot_ref[p + 1], buf.at[bs, 1], r, sem.at[bs]).start()
            return c

        lax.fori_loop(0, tc, issue, 0)

    @pl.when(i == 0)
    def _():
        issue_tile(i)

    @pl.when(i + 1 < nsteps)
    def _():
        issue_tile(i + 1)

    bs = i % 2

    def wait(r, c):
        _row_copy(ys_hbm, 0, buf.at[bs, 0], r, sem.at[bs]).wait()
        _row_copy(ys_hbm, 0, buf.at[bs, 1], r, sem.at[bs]).wait()
        return c

    lax.fori_loop(0, tc, wait, 0)
    g = g_ref[...]
    x = h_ref[...] + (g[:, 0:1] * buf[bs, 0] + g[:, 1:2] * buf[bs, 1])
    y = x * lax.rsqrt(jnp.mean(x * x, axis=-1, keepdims=True) + NORM_EPS)
    o_ref[...] = y * fn_ref[...]


def moe_combine_norm(h, rout, ys, slot, final_norm, row0, n):
    d = h.shape[1]
    tc = _pick(math.gcd(n, row0) if row0 else n, (256, 128, 64, 32, 16, 8))
    rb0 = row0 // tc
    return pl.pallas_call(
        functools.partial(_moe_combine_kernel, tc=tc, rb0=rb0, nsteps=n // tc),
        grid_spec=pltpu.PrefetchScalarGridSpec(
            num_scalar_prefetch=1, grid=(n // tc,),
            in_specs=[pl.BlockSpec((tc, d), lambda i, s: (rb0 + i, 0)),
                      pl.BlockSpec((tc, LANES), lambda i, s: (rb0 + i, 0)),
                      pl.BlockSpec(memory_space=pl.ANY),
                      pl.BlockSpec((1, d), lambda i, s: (0, 0))],
            out_specs=pl.BlockSpec((tc, d), lambda i, s: (i, 0)),
            scratch_shapes=[pltpu.VMEM((2, 2, tc, d), F32), pltpu.SemaphoreType.DMA((2,))]),
        out_shape=jax.ShapeDtypeStruct((n, d), F32),
        compiler_params=_cparams(("arbitrary",), 10 * tc * d * 4),
        name="moe_combine_norm",
    )(slot, h, rout, ys, final_norm.astype(F32).reshape(1, d))


def _rope_tables(pos, rope, width, lead):
    inv = ROPE_THETA ** (-jnp.arange(0, rope, 2, dtype=F32) / rope)
    ang = pos.astype(F32)[:, None] * inv[None, :]
    cos, sin = jnp.cos(ang), jnp.sin(ang)
    n = pos.shape[0]
    ctab = jnp.concatenate([jnp.ones((n, lead), F32), cos, cos, jnp.zeros((n, width - lead - rope), F32)], axis=1)
    stab = jnp.concatenate([jnp.zeros((n, lead), F32), -sin, sin, jnp.zeros((n, width - lead - rope), F32)], axis=1)
    return ctab, stab


def _swap_halves(w):
    half = w.shape[-1] // 2
    return jnp.concatenate([w[..., half:], w[..., :half]], axis=-1)


def kernel(x_prompt, x_sample, cache_mla_ckv, cache_mla_kpe, state_s5_re, state_s5_im, state_rwkv_wkv, state_rwkv_shift, cache_fox_k, cache_fox_v, cache_fox_logf, ln0_mix, w_in0, s5_a_re, s5_a_im, s5_log_dt, s5_b_re, s5_b_im, s5_c_re, s5_c_im, s5_d, s5_w_glu, s5_b_glu, mla_q_norm, mla_w_q_up, mla_kv_norm, mla_w_uk, mla_w_uv, w_out0, ln0_ffn, ffn_w_gate, ffn_w_up, ffn_w_down, ln1_mix, w_in1, rwkv_mu, rwkv_w0, rwkv_w2, rwkv_a0, rwkv_a2, rwkv_g2, rwkv_k_k, rwkv_k_a, rwkv_r_k, rwkv_ln_w, rwkv_ln_b, fox_b_f, w_out1, ln1_ffn, moe_w_router, moe_w_gate, moe_w_up, moe_w_down, final_norm):
    bp, tp, d = x_prompt.shape
    bs, ts, _ = x_sample.shape
    past = cache_mla_ckv.shape[1]
    n_p, n_s = bp * tp, bs * ts
    ntok = n_p + n_s
    streams = ((0, bp, tp), (n_p, bs, ts))

    h = jnp.concatenate([x_prompt.reshape(n_p, d), x_sample.reshape(n_s, d)], axis=0)

    s5_w = s5_d.shape[0]
    q_rank = mla_q_norm.shape[0]
    kv_rank = mla_kv_norm.shape[0]
    n_mh, qk = mla_w_q_up.shape[1], mla_w_q_up.shape[2]
    nope = mla_w_uk.shape[2]
    rope = qk - nope
    vdim = mla_w_uv.shape[2]
    mla_scale = float(qk) ** -0.5
    qw = 2 * LANES
    kvw = kv_rank + LANES

    (xn,) = rmsnorm(h, ln0_mix, (BF16,))
    w_in0b = w_in0.astype(BF16)
    (z_uq,) = matmul(xn, w_in0b[:, :s5_w + q_rank])
    off_kv = s5_w + q_rank
    w_kpe = w_in0b[:, off_kv + kv_rank:]
    zpad = jnp.zeros((d, kvw - kv_rank - rope), BF16)
    w_kv1 = jnp.concatenate([w_in0b[:, off_kv:off_kv + kv_rank], w_kpe, zpad], axis=1)
    w_kv2 = jnp.concatenate([jnp.zeros((d, kv_rank), BF16), _swap_halves(w_kpe), zpad], axis=1)
    pos_p = jnp.arange(tp)
    pos_s = past + jnp.arange(ts)

    def token_tables(width, lead):
        cp, sp = _rope_tables(pos_p, rope, width, lead)
        cs, ss = _rope_tables(pos_s, rope, width, lead)
        return (jnp.concatenate([jnp.tile(cp, (bp, 1)), jnp.tile(cs, (bs, 1))], axis=0),
                jnp.concatenate([jnp.tile(sp, (bp, 1)), jnp.tile(ss, (bs, 1))], axis=0))

    ckv_c, ckv_s = token_tables(kvw, kv_rank)
    z_kv = rope_matmul(xn, w_kv1, w_kv2, ckv_c, ckv_s, F32, tn=kvw)
    ckv_f, ckv_b = rmsnorm(z_kv, mla_kv_norm, (F32, BF16), col_block=0, width=kv_rank)
    kpe_f = z_kv[:, kv_rank:kv_rank + rope]
    kpe_b = z_kv[:, kv_rank:].astype(BF16)

    (cqn,) = rmsnorm(z_uq, mla_q_norm, (BF16,), col_block=s5_w // q_rank, width=q_rank)
    wq = mla_w_q_up.astype(BF16)
    zq = jnp.zeros((q_rank, n_mh, qw - qk), BF16)
    wq1 = jnp.concatenate([wq, zq], axis=2).reshape(q_rank, n_mh * qw)
    wq2 = jnp.concatenate([jnp.zeros((q_rank, n_mh, nope), BF16), _swap_halves(wq[:, :, nope:]), zq],
                          axis=2).reshape(q_rank, n_mh * qw)
    q_c, q_s = token_tables(qw, nope)
    q_all = rope_matmul(cqn, wq1, wq2, q_c, q_s, BF16, tn=qw)

    s5p = s5_params(s5_a_re, s5_a_im, s5_log_dt, s5_b_re, s5_b_im, s5_c_re, s5_c_im, s5_d, s5_w_glu, s5_b_glu)
    g5, p5 = s5_a_re.shape
    wuk = jnp.transpose(mla_w_uk, (1, 2, 0)).astype(BF16)
    wuv = jnp.transpose(mla_w_uv, (1, 0, 2)).astype(BF16)

    y_s5, s5_re, s5_im, y_mla = [], [], [], []
    for si, (off, bsz, t) in enumerate(streams):
        if si == 0:
            h0r = jnp.zeros((bsz, g5 * p5), F32)
            h0i = h0r
            ckv_k = ckv_b[off:off + bsz * t].reshape(bsz, t, kv_rank)
            kpe_k = kpe_b[off:off + bsz * t].reshape(bsz, t, LANES)
            klen, causal = t, True
        else:
            h0r, h0i = state_s5_re.astype(F32), state_s5_im.astype(F32)
            klen, causal = past + t, False
            padk = -(-klen // LANES) * LANES - klen
            ckv_k = jnp.concatenate([cache_mla_ckv.astype(BF16), ckv_b[off:off + bsz * t].reshape(bsz, t, kv_rank),
                                     jnp.zeros((bsz, padk, kv_rank), BF16)], axis=1)
            kpe_cache = jnp.concatenate([cache_mla_kpe.astype(BF16),
                                         jnp.zeros((bsz, past, LANES - rope), BF16)], axis=2)
            kpe_k = jnp.concatenate([kpe_cache, kpe_b[off:off + bsz * t].reshape(bsz, t, LANES),
                                     jnp.zeros((bsz, padk, LANES), BF16)], axis=1)
        ys, hr, hi = s5_mixer(z_uq, off, bsz, t, h0r, h0i, s5p)
        y_s5.append(ys)
        s5_re.append(hr.reshape(bsz, g5, p5))
        s5_im.append(hi.reshape(bsz, g5, p5))
        y_mla.append(mla_attention(q_all, off, bsz, t, ckv_k, kpe_k, wuk, wuv, klen, causal, mla_scale))

    h = matmul_pair(jnp.concatenate(y_s5, axis=0), jnp.concatenate(y_mla, axis=0), w_out0.astype(BF16), h)
    (hn,) = rmsnorm(h, ln0_ffn, (BF16,))
    act = swiglu_up(hn, ffn_w_gate, ffn_w_up)
    (h,) = matmul(act, ffn_w_down.astype(BF16), res=h, tm=_pick(ntok, (640, 512, 256, 128, 64, 32, 16, 8)),
                  tn=256, tk=ffn_w_down.shape[0])

    rw = rwkv_w0.shape[0]
    nh_r, hd_r = rwkv_k_k.shape
    shift_w = rwkv_mu.shape[0]
    lora_w = shift_w - 3 * rw
    slab = -(-lora_w // LANES) * LANES
    wz = 3 * rw + slab
    nh_f = fox_b_f.shape[0]
    fw = (w_in1.shape[1] - shift_w - nh_f) // 3
    fox_scale = float(fw // nh_f) ** -0.5
    d_lw, d_la = rwkv_w2.shape[0], rwkv_a2.shape[0]

    (xn,) = rmsnorm(h, ln1_mix, (BF16,))
    w_in1b = w_in1.astype(BF16)
    (zr,) = matmul(xn, jnp.pad(w_in1b[:, :shift_w], ((0, 0), (0, wz - shift_w))))
    (fq,) = matmul(xn, w_in1b[:, shift_w:shift_w + fw], (BF16,))
    fk = [matmul_heads(xn, w_in1b[:, shift_w + fw:shift_w + 2 * fw], nh_f, (o_, b_ * t_)) for o_, b_, t_ in streams]
    fv = [matmul_heads(xn, w_in1b[:, shift_w + 2 * fw:shift_w + 3 * fw], nh_f, (o_, b_ * t_)) for o_, b_, t_ in streams]
    (zf,) = matmul(xn, jnp.pad(w_in1b[:, shift_w + 3 * fw:], ((0, 0), (0, LANES - nh_f))))

    padrow = lambda wgt, lo: jnp.zeros((slab, rw), BF16).at[lo:lo + wgt.shape[0]].set(wgt.astype(BF16))
    row2 = lambda x_, n_: x_.astype(F32).reshape(1, n_)
    head3 = lambda x_: x_.astype(F32).reshape(nh_r, 1, hd_r)
    rprm = dict(mu=jnp.pad(row2(rwkv_mu, shift_w), ((0, 0), (0, wz - shift_w))), w0=row2(rwkv_w0, rw),
                a0=row2(rwkv_a0, rw), w2=padrow(rwkv_w2, 0), a2=padrow(rwkv_a2, d_lw),
                g2=padrow(rwkv_g2, d_lw + d_la), k_k=head3(rwkv_k_k), k_a=head3(rwkv_k_a), r_k=head3(rwkv_r_k),
                ln_w=head3(rwkv_ln_w), ln_b=head3(rwkv_ln_b))

    y_r, wkv, shift_new, y_f, logf_out = [], [], [], [], []
    for si, (off, bsz, t) in enumerate(streams):
        if si == 0:
            shift_prev = jnp.zeros((bsz, wz), F32)
            s0 = jnp.zeros((bsz, nh_r, hd_r, hd_r), F32)
            pre = None
            k_all, v_all, tkeys = fk[si][1], fv[si][1], t
            qoff = 0
        else:
            shift_prev = jnp.pad(state_rwkv_shift.astype(F32), ((0, 0), (0, wz - shift_w)))
            s0 = state_rwkv_wkv.astype(F32)
            pre = jnp.pad(cache_fox_logf.astype(F32), ((0, 0), (0, 0), (0, LANES - nh_f)))
        parts = rwkv_prep(zr, off, bsz, t, shift_prev, rprm, nh_r)
        yr, s_fin = rwkv_scan(*parts, s0, rprm)
        y_r.append(yr)
        wkv.append(s_fin)
        shift_new.append(zr[off + t - 1:off + bsz * t:t, :shift_w])
        logf, cum, cumt = fox_gate(zf, off, bsz, t, fox_b_f, pre)
        logf_out.append(logf[:, :, :nh_f])
        if si == 0:
            y_f.append(fox_attention(fq, off, bsz, t, k_all, v_all, tkeys, cum, cumt, qoff, fox_scale))
        else:
            y_f.append(fox_decode_attention(fq, off, bsz, t, cache_fox_k.astype(F32), cache_fox_v.astype(F32),
                                            fk[si][1], fv[si][1], cum, fox_scale))

    h = matmul_pair(jnp.concatenate(y_r, axis=0), jnp.concatenate(y_f, axis=0), w_out1.astype(BF16), h)
    (hn,) = rmsnorm(h, ln1_ffn, (F32,))
    rout = moe_router(hn, moe_w_router)
    n_exp = moe_w_gate.shape[0]
    slot, tok, te, used, n_tiles = _moe_plan(rout, n_exp, MOE_TILE)
    xs = moe_gather(hn, tok, used, n_tiles, MOE_TILE)
    act, wd_b = moe_up(xs, moe_w_gate, moe_w_up, moe_w_down, te, used, MOE_TILE)
    ys = moe_down(act, wd_b, te, used, MOE_TILE)
    nfh = fw // nh_f
    outs = [moe_combine_norm(h, rout, ys, slot, final_norm, off, bsz * t).reshape(bsz, t, d) for off, bsz, t in streams]
    for si, (off, bsz, t) in enumerate(streams):
        rows = slice(off, off + bsz * t)
        outs += [ckv_f[rows].reshape(bsz, t, kv_rank), kpe_f[rows].reshape(bsz, t, rope), s5_re[si], s5_im[si],
                 wkv[si], shift_new[si], fk[si][0].reshape(bsz, t, nh_f, nfh), fv[si][0].reshape(bsz, t, nh_f, nfh),
                 logf_out[si]]
    return tuple(outs)
```
